```python
import math
import jax
import jax.numpy as jnp
from jax import lax
import numpy as np

D_MODEL = 1024
BATCH = 8
SEQ = 2048
DEPTH = 2
DEC_BATCH = 128
DEC_SEQ = 8
PAST_LEN = 8192
PAGE_SIZE = 128

SSM_GROUP = 16
SSM_GROUPS = D_MODEL // SSM_GROUP
SSM_STATE = 64
DT_MIN = 1e-3
DT_MAX = 1e-1
MLA_HEADS = 8
QK_NOPE = 128
QK_ROPE = 64
V_HEAD = 128
KV_LORA = 256
Q_LORA = 384
ROPE_BASE = 10000.0
Q_BLOCK = 128
MLA_SCALE = (QK_NOPE + QK_ROPE) ** -0.5
MEM_TOKENS = 256
MEM_HEADS = 4
MEM_HEAD_DIM = D_MODEL // MEM_HEADS
MEM_SCALE = MEM_HEAD_DIM ** -0.5
D_FF = 4 * D_MODEL
EPS = 1e-6

kernel_name = 'yoco_s5_mla_memory_decoder_step'


def rms_norm(x, g):
    xf = x.astype(jnp.float32)
    y = xf * lax.rsqrt(jnp.mean(xf * xf, axis=-1, keepdims=True) + EPS)
    return (y * g.astype(jnp.float32)).astype(x.dtype)


def rope(x, pos):
    half = x.shape[-1] // 2
    inv = ROPE_BASE ** (-jnp.arange(half, dtype=jnp.float32) / half)
    ang = pos.astype(jnp.float32)[:, None] * inv[None, :]
    shape = (ang.shape[0],) + (1,) * (x.ndim - 3) + (half,)
    cos = jnp.cos(ang).reshape(shape)
    sin = jnp.sin(ang).reshape(shape)
    xf = x.astype(jnp.float32)
    x1, x2 = xf[..., :half], xf[..., half:]
    return jnp.concatenate([x1 * cos - x2 * sin, x2 * cos + x1 * sin], axis=-1).astype(x.dtype)


def _complex_affine_combine(e1, e2):
    a1r, a1i, b1r, b1i = e1
    a2r, a2i, b2r, b2i = e2
    return (a2r * a1r - a2i * a1i,
            a2r * a1i + a2i * a1r,
            a2r * b1r - a2i * b1i + b2r,
            a2r * b1i + a2i * b1r + b2i)


def s5_mixer(u, a_re, a_im, log_dt, b_re, b_im, c_re, c_im, d_skip, w_glu, h0_re, h0_im):
    f32 = jnp.float32
    bsz, t, dm = u.shape
    ar = a_re.astype(f32)
    ai = a_im.astype(f32)
    dt = jnp.exp(log_dt.astype(f32))[:, None]
    mag = jnp.exp(ar * dt)
    abar_re = mag * jnp.cos(ai * dt)
    abar_im = mag * jnp.sin(ai * dt)
    den = ar * ar + ai * ai
    nr = abar_re - 1.0
    ni = abar_im
    coef_re = (nr * ar + ni * ai) / den
    coef_im = (ni * ar - nr * ai) / den
    br = b_re.astype(f32)
    bi = b_im.astype(f32)
    bbar_re = coef_re[..., None] * br - coef_im[..., None] * bi
    bbar_im = coef_re[..., None] * bi + coef_im[..., None] * br
    ug = u.astype(f32).reshape(bsz, t, SSM_GROUPS, SSM_GROUP)
    bu_re = jnp.einsum('gnk,btgk->btgn', bbar_re, ug)
    bu_im = jnp.einsum('gnk,btgk->btgn', bbar_im, ug)
    h0r = h0_re.astype(f32)
    h0i = h0_im.astype(f32)
    bu_re = bu_re.at[:, 0].add(abar_re * h0r - abar_im * h0i)
    bu_im = bu_im.at[:, 0].add(abar_re * h0i + abar_im * h0r)
    a_seq_re = jnp.broadcast_to(abar_re, bu_re.shape)
    a_seq_im = jnp.broadcast_to(abar_im, bu_im.shape)
    _, _, h_re, h_im = lax.associative_scan(
        _complex_affine_combine, (a_seq_re, a_seq_im, bu_re, bu_im), axis=1)
    cr = c_re.astype(f32)
    ci = c_im.astype(f32)
    y = jnp.einsum('gkn,btgn->btgk', cr, h_re) - jnp.einsum('gkn,btgn->btgk', ci, h_im)
    y = y.reshape(bsz, t, dm) + d_skip.astype(f32) * u.astype(f32)
    g = jax.nn.gelu(y, approximate=False).astype(u.dtype)
    z = g @ w_glu
    out = z[..., :dm] * jax.nn.sigmoid(z[..., dm:])
    return out, h_re[:, -1], h_im[:, -1]


def mem_kv(mem, g, w_k, w_v):
    b, m, _ = mem.shape
    mn = rms_norm(mem, g)
    k = (mn @ w_k).reshape(b, m, MEM_HEADS, MEM_HEAD_DIM)
    v = (mn @ w_v).reshape(b, m, MEM_HEADS, MEM_HEAD_DIM)
    return k, v


def mem_attend(xn, k, v, w_q, w_o):
    b, t, dm = xn.shape
    q = (xn @ w_q).reshape(b, t, MEM_HEADS, MEM_HEAD_DIM)
    s = jnp.einsum('bthd,bmhd->bhtm', q, k).astype(jnp.float32) * MEM_SCALE
    p = jax.nn.softmax(s, axis=-1).astype(v.dtype)
    o = jnp.einsum('bhtm,bmhd->bthd', p, v).reshape(b, t, dm)
    return o @ w_o


def shared_latent(h, kv_in_norm, w_dkv, kv_latent_norm, w_kr, pos):
    hn = rms_norm(h, kv_in_norm)
    ckv = rms_norm(hn @ w_dkv, kv_latent_norm)
    kr = rope(hn @ w_kr, pos)
    return ckv, kr


def latent_attend(q_lat, q_rope, q_pos, ckv, krope, k_pos):
    s = jnp.einsum('bthr,bsr->bhts', q_lat, ckv) + jnp.einsum('bthd,bsd->bhts', q_rope, krope)
    s = s.astype(jnp.float32) * MLA_SCALE
    s = jnp.where(k_pos[None, :] <= q_pos[:, None], s, -jnp.inf)
    p = jax.nn.softmax(s, axis=-1).astype(ckv.dtype)
    return jnp.einsum('bhts,bsr->bthr', p, ckv)


def mla_mixer(xn, q_pos, ckv, krope, k_pos, w_dq, q_norm, w_uq, w_uk, w_uv, w_o):
    b, t, _ = xn.shape
    cq = rms_norm(xn @ w_dq, q_norm)
    q = jnp.einsum('btr,rhd->bthd', cq, w_uq)
    q_rope = rope(q[..., QK_NOPE:], q_pos)
    q_lat = jnp.einsum('bthd,rhd->bthr', q[..., :QK_NOPE], w_uk)
    if t % Q_BLOCK == 0:
        nb = t // Q_BLOCK
        blk = lambda a: a.reshape((b, nb, Q_BLOCK) + a.shape[2:]).swapaxes(0, 1)
        o = lax.map(lambda args: latent_attend(args[0], args[1], args[2], ckv, krope, k_pos),
                    (blk(q_lat), blk(q_rope), q_pos.reshape(nb, Q_BLOCK)))
        o_lat = o.swapaxes(0, 1).reshape(b, t, MLA_HEADS, KV_LORA)
    else:
        o_lat = latent_attend(q_lat, q_rope, q_pos, ckv, krope, k_pos)
    o = jnp.einsum('bthr,rhv->bthv', o_lat, w_uv).reshape(b, t, MLA_HEADS * V_HEAD)
    return o @ w_o


def sq_relu_mlp(xn, w_up, w_down):
    hdn = jax.nn.relu(xn @ w_up)
    return (hdn * hdn) @ w_down


def setup_inputs(seed: int = 0) -> dict:
    key = jax.random.key(seed)
    ks = iter(jax.random.split(key, 64))
    f32 = jnp.float32
    n_a = DEPTH // 2
    n_b = DEPTH - n_a
    n_pages = PAST_LEN // PAGE_SIZE
    used = DEC_BATCH * n_pages
    n_phys = used + max(1, used // 4)
    d = D_MODEL
    G, N, K = SSM_GROUPS, SSM_STATE, SSM_GROUP

    def nrm(shape, scale):
        return jax.random.normal(next(ks), shape, f32) * scale

    def gain(shape):
        return 1.0 + nrm(shape, 0.02)

    inp = {}
    inp['x_prompt'] = nrm((BATCH, SEQ, d), 1.0)
    inp['x_sample'] = nrm((DEC_BATCH, DEC_SEQ, d), 1.0)
    inp['cache_ssm_re'] = nrm((n_a, DEC_BATCH, G, N), 0.5)
    inp['cache_ssm_im'] = nrm((n_a, DEC_BATCH, G, N), 0.5)
    inp['cache_kv_latent'] = nrm((n_phys, PAGE_SIZE, KV_LORA), 1.0)
    inp['cache_k_rope'] = nrm((n_phys, PAGE_SIZE, QK_ROPE), 1.0)
    inp['cache_mem_k'] = nrm((DEPTH, DEC_BATCH, MEM_TOKENS, MEM_HEADS, MEM_HEAD_DIM), 1.0)
    inp['cache_mem_v'] = nrm((DEPTH, DEC_BATCH, MEM_TOKENS, MEM_HEADS, MEM_HEAD_DIM), 1.0)
    inp['page_table'] = jax.random.permutation(next(ks), n_phys)[:used].reshape(DEC_BATCH, n_pages).astype(jnp.int32)
    inp['mem_prompt'] = nrm((BATCH, MEM_TOKENS, d), 1.0)
    inp['norm_mix_pre'] = gain((DEPTH, d))
    inp['norm_mix_post'] = gain((DEPTH, d))
    inp['norm_mem_pre'] = gain((DEPTH, d))
    inp['norm_mem_post'] = gain((DEPTH, d))
    inp['norm_mlp_pre'] = gain((DEPTH, d))
    inp['norm_mlp_post'] = gain((DEPTH, d))
    inp['mem_in_norm'] = gain((DEPTH, d))
    inp['w_mem_q'] = nrm((DEPTH, d, MEM_HEADS * MEM_HEAD_DIM), d ** -0.5)
    inp['w_mem_k'] = nrm((DEPTH, d, MEM_HEADS * MEM_HEAD_DIM), d ** -0.5)
    inp['w_mem_v'] = nrm((DEPTH, d, MEM_HEADS * MEM_HEAD_DIM), d ** -0.5)
    inp['w_mem_o'] = nrm((DEPTH, MEM_HEADS * MEM_HEAD_DIM, d), (MEM_HEADS * MEM_HEAD_DIM) ** -0.5)
    inp['w_mlp_up'] = nrm((DEPTH, d, D_FF), d ** -0.5)
    inp['w_mlp_down'] = nrm((DEPTH, D_FF, d), D_FF ** -0.5)
    n_idx = jnp.arange(N, dtype=f32)
    inp['ssm_a_re'] = -0.5 + nrm((n_a, G, N), 0.01)
    inp['ssm_a_im'] = math.pi * n_idx + nrm((n_a, G, N), 0.01)
    inp['ssm_log_dt'] = jax.random.uniform(next(ks), (n_a, G), f32, math.log(DT_MIN), math.log(DT_MAX))
    inp['ssm_b_re'] = nrm((n_a, G, N, K), (2 * K) ** -0.5)
    inp['ssm_b_im'] = nrm((n_a, G, N, K), (2 * K) ** -0.5)
    inp['ssm_c_re'] = nrm((n_a, G, K, N), (2 * N) ** -0.5)
    inp['ssm_c_im'] = nrm((n_a, G, K, N), (2 * N) ** -0.5)
    inp['ssm_d'] = nrm((n_a, d), 1.0)
    inp['w_glu'] = nrm((n_a, d, 2 * d), d ** -0.5)
    inp['kv_in_norm'] = gain((d,))
    inp['w_dkv'] = nrm((d, KV_LORA), d ** -0.5)
    inp['kv_latent_norm'] = gain((KV_LORA,))
    inp['w_kr'] = nrm((d, QK_ROPE), d ** -0.5)
    inp['w_uk'] = nrm((KV_LORA, MLA_HEADS, QK_NOPE), KV_LORA ** -0.5)
    inp['w_uv'] = nrm((KV_LORA, MLA_HEADS, V_HEAD), KV_LORA ** -0.5)
    inp['w_dq'] = nrm((n_b, d, Q_LORA), d ** -0.5)
    inp['q_norm'] = gain((n_b, Q_LORA))
    inp['w_uq'] = nrm((n_b, Q_LORA, MLA_HEADS, QK_NOPE + QK_ROPE), Q_LORA ** -0.5)
    inp['w_o'] = nrm((n_b, MLA_HEADS * V_HEAD, d), (MLA_HEADS * V_HEAD) ** -0.5)
    return inp


def reference(x_prompt, x_sample, cache_ssm_re, cache_ssm_im, cache_kv_latent, cache_k_rope,
              cache_mem_k, cache_mem_v, page_table, mem_prompt,
              norm_mix_pre, norm_mix_post, norm_mem_pre, norm_mem_post, norm_mlp_pre, norm_mlp_post,
              mem_in_norm, w_mem_q, w_mem_k, w_mem_v, w_mem_o, w_mlp_up, w_mlp_down,
              ssm_a_re, ssm_a_im, ssm_log_dt, ssm_b_re, ssm_b_im, ssm_c_re, ssm_c_im, ssm_d, w_glu,
              kv_in_norm, w_dkv, kv_latent_norm, w_kr, w_uk, w_uv,
              w_dq, q_norm, w_uq, w_o):
    n_a = DEPTH // 2
    bp, sp, _ = x_prompt.shape
    bs, ts, _ = x_sample.shape
    past_len = page_table.shape[1] * PAGE_SIZE
    pos_p = jnp.arange(sp, dtype=jnp.int32)
    pos_s = past_len + jnp.arange(ts, dtype=jnp.int32)
    k_pos_s = jnp.arange(past_len + ts, dtype=jnp.int32)

    hp, hs = x_prompt, x_sample
    ssm_p_re, ssm_p_im, ssm_s_re, ssm_s_im = [], [], [], []
    memk_p, memv_p = [], []
    for i in range(DEPTH):
        if i < n_a:
            ssm_w = (ssm_a_re[i], ssm_a_im[i], ssm_log_dt[i], ssm_b_re[i], ssm_b_im[i],
                     ssm_c_re[i], ssm_c_im[i], ssm_d[i], w_glu[i])
            h0 = jnp.zeros((bp, SSM_GROUPS, SSM_STATE), jnp.float32)
            yp, fpr, fpi = s5_mixer(rms_norm(hp, norm_mix_pre[i]), *ssm_w, h0, h0)
            ys, fsr, fsi = s5_mixer(rms_norm(hs, norm_mix_pre[i]), *ssm_w, cache_ssm_re[i], cache_ssm_im[i])
            ssm_p_re.append(fpr)
            ssm_p_im.append(fpi)
            ssm_s_re.append(fsr)
            ssm_s_im.append(fsi)
        else:
            if i == n_a:
                ckv_p, kr_p = shared_latent(hp, kv_in_norm, w_dkv, kv_latent_norm, w_kr, pos_p)
                ckv_s_new, kr_s_new = shared_latent(hs, kv_in_norm, w_dkv, kv_latent_norm, w_kr, pos_s)
                ckv_s = jnp.concatenate(
                    [cache_kv_latent[page_table].reshape(bs, past_len, KV_LORA), ckv_s_new], axis=1)
                kr_s = jnp.concatenate(
                    [cache_k_rope[page_table].reshape(bs, past_len, QK_ROPE), kr_s_new], axis=1)
            j = i - n_a
            yp = mla_mixer(rms_norm(hp, norm_mix_pre[i]), pos_p, ckv_p, kr_p, pos_p,
                           w_dq[j], q_norm[j], w_uq[j], w_uk, w_uv, w_o[j])
            ys = mla_mixer(rms_norm(hs, norm_mix_pre[i]), pos_s, ckv_s, kr_s, k_pos_s,
                           w_dq[j], q_norm[j], w_uq[j], w_uk, w_uv, w_o[j])
        hp = hp + rms_norm(yp, norm_mix_post[i])
        hs = hs + rms_norm(ys, norm_mix_post[i])
        mk, mv = mem_kv(mem_prompt, mem_in_norm[i], w_mem_k[i], w_mem_v[i])
        memk_p.append(mk)
        memv_p.append(mv)
        hp = hp + rms_norm(mem_attend(rms_norm(hp, norm_mem_pre[i]), mk, mv, w_mem_q[i], w_mem_o[i]), norm_mem_post[i])
        hs = hs + rms_norm(mem_attend(rms_norm(hs, norm_mem_pre[i]), cache_mem_k[i], cache_mem_v[i],
                                      w_mem_q[i], w_mem_o[i]), norm_mem_post[i])
        hp = hp + rms_norm(sq_relu_mlp(rms_norm(hp, norm_mlp_pre[i]), w_mlp_up[i], w_mlp_down[i]), norm_mlp_post[i])
        hs = hs + rms_norm(sq_relu_mlp(rms_norm(hs, norm_mlp_pre[i]), w_mlp_up[i], w_mlp_down[i]), norm_mlp_post[i])

    ssm_re_prompt = jnp.stack(ssm_p_re)
    ssm_im_prompt = jnp.stack(ssm_p_im)
    ssm_re_sample = jnp.stack(ssm_s_re)
    ssm_im_sample = jnp.stack(ssm_s_im)
    mem_k_prompt = jnp.stack(memk_p)
    mem_v_prompt = jnp.stack(memv_p)
    return (hp, hs, ssm_re_prompt, ssm_im_prompt, ssm_re_sample, ssm_im_sample,
            ckv_p, kr_p, ckv_s_new, kr_s_new, mem_k_prompt, mem_v_prompt)
```

```python
import functools
import math

import jax
import jax.numpy as jnp
from jax import lax
from jax.experimental import pallas as pl
from jax.experimental.pallas import tpu as pltpu

F32 = jnp.float32
BF16 = jnp.bfloat16

D_MODEL = 1024
PAGE_SIZE = 128
SSM_GROUP = 16
SSM_GROUPS = D_MODEL // SSM_GROUP
SSM_STATE = 64
SSM_WIDTH = SSM_GROUPS * SSM_STATE
SSM_KBLOCKS = 4
MLA_HEADS = 8
QK_NOPE = 128
QK_ROPE = 64
ROPE_PAD = 128
V_HEAD = 128
KV_LORA = 256
Q_LORA = 384
ROPE_BASE = 10000.0
MLA_SCALE = (QK_NOPE + QK_ROPE) ** -0.5
MEM_TOKENS = 256
MEM_HEADS = 4
MEM_HEAD_DIM = D_MODEL // MEM_HEADS
MEM_SCALE = MEM_HEAD_DIM ** -0.5
D_FF = 4 * D_MODEL
EPS = 1e-6
NEG_BIG = -1e30

VMEM_LIMIT_BYTES = 56 * 1024 * 1024


def _params(*semantics):
    return pltpu.CompilerParams(dimension_semantics=semantics, vmem_limit_bytes=VMEM_LIMIT_BYTES)


def _dot(a, b):
    return jnp.dot(a, b, preferred_element_type=F32)


def _dot_t(a, b):
    return lax.dot_general(a, b, (((1,), (1,)), ((), ())), preferred_element_type=F32)


def _rms(x, g):
    return x * lax.rsqrt(jnp.mean(x * x, axis=-1, keepdims=True) + EPS) * g


def _row(v):
    return v.reshape(1, -1).astype(F32)


def _ssm_disc_kernel(ar_ref, ai_ref, ldt_ref, br_ref, bi_ref,
                     abar_re_ref, abar_im_ref, bbar_re_ref, bbar_im_ref):
    ar = ar_ref[...]
    ai = ai_ref[...]
    dt = jnp.exp(ldt_ref[...])
    mag = jnp.exp(ar * dt)
    abar_re = mag * jnp.cos(ai * dt)
    abar_im = mag * jnp.sin(ai * dt)
    den = ar * ar + ai * ai
    nr = abar_re - 1.0
    ni = abar_im
    coef_re = (nr * ar + ni * ai) / den
    coef_im = (ni * ar - nr * ai) / den
    br = br_ref[...]
    bi = bi_ref[...]
    abar_re_ref[...] = abar_re
    abar_im_ref[...] = abar_im
    bbar_re_ref[...] = coef_re * br - coef_im * bi
    bbar_im_ref[...] = coef_re * bi + coef_im * br


def _ssm_discretise(a_re, a_im, log_dt, b_re, b_im):
    g, n, k = SSM_GROUPS, SSM_STATE, SSM_GROUP
    rep = lambda a: jnp.repeat(a.astype(F32), k, axis=0)
    brt = b_re.astype(F32).transpose(0, 2, 1).reshape(g * k, n)
    bit = b_im.astype(F32).transpose(0, 2, 1).reshape(g * k, n)
    ldt = jnp.repeat(log_dt.astype(F32), k, axis=0).reshape(g * k, 1)
    shp = jax.ShapeDtypeStruct((g * k, n), F32)
    abar_re, abar_im, bbar_re, bbar_im = pl.pallas_call(
        _ssm_disc_kernel, out_shape=(shp, shp, shp, shp), name="ssm_discretise",
    )(rep(a_re), rep(a_im), ldt, brt, bit)
    return abar_re[::k], abar_im[::k], bbar_re, bbar_im


def _block_diag(x):
    kb, ng, r, c = x.shape
    eye = jnp.eye(ng, dtype=x.dtype)
    return (x[:, :, :, None, :] * eye[None, :, None, :, None]).reshape(kb, ng * r, ng * c)


def _s5_kernel(x_ref, gpre_ref, bdre_ref, bdim_ref, are_ref, aim_ref, h0re_ref, h0im_ref,
               cre_ref, ncim_ref, dskip_ref, wglu_ref, gpost_ref,
               out_ref, fre_ref, fim_ref, bure_ref, buim_ref, *, tt, bb):
    ti = pl.program_id(1)
    rows = tt * bb

    @pl.when(ti == 0)
    def _():
        fre_ref[...] = h0re_ref[...]
        fim_ref[...] = h0im_ref[...]

    x = x_ref[...].reshape(rows, D_MODEL)
    xn = _rms(x, gpre_ref[...])
    xb = xn.astype(BF16)
    kw = D_MODEL // SSM_KBLOCKS
    sw = SSM_WIDTH // SSM_KBLOCKS
    for kb in range(SSM_KBLOCKS):
        xk = xb[:, kb * kw:(kb + 1) * kw]
        bure_ref[:, kb * sw:(kb + 1) * sw] = _dot(xk, bdre_ref[kb])
        buim_ref[:, kb * sw:(kb + 1) * sw] = _dot(xk, bdim_ref[kb])

    cw = 512
    for cg in range(SSM_WIDTH // cw):
        cols = slice(cg * cw, (cg + 1) * cw)
        a_re = jnp.broadcast_to(are_ref[:, cols], (8, cw))
        a_im = jnp.broadcast_to(aim_ref[:, cols], (8, cw))

        def rb_body(rb, _, cols=cols, a_re=a_re, a_im=a_im):
            r0 = pl.multiple_of(rb * 8, 8)
            h_re = fre_ref[pl.ds(r0, 8), cols]
            h_im = fim_ref[pl.ds(r0, 8), cols]

            def t_body(t, carry):
                h_re, h_im = carry
                row = pl.multiple_of(t * bb + r0, 8)
                n_re = a_re * h_re - a_im * h_im + bure_ref[pl.ds(row, 8), cols]
                n_im = a_re * h_im + a_im * h_re + buim_ref[pl.ds(row, 8), cols]
                bure_ref[pl.ds(row, 8), cols] = n_re
                buim_ref[pl.ds(row, 8), cols] = n_im
                return n_re, n_im

            h_re, h_im = lax.fori_loop(0, tt, t_body, (h_re, h_im), unroll=4)
            fre_ref[pl.ds(r0, 8), cols] = h_re
            fim_ref[pl.ds(r0, 8), cols] = h_im
            return 0

        lax.fori_loop(0, bb // 8, rb_body, 0)

    ys = []
    for kb in range(SSM_KBLOCKS):
        hre = bure_ref[:, kb * sw:(kb + 1) * sw].astype(BF16)
        him = buim_ref[:, kb * sw:(kb + 1) * sw].astype(BF16)
        ys.append(_dot(hre, cre_ref[kb]) + _dot(him, ncim_ref[kb]))
    y = jnp.concatenate(ys, axis=1) + dskip_ref[...] * xn
    g = 0.5 * y * (1.0 + lax.erf(y * (2.0 ** -0.5)))
    z = _dot(g.astype(BF16), wglu_ref[...])
    o = z[:, :D_MODEL] * (1.0 / (1.0 + jnp.exp(-z[:, D_MODEL:])))
    out_ref[...] = (x + _rms(o, gpost_ref[...])).reshape(tt, bb, D_MODEL)


def _s5_layer(h, t_len, b_all, tt, bb, gpre, bdre, bdim, a_re, a_im, h0_re, h0_im,
              cre, ncim, dskip, wglu, gpost, name):
    x3 = h.reshape(t_len, b_all, D_MODEL)
    grid = (b_all // bb, t_len // tt)
    const2 = lambda bj, ti: (0, 0)
    const3 = lambda bj, ti: (0, 0, 0)
    out, fre, fim = pl.pallas_call(
        functools.partial(_s5_kernel, tt=tt, bb=bb),
        grid=grid,
        in_specs=[
            pl.BlockSpec((tt, bb, D_MODEL), lambda bj, ti: (ti, bj, 0)),
            pl.BlockSpec((1, D_MODEL), const2),
            pl.BlockSpec(bdre.shape, const3),
            pl.BlockSpec(bdim.shape, const3),
            pl.BlockSpec((1, SSM_WIDTH), const2),
            pl.BlockSpec((1, SSM_WIDTH), const2),
            pl.BlockSpec((bb, SSM_WIDTH), lambda bj, ti: (bj, 0)),
            pl.BlockSpec((bb, SSM_WIDTH), lambda bj, ti: (bj, 0)),
            pl.BlockSpec(cre.shape, const3),
            pl.BlockSpec(ncim.shape, const3),
            pl.BlockSpec((1, D_MODEL), const2),
            pl.BlockSpec(wglu.shape, const2),
            pl.BlockSpec((1, D_MODEL), const2),
        ],
        out_specs=(
            pl.BlockSpec((tt, bb, D_MODEL), lambda bj, ti: (ti, bj, 0)),
            pl.BlockSpec((bb, SSM_WIDTH), lambda bj, ti: (bj, 0)),
            pl.BlockSpec((bb, SSM_WIDTH), lambda bj, ti: (bj, 0)),
        ),
        out_shape=(
            jax.ShapeDtypeStruct((t_len, b_all, D_MODEL), F32),
            jax.ShapeDtypeStruct((b_all, SSM_WIDTH), F32),
            jax.ShapeDtypeStruct((b_all, SSM_WIDTH), F32),
        ),
        scratch_shapes=[pltpu.VMEM((tt * bb, SSM_WIDTH), F32), pltpu.VMEM((tt * bb, SSM_WIDTH), F32)],
        compiler_params=_params("parallel", "arbitrary"),
        name=name,
    )(x3, gpre, bdre, bdim, a_re, a_im, h0_re, h0_im, cre, ncim, dskip, wglu, gpost)
    return out.reshape(t_len * b_all, D_MODEL), fre, fim


def _mem_kv_kernel(x_ref, g_ref, wk_ref, wv_ref, k_ref, v_ref, kb_ref, vb_ref):
    xn = _rms(x_ref[...], g_ref[0]).astype(BF16)
    k = _dot(xn, wk_ref[0])
    v = _dot(xn, wv_ref[0])
    k_ref[0] = k
    v_ref[0] = v
    kb_ref[0] = k.astype(BF16)
    vb_ref[0] = v.astype(BF16)


def _mem_kv(mem, g, wk, wv):
    rows = mem.shape[0]
    nl = g.shape[0]
    tm = 512
    blk = pl.BlockSpec((1, tm, D_MODEL), lambda l, i: (l, i, 0))
    wspec = pl.BlockSpec((1, D_MODEL, D_MODEL), lambda l, i: (l, 0, 0))
    return pl.pallas_call(
        _mem_kv_kernel,
        grid=(nl, rows // tm),
        in_specs=[pl.BlockSpec((tm, D_MODEL), lambda l, i: (i, 0)),
                  pl.BlockSpec((1, 1, D_MODEL), lambda l, i: (l, 0, 0)), wspec, wspec],
        out_specs=(blk, blk, blk, blk),
        out_shape=(jax.ShapeDtypeStruct((nl, rows, D_MODEL), F32),) * 2
        + (jax.ShapeDtypeStruct((nl, rows, D_MODEL), BF16),) * 2,
        compiler_params=_params("parallel", "parallel"),
        name="mem_kv",
    )(mem, g, wk, wv)


def _softmax_rows(s):
    m = jnp.max(s, axis=-1, keepdims=True)
    p = jnp.exp(s - m)
    return p / jnp.sum(p, axis=-1, keepdims=True)


def _mem_attn_prompt_kernel(h_ref, gpre_ref, wq_ref, k_ref, v_ref, wo_ref, gpost_ref, out_ref):
    x = h_ref[...]
    xn = _rms(x, gpre_ref[...]).astype(BF16)
    q = (_dot(xn, wq_ref[...]) * MEM_SCALE).astype(BF16)
    os = []
    for hh in range(MEM_HEADS):
        cols = slice(hh * MEM_HEAD_DIM, (hh + 1) * MEM_HEAD_DIM)
        p = _softmax_rows(_dot_t(q[:, cols], k_ref[0, :, cols]))
        os.append(_dot(p.astype(BF16), v_ref[0, :, cols]).astype(BF16))
    y = _dot(jnp.concatenate(os, axis=1), wo_ref[...])
    out_ref[...] = x + _rms(y, gpost_ref[...])


def _mem_attn_prompt(h, t_len, b_all, layer, gpre, wq, kb, vb, wo, gpost):
    tq = 512
    hv = h.reshape(t_len, b_all * D_MODEL)
    const2 = lambda b, i: (0, 0)
    hspec = pl.BlockSpec((tq, D_MODEL), lambda b, i: (i, b))
    kvspec = pl.BlockSpec((1, MEM_TOKENS, D_MODEL), lambda b, i: (layer, b, 0))
    wspec = pl.BlockSpec((D_MODEL, D_MODEL), const2)
    gspec = pl.BlockSpec((1, D_MODEL), const2)
    out = pl.pallas_call(
        _mem_attn_prompt_kernel,
        grid=(b_all, t_len // tq),
        in_specs=[hspec, gspec, wspec, kvspec, kvspec, wspec, gspec],
        out_specs=hspec,
        out_shape=jax.ShapeDtypeStruct(hv.shape, F32),
        compiler_params=_params("parallel", "parallel"),
        name="mem_attn_prompt",
    )(hv, gpre, wq, kb, vb, wo, gpost)
    return out.reshape(t_len * b_all, D_MODEL)


def _mem_attn_sample_kernel(q_ref, k_ref, v_ref, o_ref, *, nb):
    for g in range(nb):
        for hh in range(MEM_HEADS):
            cols = slice(hh * MEM_HEAD_DIM, (hh + 1) * MEM_HEAD_DIM)
            qcols = slice(g * D_MODEL + hh * MEM_HEAD_DIM, g * D_MODEL + (hh + 1) * MEM_HEAD_DIM)
            k = k_ref[0, g, :, cols].astype(BF16)
            v = v_ref[0, g, :, cols].astype(BF16)
            p = _softmax_rows(_dot_t(q_ref[:, qcols].astype(BF16), k))
            o_ref[:, qcols] = _dot(p.astype(BF16), v)


def _mem_attn_sample(q, t_len, b_all, layer, cache_k, cache_v):
    nb = 4
    qv = q.reshape(t_len, b_all * D_MODEL)
    qspec = pl.BlockSpec((t_len, nb * D_MODEL), lambda j: (0, j))
    cspec = pl.BlockSpec((1, nb, MEM_TOKENS, D_MODEL), lambda j: (layer, j, 0, 0))
    out = pl.pallas_call(
        functools.partial(_mem_attn_sample_kernel, nb=nb),
        grid=(b_all // nb,),
        in_specs=[qspec, cspec, cspec],
        out_specs=qspec,
        out_shape=jax.ShapeDtypeStruct(qv.shape, F32),
        compiler_params=_params("parallel"),
        name="mem_attn_sample",
    )(qv, cache_k, cache_v)
    return out.reshape(t_len * b_all, D_MODEL)


def _norm_matmul_kernel(x_ref, g_ref, w_ref, o_ref, *, scale):
    xn = _rms(x_ref[...], g_ref[...]).astype(BF16)
    o_ref[...] = (_dot(xn, w_ref[...]) * scale).astype(o_ref.dtype)


def _norm_matmul(x, g, w, scale, out_dtype, name):
    rows = x.shape[0]
    tm = 512
    n = w.shape[1]
    return pl.pallas_call(
        functools.partial(_norm_matmul_kernel, scale=scale),
        grid=(rows // tm,),
        in_specs=[pl.BlockSpec((tm, D_MODEL), lambda i: (i, 0)),
                  pl.BlockSpec((1, D_MODEL), lambda i: (0, 0)),
                  pl.BlockSpec(w.shape, lambda i: (0, 0))],
        out_specs=pl.BlockSpec((tm, n), lambda i: (i, 0)),
        out_shape=jax.ShapeDtypeStruct((rows, n), out_dtype),
        compiler_params=_params("parallel"),
        name=name,
    )(x, g, w)


def _matmul_norm_res_kernel(a_ref, w_ref, g_ref, h_ref, o_ref):
    y = _dot(a_ref[...].astype(BF16), w_ref[...])
    o_ref[...] = h_ref[...] + _rms(y, g_ref[...])


def _matmul_norm_res(a, w, g, h, name):
    rows, k = a.shape
    tm = 512
    return pl.pallas_call(
        _matmul_norm_res_kernel,
        grid=(rows // tm,),
        in_specs=[pl.BlockSpec((tm, k), lambda i: (i, 0)),
                  pl.BlockSpec(w.shape, lambda i: (0, 0)),
                  pl.BlockSpec((1, D_MODEL), lambda i: (0, 0)),
                  pl.BlockSpec((tm, D_MODEL), lambda i: (i, 0))],
        out_specs=pl.BlockSpec((tm, D_MODEL), lambda i: (i, 0)),
        out_shape=jax.ShapeDtypeStruct((rows, D_MODEL), F32),
        compiler_params=_params("parallel"),
        name=name,
    )(a, w, g, h)


def _mlp_kernel(x_ref, gpre_ref, wup_ref, wdn_ref, gpost_ref, o_ref, xn_ref, acc_ref):
    j = pl.program_id(1)

    @pl.when(j == 0)
    def _():
        xn_ref[...] = _rms(x_ref[...], gpre_ref[...]).astype(BF16)
        acc_ref[...] = jnp.zeros_like(acc_ref)

    hid = jnp.maximum(_dot(xn_ref[...], wup_ref[...]), 0.0)
    acc_ref[...] += _dot((hid * hid).astype(BF16), wdn_ref[...])

    @pl.when(j == pl.num_programs(1) - 1)
    def _():
        o_ref[...] = x_ref[...] + _rms(acc_ref[...], gpost_ref[...])


def _mlp(h, gpre, wup, wdn, gpost, name):
    rows = h.shape[0]
    tm, tf = 512, 1024
    return pl.pallas_call(
        _mlp_kernel,
        grid=(rows // tm, D_FF // tf),
        in_specs=[pl.BlockSpec((tm, D_MODEL), lambda i, j: (i, 0)),
                  pl.BlockSpec((1, D_MODEL), lambda i, j: (0, 0)),
                  pl.BlockSpec((D_MODEL, tf), lambda i, j: (0, j)),
                  pl.BlockSpec((tf, D_MODEL), lambda i, j: (j, 0)),
                  pl.BlockSpec((1, D_MODEL), lambda i, j: (0, 0))],
        out_specs=pl.BlockSpec((tm, D_MODEL), lambda i, j: (i, 0)),
        out_shape=jax.ShapeDtypeStruct((rows, D_MODEL), F32),
        scratch_shapes=[pltpu.VMEM((tm, D_MODEL), BF16), pltpu.VMEM((tm, D_MODEL), F32)],
        compiler_params=_params("parallel", "arbitrary"),
        name=name,
    )(h, gpre, wup, wdn, gpost)


def _mla_pre_kernel(h_ref, cos_ref, sin_ref, gmix_ref, gkv_ref, wdq_ref, qn_ref, wuq_ref, ukt_ref,
                    wdkv_ref, glat_ref, wkr_ref,
                    qlat_ref, qrope_ref, kvcat_ref, ckv_ref, kr_ref):
    x = h_ref[...]
    y = x * lax.rsqrt(jnp.mean(x * x, axis=-1, keepdims=True) + EPS)
    cos = cos_ref[...]
    sin = sin_ref[...]
    hn = (y * gkv_ref[...]).astype(BF16)
    ckv = _rms(_dot(hn, wdkv_ref[...]), glat_ref[...])
    kr2 = _dot(hn, wkr_ref[...])
    kr = kr2[:, :ROPE_PAD] * cos + kr2[:, ROPE_PAD:] * sin
    ckv_ref[...] = ckv
    kr_ref[...] = kr
    kvcat_ref[...] = jnp.concatenate([ckv, kr], axis=1).astype(BF16)
    xn = (y * gmix_ref[...]).astype(BF16)
    cq = _rms(_dot(xn, wdq_ref[...]), qn_ref[...]).astype(BF16)
    q = _dot(cq, wuq_ref[...])
    nope_w = MLA_HEADS * QK_NOPE
    rope_w = MLA_HEADS * ROPE_PAD
    cos_t = jnp.concatenate([cos] * MLA_HEADS, axis=1)
    sin_t = jnp.concatenate([sin] * MLA_HEADS, axis=1)
    qr = q[:, nope_w:nope_w + rope_w] * cos_t + q[:, nope_w + rope_w:] * sin_t
    qrope_ref[...] = (qr * MLA_SCALE).astype(qrope_ref.dtype)
    for hh in range(MLA_HEADS):
        qn_h = q[:, hh * QK_NOPE:(hh + 1) * QK_NOPE].astype(BF16)
        ql = _dot(qn_h, ukt_ref[hh])
        qlat_ref[:, hh * KV_LORA:(hh + 1) * KV_LORA] = (ql * MLA_SCALE).astype(qlat_ref.dtype)


def _mla_pre(h, cos, sin, gmix, gkv, wdq, qn, wuq, ukt, wdkv, glat, wkr, q_dtype, name):
    rows = h.shape[0]
    tm = 512
    row = lambda w: pl.BlockSpec((tm, w), lambda i: (i, 0))
    full = lambda a: pl.BlockSpec(a.shape, lambda i: (0,) * a.ndim)
    return pl.pallas_call(
        _mla_pre_kernel,
        grid=(rows // tm,),
        in_specs=[row(D_MODEL), row(ROPE_PAD), row(ROPE_PAD), full(gmix), full(gkv), full(wdq), full(qn),
                  full(wuq), full(ukt), full(wdkv), full(glat), full(wkr)],
        out_specs=(row(MLA_HEADS * KV_LORA), row(MLA_HEADS * ROPE_PAD), row(KV_LORA + ROPE_PAD),
                   row(KV_LORA), row(ROPE_PAD)),
        out_shape=(jax.ShapeDtypeStruct((rows, MLA_HEADS * KV_LORA), q_dtype),
                   jax.ShapeDtypeStruct((rows, MLA_HEADS * ROPE_PAD), q_dtype),
                   jax.ShapeDtypeStruct((rows, KV_LORA + ROPE_PAD), BF16),
                   jax.ShapeDtypeStruct((rows, KV_LORA), F32),
                   jax.ShapeDtypeStruct((rows, ROPE_PAD), F32)),
        compiler_params=_params("parallel"),
        name=name,
    )(h, cos, sin, gmix, gkv, wdq, qn, wuq, ukt, wdkv, glat, wkr)


def _stack_heads(ref, width):
    return jnp.concatenate([ref[:, hh * width:(hh + 1) * width] for hh in range(MLA_HEADS)], axis=0)


def _mla_attn_prompt_kernel(qlat_ref, qrope_ref, kv_ref, o_ref, m_ref, l_ref, acc_ref, *, tq, tk):
    qi = pl.program_id(1)
    ql = _stack_heads(qlat_ref, KV_LORA)
    qr = _stack_heads(qrope_ref, ROPE_PAD)
    m_ref[...] = jnp.full_like(m_ref, NEG_BIG)
    l_ref[...] = jnp.zeros_like(l_ref)
    acc_ref[...] = jnp.zeros_like(acc_ref)
    rows = MLA_HEADS * tq
    q_pos = qi * tq + (lax.broadcasted_iota(jnp.int32, (rows, tk), 0) & (tq - 1))
    k_off = lax.broadcasted_iota(jnp.int32, (rows, tk), 1)

    def body(j, _):
        k0 = pl.multiple_of(j * tk, tk)
        ck = kv_ref[pl.ds(k0, tk), :KV_LORA]
        kr = kv_ref[pl.ds(k0, tk), KV_LORA:]
        s = _dot_t(ql, ck) + _dot_t(qr, kr)
        s = jnp.where(k_off + k0 <= q_pos, s, NEG_BIG)
        m_old = m_ref[...]
        m_new = jnp.maximum(m_old, jnp.max(s, axis=-1, keepdims=True))
        alpha = jnp.exp(m_old - m_new)
        p = jnp.exp(s - m_new)
        l_ref[...] = alpha * l_ref[...] + jnp.sum(p, axis=-1, keepdims=True)
        acc_ref[...] = alpha * acc_ref[...] + _dot(p.astype(BF16), ck)
        m_ref[...] = m_new
        return 0

    lax.fori_loop(0, (qi * tq + tq + tk - 1) // tk, body, 0)
    o = acc_ref[...] / l_ref[...]
    for hh in range(MLA_HEADS):
        o_ref[:, hh * KV_LORA:(hh + 1) * KV_LORA] = o[hh * tq:(hh + 1) * tq].astype(BF16)


def _mla_attn_prompt(qlat, qrope, kvcat, t_len, b_all):
    tq, tk = 128, 256
    lw, rw, kw = MLA_HEADS * KV_LORA, MLA_HEADS * ROPE_PAD, KV_LORA + ROPE_PAD
    rows = MLA_HEADS * tq
    out = pl.pallas_call(
        functools.partial(_mla_attn_prompt_kernel, tq=tq, tk=tk),
        grid=(b_all, t_len // tq),
        in_specs=[pl.BlockSpec((tq, lw), lambda b, i: (i, b)),
                  pl.BlockSpec((tq, rw), lambda b, i: (i, b)),
                  pl.BlockSpec((t_len, kw), lambda b, i: (0, b))],
        out_specs=pl.BlockSpec((tq, lw), lambda b, i: (i, b)),
        out_shape=jax.ShapeDtypeStruct((t_len, b_all * lw), BF16),
        scratch_shapes=[pltpu.VMEM((rows, 1), F32), pltpu.VMEM((rows, 1), F32),
                        pltpu.VMEM((rows, KV_LORA), F32)],
        compiler_params=_params("parallel", "parallel"),
        name="mla_attn_prompt",
    )(qlat.reshape(t_len, b_all * lw), qrope.reshape(t_len, b_all * rw), kvcat.reshape(t_len, b_all * kw))
    return out.reshape(t_len * b_all, lw)


def _mla_attn_sample_kernel(pt_ref, qlat_ref, qrope_ref, ckvn_ref, krn_ref, *refs, ts, ppc):
    kv_refs = refs[:ppc]
    kr_refs = refs[ppc:2 * ppc]
    o_ref, m_ref, l_ref, acc_ref, ql_ref, qr_ref = refs[2 * ppc:]
    c = pl.program_id(1)
    rows = MLA_HEADS * ts

    @pl.when(c == 0)
    def _():
        m_ref[...] = jnp.full_like(m_ref, NEG_BIG)
        l_ref[...] = jnp.zeros_like(l_ref)
        acc_ref[...] = jnp.zeros_like(acc_ref)
        ql_ref[...] = _stack_heads(qlat_ref, KV_LORA).astype(BF16)
        qr_ref[...] = _stack_heads(qrope_ref, ROPE_PAD).astype(BF16)

    def update(s, vals):
        m_old = m_ref[...]
        m_new = jnp.maximum(m_old, jnp.max(s, axis=-1, keepdims=True))
        alpha = jnp.exp(m_old - m_new)
        p = jnp.exp(s - m_new)
        l_ref[...] = alpha * l_ref[...] + jnp.sum(p, axis=-1, keepdims=True)
        acc_ref[...] = alpha * acc_ref[...] + _dot(p.astype(BF16), vals)
        m_ref[...] = m_new

    ql = ql_ref[...]
    qr = qr_ref[...]
    qr_cache = qr[:, :QK_ROPE]
    cks = [kv_refs[k][0].astype(BF16) for k in range(ppc)]
    ss = [_dot_t(ql, cks[k]) + _dot_t(qr_cache, kr_refs[k][0].astype(BF16)) for k in range(ppc)]
    update(jnp.concatenate(ss, axis=1), jnp.concatenate(cks, axis=0))

    @pl.when(c == pl.num_programs(1) - 1)
    def _():
        pad = PAGE_SIZE - ts
        ckn = jnp.concatenate([ckvn_ref[...], jnp.zeros((pad, KV_LORA), F32)], axis=0).astype(BF16)
        krn = jnp.concatenate([krn_ref[...], jnp.zeros((pad, ROPE_PAD), F32)], axis=0).astype(BF16)
        s = _dot_t(ql, ckn) + _dot_t(qr, krn)
        t_q = lax.broadcasted_iota(jnp.int32, (rows, PAGE_SIZE), 0) & (ts - 1)
        t_k = lax.broadcasted_iota(jnp.int32, (rows, PAGE_SIZE), 1)
        update(jnp.where(t_k <= t_q, s, NEG_BIG), ckn)
        o = acc_ref[...] / l_ref[...]
        for hh in range(MLA_HEADS):
            o_ref[:, hh * KV_LORA:(hh + 1) * KV_LORA] = o[hh * ts:(hh + 1) * ts]


def _mla_attn_sample(qlat, qrope, ckv_new, kr_new, cache_kv, cache_kr, page_table, ts, b_all):
    ppc = 8
    n_pages = page_table.shape[1]
    lw, rw = MLA_HEADS * KV_LORA, MLA_HEADS * ROPE_PAD
    rows = MLA_HEADS * ts
    pt = page_table.reshape(-1)

    def page_spec(width, k):
        return pl.BlockSpec((1, PAGE_SIZE, width), lambda b, c, pt_ref: (pt_ref[b * n_pages + c * ppc + k], 0, 0))

    grid_spec = pltpu.PrefetchScalarGridSpec(
        num_scalar_prefetch=1,
        grid=(b_all, n_pages // ppc),
        in_specs=[pl.BlockSpec((ts, lw), lambda b, c, pt_ref: (0, b)),
                  pl.BlockSpec((ts, rw), lambda b, c, pt_ref: (0, b)),
                  pl.BlockSpec((ts, KV_LORA), lambda b, c, pt_ref: (0, b)),
                  pl.BlockSpec((ts, ROPE_PAD), lambda b, c, pt_ref: (0, b))]
        + [page_spec(KV_LORA, k) for k in range(ppc)]
        + [page_spec(QK_ROPE, k) for k in range(ppc)],
        out_specs=pl.BlockSpec((ts, lw), lambda b, c, pt_ref: (0, b)),
        scratch_shapes=[pltpu.VMEM((rows, 1), F32), pltpu.VMEM((rows, 1), F32),
                        pltpu.VMEM((rows, KV_LORA), F32),
                        pltpu.VMEM((rows, KV_LORA), BF16), pltpu.VMEM((rows, ROPE_PAD), BF16)],
    )
    out = pl.pallas_call(
        functools.partial(_mla_attn_sample_kernel, ts=ts, ppc=ppc),
        grid_spec=grid_spec,
        out_shape=jax.ShapeDtypeStruct((ts, b_all * lw), F32),
        compiler_params=_params("parallel", "arbitrary"),
        name="mla_attn_sample",
    )(pt, qlat.reshape(ts, b_all * lw), qrope.reshape(ts, b_all * rw),
      ckv_new.reshape(ts, b_all * KV_LORA), kr_new.reshape(ts, b_all * ROPE_PAD),
      *([cache_kv] * ppc), *([cache_kr] * ppc))
    return out.reshape(ts * b_all, lw)


def _mla_out_kernel(o_ref, wuv_ref, wo_ref, g_ref, h_ref, out_ref):
    vs = []
    for hh in range(MLA_HEADS):
        vs.append(_dot(o_ref[:, hh * KV_LORA:(hh + 1) * KV_LORA].astype(BF16), wuv_ref[hh]).astype(BF16))
    y = _dot(jnp.concatenate(vs, axis=1), wo_ref[...])
    out_ref[...] = h_ref[...] + _rms(y, g_ref[...])


def _mla_out(o, wuv, wo, g, h, name):
    rows = h.shape[0]
    tm = 512
    lw = MLA_HEADS * KV_LORA
    return pl.pallas_call(
        _mla_out_kernel,
        grid=(rows // tm,),
        in_specs=[pl.BlockSpec((tm, lw), lambda i: (i, 0)),
                  pl.BlockSpec(wuv.shape, lambda i: (0, 0, 0)),
                  pl.BlockSpec(wo.shape, lambda i: (0, 0)),
                  pl.BlockSpec((1, D_MODEL), lambda i: (0, 0)),
                  pl.BlockSpec((tm, D_MODEL), lambda i: (i, 0))],
        out_specs=pl.BlockSpec((tm, D_MODEL), lambda i: (i, 0)),
        out_shape=jax.ShapeDtypeStruct((rows, D_MODEL), F32),
        compiler_params=_params("parallel"),
        name=name,
    )(o, wuv, wo, g, h)


def _rope_tables(pos):
    half = QK_ROPE // 2
    inv = ROPE_BASE ** (-jnp.arange(half, dtype=F32) / half)
    ang = pos.astype(F32)[:, None] * inv[None, :]
    cos = jnp.cos(ang)
    sin = jnp.sin(ang)
    zeros = jnp.zeros((pos.shape[0], ROPE_PAD - QK_ROPE), F32)
    return jnp.concatenate([cos, cos, zeros], axis=1), jnp.concatenate([sin, sin, zeros], axis=1)


def _rot_cols(w):
    half = QK_ROPE // 2
    return jnp.concatenate([-w[..., half:], w[..., :half]], axis=-1)


def _pad_rope(w):
    pad = [(0, 0)] * (w.ndim - 1) + [(0, ROPE_PAD - QK_ROPE)]
    return jnp.pad(w, pad)


def kernel(x_prompt, x_sample, cache_ssm_re, cache_ssm_im, cache_kv_latent, cache_k_rope, cache_mem_k, cache_mem_v, page_table, mem_prompt, norm_mix_pre, norm_mix_post, norm_mem_pre, norm_mem_post, norm_mlp_pre, norm_mlp_post, mem_in_norm, w_mem_q, w_mem_k, w_mem_v, w_mem_o, w_mlp_up, w_mlp_down, ssm_a_re, ssm_a_im, ssm_log_dt, ssm_b_re, ssm_b_im, ssm_c_re, ssm_c_im, ssm_d, w_glu, kv_in_norm, w_dkv, kv_latent_norm, w_kr, w_uk, w_uv, w_dq, q_norm, w_uq, w_o):
    depth = norm_mix_pre.shape[0]
    n_a = ssm_a_re.shape[0]
    bp, sp, _ = x_prompt.shape
    bs, ts, _ = x_sample.shape
    past_len = page_table.shape[1] * PAGE_SIZE

    hp = x_prompt.astype(F32).transpose(1, 0, 2).reshape(sp * bp, D_MODEL)
    hs = x_sample.astype(F32).transpose(1, 0, 2).reshape(ts * bs, D_MODEL)

    mem_rows = mem_prompt.astype(F32).reshape(bp * MEM_TOKENS, D_MODEL)
    mk, mv, mkb, mvb = _mem_kv(mem_rows, mem_in_norm.astype(F32).reshape(depth, 1, D_MODEL),
                               w_mem_k.astype(BF16), w_mem_v.astype(BF16))
    cmk = cache_mem_k.reshape(depth, bs, MEM_TOKENS, D_MODEL)
    cmv = cache_mem_v.reshape(depth, bs, MEM_TOKENS, D_MODEL)

    ssm_p_re, ssm_p_im, ssm_s_re, ssm_s_im = [], [], [], []
    ckv_p = kr_p = ckv_s = kr_s = None
    for i in range(depth):
        if i < n_a:
            abar_re, abar_im, bbar_re, bbar_im = _ssm_discretise(
                ssm_a_re[i], ssm_a_im[i], ssm_log_dt[i], ssm_b_re[i], ssm_b_im[i])
            ng = SSM_GROUPS // SSM_KBLOCKS
            bdre = _block_diag(bbar_re.reshape(SSM_KBLOCKS, ng, SSM_GROUP, SSM_STATE)).astype(BF16)
            bdim = _block_diag(bbar_im.reshape(SSM_KBLOCKS, ng, SSM_GROUP, SSM_STATE)).astype(BF16)
            c_t = lambda c: c.astype(F32).reshape(SSM_KBLOCKS, ng, SSM_GROUP, SSM_STATE).transpose(0, 1, 3, 2)
            cre = _block_diag(c_t(ssm_c_re[i])).astype(BF16)
            ncim = _block_diag(-c_t(ssm_c_im[i])).astype(BF16)
            s5_w = (_row(norm_mix_pre[i]), bdre, bdim, abar_re.reshape(1, SSM_WIDTH), abar_im.reshape(1, SSM_WIDTH))
            s5_w2 = (cre, ncim, _row(ssm_d[i]), w_glu[i].astype(BF16), _row(norm_mix_post[i]))
            zeros = jnp.zeros((bp, SSM_WIDTH), F32)
            hp, fpr, fpi = _s5_layer(hp, sp, bp, 32, bp, *s5_w, zeros, zeros, *s5_w2, name="s5_prompt")
            hs, fsr, fsi = _s5_layer(hs, ts, bs, ts, 32, *s5_w,
                                     cache_ssm_re[i].astype(F32).reshape(bs, SSM_WIDTH),
                                     cache_ssm_im[i].astype(F32).reshape(bs, SSM_WIDTH), *s5_w2, name="s5_sample")
            ssm_p_re.append(fpr.reshape(bp, SSM_GROUPS, SSM_STATE))
            ssm_p_im.append(fpi.reshape(bp, SSM_GROUPS, SSM_STATE))
            ssm_s_re.append(fsr.reshape(bs, SSM_GROUPS, SSM_STATE))
            ssm_s_im.append(fsi.reshape(bs, SSM_GROUPS, SSM_STATE))
        else:
            j = i - n_a
            wuq = w_uq[j].astype(F32)
            w_nope = wuq[:, :, :QK_NOPE].reshape(Q_LORA, MLA_HEADS * QK_NOPE)
            w_r = wuq[:, :, QK_NOPE:]
            w_rope = _pad_rope(w_r).reshape(Q_LORA, MLA_HEADS * ROPE_PAD)
            w_rope_rot = _pad_rope(_rot_cols(w_r)).reshape(Q_LORA, MLA_HEADS * ROPE_PAD)
            wuq_all = jnp.concatenate([w_nope, w_rope, w_rope_rot], axis=1).astype(BF16)
            ukt = w_uk.astype(F32).transpose(1, 2, 0).astype(BF16)
            wkr = w_kr.astype(F32)
            wkr2 = jnp.concatenate([_pad_rope(wkr), _pad_rope(_rot_cols(wkr))], axis=1).astype(BF16)
            pre_w = (_row(norm_mix_pre[i]), _row(kv_in_norm), w_dq[j].astype(BF16), _row(q_norm[j]), wuq_all, ukt,
                     w_dkv.astype(BF16), _row(kv_latent_norm), wkr2)
            cos_p, sin_p = _rope_tables(jnp.repeat(jnp.arange(sp, dtype=jnp.int32), bp))
            cos_s, sin_s = _rope_tables(past_len + jnp.repeat(jnp.arange(ts, dtype=jnp.int32), bs))
            qlat_p, qrope_p, kvcat_p, ckv_pj, kr_pj = _mla_pre(hp, cos_p, sin_p, *pre_w, BF16, name="mla_pre_prompt")
            qlat_s, qrope_s, _, ckv_sj, kr_sj = _mla_pre(hs, cos_s, sin_s, *pre_w, F32, name="mla_pre_sample")
            if i == n_a:
                ckv_p, kr_p, ckv_s, kr_s, kvcat_shared = ckv_pj, kr_pj, ckv_sj, kr_sj, kvcat_p
            o_p = _mla_attn_prompt(qlat_p, qrope_p, kvcat_shared, sp, bp)
            o_s = _mla_attn_sample(qlat_s, qrope_s, ckv_s, kr_s, cache_kv_latent.astype(F32),
                                   cache_k_rope.astype(F32), page_table, ts, bs)
            wuv = w_uv.astype(F32).transpose(1, 0, 2).astype(BF16)
            wo = w_o[j].astype(BF16)
            hp = _mla_out(o_p, wuv, wo, _row(norm_mix_post[i]), hp, name="mla_out_prompt")
            hs = _mla_out(o_s, wuv, wo, _row(norm_mix_post[i]), hs, name="mla_out_sample")

        wq = w_mem_q[i].astype(BF16)
        wo_mem = w_mem_o[i].astype(BF16)
        hp = _mem_attn_prompt(hp, sp, bp, i, _row(norm_mem_pre[i]), wq, mkb, mvb, wo_mem, _row(norm_mem_post[i]))
        q_s = _norm_matmul(hs, _row(norm_mem_pre[i]), wq, MEM_SCALE, F32, name="mem_q_sample")
        o_s = _mem_attn_sample(q_s, ts, bs, i, cmk, cmv)
        hs = _matmul_norm_res(o_s, wo_mem, _row(norm_mem_post[i]), hs, name="mem_out_sample")

        wup = w_mlp_up[i].astype(BF16)
        wdn = w_mlp_down[i].astype(BF16)
        hp = _mlp(hp, _row(norm_mlp_pre[i]), wup, wdn, _row(norm_mlp_post[i]), name="mlp_prompt")
        hs = _mlp(hs, _row(norm_mlp_pre[i]), wup, wdn, _row(norm_mlp_post[i]), name="mlp_sample")

    def batch_major(a, t_len, b_all):
        return a.reshape(t_len, b_all, a.shape[-1]).transpose(1, 0, 2)

    return (batch_major(hp, sp, bp), batch_major(hs, ts, bs),
            jnp.stack(ssm_p_re), jnp.stack(ssm_p_im), jnp.stack(ssm_s_re), jnp.stack(ssm_s_im),
            batch_major(ckv_p, sp, bp), batch_major(kr_p[:, :QK_ROPE], sp, bp),
            batch_major(ckv_s, ts, bs), batch_major(kr_s[:, :QK_ROPE], ts, bs),
            mk.reshape(depth, bp, MEM_TOKENS, MEM_HEADS, MEM_HEAD_DIM),
            mv.reshape(depth, bp, MEM_TOKENS, MEM_HEADS, MEM_HEAD_DIM))
```

```python
import functools
import math

import jax
import jax.numpy as jnp
from jax import lax
from jax.experimental import pallas as pl
from jax.experimental.pallas import tpu as pltpu

F32 = jnp.float32
BF16 = jnp.bfloat16

LANES = 128
SUBLANES = 8
D_MODEL = 1024
PAGE_SIZE = 128
SSM_GROUP = 16
SSM_GROUPS = D_MODEL // SSM_GROUP
SSM_STATE = 64
SSM_WIDTH = SSM_GROUPS * SSM_STATE
SSM_KBLOCKS = 4
SSM_SLABS = SSM_WIDTH // LANES
MLA_HEADS = 8
QK_NOPE = 128
QK_ROPE = 64
ROPE_PAD = LANES
V_HEAD = 128
KV_LORA = 256
Q_LORA = 384
ROPE_BASE = 10000.0
MLA_SCALE = (QK_NOPE + QK_ROPE) ** -0.5
MLA_Q_SCALE = MLA_SCALE * math.log2(math.e)
MEM_TOKENS = 256
MEM_HEADS = 4
MEM_HEAD_DIM = D_MODEL // MEM_HEADS
MEM_SCALE = MEM_HEAD_DIM ** -0.5
D_FF = 4 * D_MODEL
EPS = 1e-6
NEG_BIG = -1e30

VMEM_LIMIT_BYTES = 56 * 1024 * 1024


def _params(*semantics):
    return pltpu.CompilerParams(dimension_semantics=semantics, vmem_limit_bytes=VMEM_LIMIT_BYTES)


def _dot(a, b):
    return jnp.dot(a, b, preferred_element_type=F32)


def _dot_t(a, b):
    return lax.dot_general(a, b, (((1,), (1,)), ((), ())), preferred_element_type=F32)


def _rms(x, g):
    return x * lax.rsqrt(jnp.mean(x * x, axis=-1, keepdims=True) + EPS) * g


def _row(v):
    return v.reshape(1, -1).astype(F32)


def _lane_tile(x, width):
    return jnp.concatenate([x] * (width // LANES), axis=1) if width > LANES else x


def _ssm_disc_kernel(ar_ref, ai_ref, ldt_ref, br_ref, bi_ref,
                     abar_re_ref, abar_im_ref, bbar_re_ref, bbar_im_ref):
    ar = ar_ref[...]
    ai = ai_ref[...]
    dt = jnp.exp(ldt_ref[...])
    mag = jnp.exp(ar * dt)
    abar_re = mag * jnp.cos(ai * dt)
    abar_im = mag * jnp.sin(ai * dt)
    den = ar * ar + ai * ai
    nr = abar_re - 1.0
    ni = abar_im
    coef_re = (nr * ar + ni * ai) / den
    coef_im = (ni * ar - nr * ai) / den
    br = br_ref[...]
    bi = bi_ref[...]
    abar_re_ref[...] = abar_re
    abar_im_ref[...] = abar_im
    bbar_re_ref[...] = coef_re * br - coef_im * bi
    bbar_im_ref[...] = coef_re * bi + coef_im * br


def _ssm_discretise(a_re, a_im, log_dt, b_re, b_im):
    g, n, k = SSM_GROUPS, SSM_STATE, SSM_GROUP
    rep = lambda a: jnp.repeat(a.astype(F32), k, axis=0)
    brt = b_re.astype(F32).transpose(0, 2, 1).reshape(g * k, n)
    bit = b_im.astype(F32).transpose(0, 2, 1).reshape(g * k, n)
    ldt = jnp.repeat(log_dt.astype(F32), k, axis=0).reshape(g * k, 1)
    shp = jax.ShapeDtypeStruct((g * k, n), F32)
    abar_re, abar_im, bbar_re, bbar_im = pl.pallas_call(
        _ssm_disc_kernel, out_shape=(shp, shp, shp, shp), name="ssm_discretise",
    )(rep(a_re), rep(a_im), ldt, brt, bit)
    return abar_re[::k], abar_im[::k], bbar_re, bbar_im


def _block_diag(x):
    kb, ng, r, c = x.shape
    eye = jnp.eye(ng, dtype=x.dtype)
    return (x[:, :, :, None, :] * eye[None, :, None, :, None]).reshape(kb, ng * r, ng * c)


def _s5_kernel(x_ref, gpre_ref, bdre_ref, bdim_ref, are_ref, aim_ref, h0re_ref, h0im_ref,
               cre_ref, ncim_ref, dskip_ref, wglu_ref, gpost_ref,
               out_ref, fre_ref, fim_ref, bure_ref, buim_ref, *, tt, bb, pitch):
    ti = pl.program_id(1)
    rows = tt * bb
    spk = SSM_SLABS // SSM_KBLOCKS

    @pl.when(ti == 0)
    def _():
        fre_ref[...] = h0re_ref[...]
        fim_ref[...] = h0im_ref[...]

    x = x_ref[...].reshape(rows, D_MODEL)
    xn = _rms(x, gpre_ref[...])
    xb = xn.astype(BF16)
    kw = D_MODEL // SSM_KBLOCKS
    for kb in range(SSM_KBLOCKS):
        xk = xb[:, kb * kw:(kb + 1) * kw]
        for w_ref, dst_ref in ((bdre_ref, bure_ref), (bdim_ref, buim_ref)):
            res = _dot(xk, w_ref[kb])
            for s in range(spk):
                piece = res[:, s * LANES:(s + 1) * LANES]
                if pitch == tt:
                    dst_ref[kb * spk + s] = piece
                else:
                    for b in range(bb):
                        dst_ref[kb * spk + s, b * pitch:b * pitch + tt, :] = piece[b * tt:(b + 1) * tt]

    sg = 4
    for g0 in range(0, SSM_SLABS, sg):
        lanes = [slice((g0 + k) * LANES, (g0 + k + 1) * LANES) for k in range(sg)]
        a_re = [jnp.broadcast_to(are_ref[:, lanes[k]], (SUBLANES, LANES)) for k in range(sg)]
        a_im = [jnp.broadcast_to(aim_ref[:, lanes[k]], (SUBLANES, LANES)) for k in range(sg)]

        def vb_body(vb, _, g0=g0, lanes=lanes, a_re=a_re, a_im=a_im):
            b0 = vb * SUBLANES
            if not isinstance(b0, int):
                b0 = pl.multiple_of(b0, SUBLANES)
            h_re = tuple(fre_ref[pl.ds(b0, SUBLANES), lanes[k]] for k in range(sg))
            h_im = tuple(fim_ref[pl.ds(b0, SUBLANES), lanes[k]] for k in range(sg))

            def t_body(t, carry):
                h_re, h_im = carry
                rws = pl.ds(b0 * pitch + t, SUBLANES, stride=pitch)
                n_re, n_im = [], []
                for k in range(sg):
                    nr = a_re[k] * h_re[k] - a_im[k] * h_im[k] + bure_ref[g0 + k, rws, :]
                    ni = a_re[k] * h_im[k] + a_im[k] * h_re[k] + buim_ref[g0 + k, rws, :]
                    bure_ref[g0 + k, rws, :] = nr
                    buim_ref[g0 + k, rws, :] = ni
                    n_re.append(nr)
                    n_im.append(ni)
                return tuple(n_re), tuple(n_im)

            h_re, h_im = lax.fori_loop(0, tt, t_body, (h_re, h_im), unroll=4)
            for k in range(sg):
                fre_ref[pl.ds(b0, SUBLANES), lanes[k]] = h_re[k]
                fim_ref[pl.ds(b0, SUBLANES), lanes[k]] = h_im[k]
            return 0

        if bb == SUBLANES:
            vb_body(0, 0)
        else:
            lax.fori_loop(0, bb // SUBLANES, vb_body, 0)

    def gather(ref, kb):
        cols = []
        for s in range(spk):
            if pitch == tt:
                cols.append(ref[kb * spk + s])
            else:
                cols.append(jnp.concatenate(
                    [ref[kb * spk + s, b * pitch:b * pitch + tt, :] for b in range(bb)], axis=0))
        return jnp.concatenate(cols, axis=1).astype(BF16)

    ys = []
    for kb in range(SSM_KBLOCKS):
        ys.append(_dot(gather(bure_ref, kb), cre_ref[kb]) + _dot(gather(buim_ref, kb), ncim_ref[kb]))
    y = jnp.concatenate(ys, axis=1) + dskip_ref[...] * xn
    g = 0.5 * y * (1.0 + lax.erf(y * (2.0 ** -0.5)))
    z = _dot(g.astype(BF16), wglu_ref[...])
    o = z[:, :D_MODEL] * (1.0 / (1.0 + jnp.exp(-z[:, D_MODEL:])))
    out_ref[...] = (x + _rms(o, gpost_ref[...])).reshape(bb, tt, D_MODEL)


def _s5_layer(x3, tt, bb, pitch, gpre, bdre, bdim, a_re, a_im, h0_re, h0_im,
              cre, ncim, dskip, wglu, gpost, name):
    b_all, t_len, _ = x3.shape
    grid = (b_all // bb, t_len // tt)
    const2 = lambda bj, ti: (0, 0)
    const3 = lambda bj, ti: (0, 0, 0)
    slab = pltpu.VMEM((SSM_SLABS, bb * pitch, LANES), F32)
    return pl.pallas_call(
        functools.partial(_s5_kernel, tt=tt, bb=bb, pitch=pitch),
        grid=grid,
        in_specs=[
            pl.BlockSpec((bb, tt, D_MODEL), lambda bj, ti: (bj, ti, 0)),
            pl.BlockSpec((1, D_MODEL), const2),
            pl.BlockSpec(bdre.shape, const3),
            pl.BlockSpec(bdim.shape, const3),
            pl.BlockSpec((1, SSM_WIDTH), const2),
            pl.BlockSpec((1, SSM_WIDTH), const2),
            pl.BlockSpec((bb, SSM_WIDTH), lambda bj, ti: (bj, 0)),
            pl.BlockSpec((bb, SSM_WIDTH), lambda bj, ti: (bj, 0)),
            pl.BlockSpec(cre.shape, const3),
            pl.BlockSpec(ncim.shape, const3),
            pl.BlockSpec((1, D_MODEL), const2),
            pl.BlockSpec(wglu.shape, const2),
            pl.BlockSpec((1, D_MODEL), const2),
        ],
        out_specs=(
            pl.BlockSpec((bb, tt, D_MODEL), lambda bj, ti: (bj, ti, 0)),
            pl.BlockSpec((bb, SSM_WIDTH), lambda bj, ti: (bj, 0)),
            pl.BlockSpec((bb, SSM_WIDTH), lambda bj, ti: (bj, 0)),
        ),
        out_shape=(
            jax.ShapeDtypeStruct((b_all, t_len, D_MODEL), F32),
            jax.ShapeDtypeStruct((b_all, SSM_WIDTH), F32),
            jax.ShapeDtypeStruct((b_all, SSM_WIDTH), F32),
        ),
        scratch_shapes=[slab, slab],
        compiler_params=_params("parallel", "arbitrary"),
        name=name,
    )(x3, gpre, bdre, bdim, a_re, a_im, h0_re, h0_im, cre, ncim, dskip, wglu, gpost)


def _mem_kv_kernel(x_ref, g_ref, wk_ref, wv_ref, k_ref, v_ref, kb_ref, vb_ref):
    xn = _rms(x_ref[...], g_ref[0]).astype(BF16)
    k = _dot(xn, wk_ref[0])
    v = _dot(xn, wv_ref[0])
    k_ref[0] = k
    v_ref[0] = v
    kb_ref[0] = k.astype(BF16)
    vb_ref[0] = v.astype(BF16)


def _mem_kv(mem, g, wk, wv):
    rows = mem.shape[0]
    nl = g.shape[0]
    tm = 512
    blk = pl.BlockSpec((1, tm, D_MODEL), lambda l, i: (l, i, 0))
    wspec = pl.BlockSpec((1, D_MODEL, D_MODEL), lambda l, i: (l, 0, 0))
    return pl.pallas_call(
        _mem_kv_kernel,
        grid=(nl, rows // tm),
        in_specs=[pl.BlockSpec((tm, D_MODEL), lambda l, i: (i, 0)),
                  pl.BlockSpec((1, 1, D_MODEL), lambda l, i: (l, 0, 0)), wspec, wspec],
        out_specs=(blk, blk, blk, blk),
        out_shape=(jax.ShapeDtypeStruct((nl, rows, D_MODEL), F32),) * 2
        + (jax.ShapeDtypeStruct((nl, rows, D_MODEL), BF16),) * 2,
        compiler_params=_params("parallel", "parallel"),
        name="mem_kv",
    )(mem, g, wk, wv)


def _softmax_rows(s):
    m = jnp.max(s, axis=-1, keepdims=True)
    p = jnp.exp(s - m)
    return p / jnp.sum(p, axis=-1, keepdims=True)


def _mem_attn_prompt_kernel(h_ref, gpre_ref, wq_ref, k_ref, v_ref, wo_ref, gpost_ref, out_ref):
    x = h_ref[...]
    xn = _rms(x, gpre_ref[...]).astype(BF16)
    q = (_dot(xn, wq_ref[...]) * MEM_SCALE).astype(BF16)
    os = []
    for hh in range(MEM_HEADS):
        cols = slice(hh * MEM_HEAD_DIM, (hh + 1) * MEM_HEAD_DIM)
        p = _softmax_rows(_dot_t(q[:, cols], k_ref[0, :, cols]))
        os.append(_dot(p.astype(BF16), v_ref[0, :, cols]).astype(BF16))
    y = _dot(jnp.concatenate(os, axis=1), wo_ref[...])
    out_ref[...] = x + _rms(y, gpost_ref[...])


def _mem_attn_prompt(h, t_len, b_all, layer, gpre, wq, kb, vb, wo, gpost):
    tq = 512
    nq = t_len // tq
    const2 = lambda b, i: (0, 0)
    hspec = pl.BlockSpec((tq, D_MODEL), lambda b, i: (b * nq + i, 0))
    kvspec = pl.BlockSpec((1, MEM_TOKENS, D_MODEL), lambda b, i: (layer, b, 0))
    wspec = pl.BlockSpec((D_MODEL, D_MODEL), const2)
    gspec = pl.BlockSpec((1, D_MODEL), const2)
    return pl.pallas_call(
        _mem_attn_prompt_kernel,
        grid=(b_all, nq),
        in_specs=[hspec, gspec, wspec, kvspec, kvspec, wspec, gspec],
        out_specs=hspec,
        out_shape=jax.ShapeDtypeStruct(h.shape, F32),
        compiler_params=_params("parallel", "parallel"),
        name="mem_attn_prompt",
    )(h, gpre, wq, kb, vb, wo, gpost)


def _mem_cache_view(cache):
    nl, nb, m, nh, dh = cache.shape
    chunks = dh // LANES
    return cache.reshape(nl, nb, m, nh, chunks, LANES).transpose(0, 1, 2, 4, 3, 5).reshape(
        nl, nb, m * chunks * nh, LANES)


def _mem_attn_sample_kernel(q_ref, k_ref, v_ref, o_ref, *, nb, ts):
    chunks = MEM_HEAD_DIM // LANES
    period = chunks * MEM_HEADS

    def head(ref, g, hh):
        parts = [ref[0, g, pl.ds(c * MEM_HEADS + hh, MEM_TOKENS, stride=period), :] for c in range(chunks)]
        return jnp.concatenate(parts, axis=1).astype(BF16)

    for g in range(nb):
        q = q_ref[g * ts:(g + 1) * ts, :].astype(BF16)
        for hh in range(MEM_HEADS):
            cols = slice(hh * MEM_HEAD_DIM, (hh + 1) * MEM_HEAD_DIM)
            p = _softmax_rows(_dot_t(q[:, cols], head(k_ref, g, hh)))
            o_ref[g * ts:(g + 1) * ts, cols] = _dot(p.astype(BF16), head(v_ref, g, hh))


def _mem_attn_sample(q, ts, b_all, layer, cache_k, cache_v):
    nb = 4
    qspec = pl.BlockSpec((nb * ts, D_MODEL), lambda j: (j, 0))
    cspec = pl.BlockSpec((1, nb) + cache_k.shape[2:], lambda j: (layer, j, 0, 0))
    return pl.pallas_call(
        functools.partial(_mem_attn_sample_kernel, nb=nb, ts=ts),
        grid=(b_all // nb,),
        in_specs=[qspec, cspec, cspec],
        out_specs=qspec,
        out_shape=jax.ShapeDtypeStruct(q.shape, F32),
        compiler_params=_params("parallel"),
        name="mem_attn_sample",
    )(q, cache_k, cache_v)


def _norm_matmul_kernel(x_ref, g_ref, w_ref, o_ref, *, scale):
    xn = _rms(x_ref[...], g_ref[...]).astype(BF16)
    o_ref[...] = (_dot(xn, w_ref[...]) * scale).astype(o_ref.dtype)


def _norm_matmul(x, g, w, scale, out_dtype, name):
    rows = x.shape[0]
    tm = 512
    n = w.shape[1]
    return pl.pallas_call(
        functools.partial(_norm_matmul_kernel, scale=scale),
        grid=(rows // tm,),
        in_specs=[pl.BlockSpec((tm, D_MODEL), lambda i: (i, 0)),
                  pl.BlockSpec((1, D_MODEL), lambda i: (0, 0)),
                  pl.BlockSpec(w.shape, lambda i: (0, 0))],
        out_specs=pl.BlockSpec((tm, n), lambda i: (i, 0)),
        out_shape=jax.ShapeDtypeStruct((rows, n), out_dtype),
        compiler_params=_params("parallel"),
        name=name,
    )(x, g, w)


def _matmul_norm_res_kernel(a_ref, w_ref, g_ref, h_ref, o_ref):
    y = _dot(a_ref[...].astype(BF16), w_ref[...])
    o_ref[...] = h_ref[...] + _rms(y, g_ref[...])


def _matmul_norm_res(a, w, g, h, name):
    rows, k = a.shape
    tm = 512
    return pl.pallas_call(
        _matmul_norm_res_kernel,
        grid=(rows // tm,),
        in_specs=[pl.BlockSpec((tm, k), lambda i: (i, 0)),
                  pl.BlockSpec(w.shape, lambda i: (0, 0)),
                  pl.BlockSpec((1, D_MODEL), lambda i: (0, 0)),
                  pl.BlockSpec((tm, D_MODEL), lambda i: (i, 0))],
        out_specs=pl.BlockSpec((tm, D_MODEL), lambda i: (i, 0)),
        out_shape=jax.ShapeDtypeStruct((rows, D_MODEL), F32),
        compiler_params=_params("parallel"),
        name=name,
    )(a, w, g, h)


def _mlp_kernel(x_ref, gpre_ref, wup_ref, wdn_ref, gpost_ref, o_ref, xn_ref, acc_ref):
    j = pl.program_id(1)

    @pl.when(j == 0)
    def _():
        xn_ref[...] = _rms(x_ref[...], gpre_ref[...]).astype(BF16)
        acc_ref[...] = jnp.zeros_like(acc_ref)

    hid = jnp.maximum(_dot(xn_ref[...], wup_ref[...]), 0.0)
    acc_ref[...] += _dot((hid * hid).astype(BF16), wdn_ref[...])

    @pl.when(j == pl.num_programs(1) - 1)
    def _():
        o_ref[...] = x_ref[...] + _rms(acc_ref[...], gpost_ref[...])


def _mlp(h, gpre, wup, wdn, gpost, name):
    rows = h.shape[0]
    tm, tf = 512, 1024
    return pl.pallas_call(
        _mlp_kernel,
        grid=(rows // tm, D_FF // tf),
        in_specs=[pl.BlockSpec((tm, D_MODEL), lambda i, j: (i, 0)),
                  pl.BlockSpec((1, D_MODEL), lambda i, j: (0, 0)),
                  pl.BlockSpec((D_MODEL, tf), lambda i, j: (0, j)),
                  pl.BlockSpec((tf, D_MODEL), lambda i, j: (j, 0)),
                  pl.BlockSpec((1, D_MODEL), lambda i, j: (0, 0))],
        out_specs=pl.BlockSpec((tm, D_MODEL), lambda i, j: (i, 0)),
        out_shape=jax.ShapeDtypeStruct((rows, D_MODEL), F32),
        scratch_shapes=[pltpu.VMEM((tm, D_MODEL), BF16), pltpu.VMEM((tm, D_MODEL), F32)],
        compiler_params=_params("parallel", "arbitrary"),
        name=name,
    )(h, gpre, wup, wdn, gpost)


def _mla_pre_kernel(h_ref, cos_ref, sin_ref, gmix_ref, gkv_ref, wdq_ref, qn_ref, wuq_ref, ukt_ref,
                    wdkv_ref, glat_ref, wkr_ref,
                    qlat_ref, qrope_ref, kvcat_ref, ckv_ref, kr_ref):
    x = h_ref[...]
    y = x * lax.rsqrt(jnp.mean(x * x, axis=-1, keepdims=True) + EPS)
    cos = cos_ref[...]
    sin = sin_ref[...]
    hn = (y * gkv_ref[...]).astype(BF16)
    ckv = _rms(_dot(hn, wdkv_ref[...]), glat_ref[...])
    kr2 = _dot(hn, wkr_ref[...])
    kr = kr2[:, :ROPE_PAD] * cos + kr2[:, ROPE_PAD:] * sin
    ckv_ref[...] = ckv
    kr_ref[...] = kr
    kvcat_ref[...] = jnp.concatenate([ckv, kr], axis=1).astype(BF16)
    xn = (y * gmix_ref[...]).astype(BF16)
    cq = _rms(_dot(xn, wdq_ref[...]), qn_ref[...]).astype(BF16)
    q = _dot(cq, wuq_ref[...])
    nope_w = MLA_HEADS * QK_NOPE
    rope_w = MLA_HEADS * ROPE_PAD
    qr = (q[:, nope_w:nope_w + rope_w] * _lane_tile(cos, rope_w)
          + q[:, nope_w + rope_w:] * _lane_tile(sin, rope_w))
    qrope_ref[...] = (qr * MLA_Q_SCALE).astype(qrope_ref.dtype)
    for hh in range(MLA_HEADS):
        qn_h = q[:, hh * QK_NOPE:(hh + 1) * QK_NOPE].astype(BF16)
        ql = _dot(qn_h, ukt_ref[hh])
        qlat_ref[:, hh * KV_LORA:(hh + 1) * KV_LORA] = (ql * MLA_Q_SCALE).astype(qlat_ref.dtype)


def _mla_pre(h, cos, sin, gmix, gkv, wdq, qn, wuq, ukt, wdkv, glat, wkr, q_dtype, name):
    rows = h.shape[0]
    tm = 512
    period = cos.shape[0] // tm
    row = lambda w: pl.BlockSpec((tm, w), lambda i: (i, 0))
    tab = pl.BlockSpec((tm, ROPE_PAD), lambda i: (i % period, 0))
    full = lambda a: pl.BlockSpec(a.shape, lambda i: (0,) * a.ndim)
    return pl.pallas_call(
        _mla_pre_kernel,
        grid=(rows // tm,),
        in_specs=[row(D_MODEL), tab, tab, full(gmix), full(gkv), full(wdq), full(qn),
                  full(wuq), full(ukt), full(wdkv), full(glat), full(wkr)],
        out_specs=(row(MLA_HEADS * KV_LORA), row(MLA_HEADS * ROPE_PAD), row(KV_LORA + ROPE_PAD),
                   row(KV_LORA), row(ROPE_PAD)),
        out_shape=(jax.ShapeDtypeStruct((rows, MLA_HEADS * KV_LORA), q_dtype),
                   jax.ShapeDtypeStruct((rows, MLA_HEADS * ROPE_PAD), q_dtype),
                   jax.ShapeDtypeStruct((rows, KV_LORA + ROPE_PAD), BF16),
                   jax.ShapeDtypeStruct((rows, KV_LORA), F32),
                   jax.ShapeDtypeStruct((rows, ROPE_PAD), F32)),
        compiler_params=_params("parallel"),
        name=name,
    )(h, cos, sin, gmix, gkv, wdq, qn, wuq, ukt, wdkv, glat, wkr)


def _mla_attn_prompt_kernel(qlat_ref, qrope_ref, kv_ref, o_ref, m_ref, l_ref, acc_ref, *, tq):
    qi = pl.program_id(1)
    m_ref[...] = jnp.full_like(m_ref, NEG_BIG)
    l_ref[...] = jnp.zeros_like(l_ref)
    acc_ref[...] = jnp.zeros_like(acc_ref)

    def block(j, diagonal):
        k0 = pl.multiple_of(j * tq, tq)
        ck = kv_ref[pl.ds(k0, tq), :KV_LORA]
        kr = kv_ref[pl.ds(k0, tq), KV_LORA:]
        for hh in range(MLA_HEADS):
            s = (_dot_t(qlat_ref[:, hh * KV_LORA:(hh + 1) * KV_LORA], ck)
                 + _dot_t(qrope_ref[:, hh * ROPE_PAD:(hh + 1) * ROPE_PAD], kr))
            if diagonal:
                visible = (lax.broadcasted_iota(jnp.int32, (tq, tq), 1)
                           <= lax.broadcasted_iota(jnp.int32, (tq, tq), 0))
                s = jnp.where(visible, s, NEG_BIG)
            m_old = m_ref[hh]
            m_new = jnp.maximum(m_old, jnp.max(s, axis=-1, keepdims=True))
            alpha = jnp.exp2(m_old - m_new)
            p = jnp.exp2(s - _lane_tile(m_new, tq))
            l_ref[hh] = alpha * l_ref[hh] + jnp.sum(p, axis=-1, keepdims=True)
            acc_ref[hh] = _lane_tile(alpha, KV_LORA) * acc_ref[hh] + _dot(p.astype(BF16), ck)
            m_ref[hh] = m_new

    def body(j, _):
        block(j, False)
        return 0

    lax.fori_loop(0, qi, body, 0)
    block(qi, True)
    for hh in range(MLA_HEADS):
        o = acc_ref[hh] / _lane_tile(l_ref[hh], KV_LORA)
        o_ref[:, hh * KV_LORA:(hh + 1) * KV_LORA] = o.astype(BF16)


def _mla_attn_prompt(qlat, qrope, kvcat, t_len, b_all):
    tq = 256
    nq = t_len // tq
    lw, rw, kw = MLA_HEADS * KV_LORA, MLA_HEADS * ROPE_PAD, KV_LORA + ROPE_PAD
    return pl.pallas_call(
        functools.partial(_mla_attn_prompt_kernel, tq=tq),
        grid=(b_all, nq),
        in_specs=[pl.BlockSpec((tq, lw), lambda b, i: (b * nq + i, 0)),
                  pl.BlockSpec((tq, rw), lambda b, i: (b * nq + i, 0)),
                  pl.BlockSpec((t_len, kw), lambda b, i: (b, 0))],
        out_specs=pl.BlockSpec((tq, lw), lambda b, i: (b * nq + i, 0)),
        out_shape=jax.ShapeDtypeStruct((b_all * t_len, lw), BF16),
        scratch_shapes=[pltpu.VMEM((MLA_HEADS, tq, LANES), F32), pltpu.VMEM((MLA_HEADS, tq, LANES), F32),
                        pltpu.VMEM((MLA_HEADS, tq, KV_LORA), F32)],
        compiler_params=_params("parallel", "parallel"),
        name="mla_attn_prompt",
    )(qlat, qrope, kvcat)


def _stack_heads(ref, width):
    return jnp.concatenate([ref[:, hh * width:(hh + 1) * width] for hh in range(MLA_HEADS)], axis=0)


def _mla_attn_sample_kernel(pt_ref, qlat_ref, qrope_ref, ckvn_ref, krn_ref, *refs, ts, ppc):
    kv_refs = refs[:ppc]
    kr_refs = refs[ppc:2 * ppc]
    o_ref, m_ref, l_ref, acc_ref, ql_ref, qr_ref = refs[2 * ppc:]
    c = pl.program_id(1)
    rows = MLA_HEADS * ts

    @pl.when(c == 0)
    def _():
        m_ref[...] = jnp.full_like(m_ref, NEG_BIG)
        l_ref[...] = jnp.zeros_like(l_ref)
        acc_ref[...] = jnp.zeros_like(acc_ref)
        ql_ref[...] = _stack_heads(qlat_ref, KV_LORA).astype(BF16)
        qr_ref[...] = _stack_heads(qrope_ref, ROPE_PAD).astype(BF16)

    def update(s, vals):
        m_old = m_ref[...]
        m_new = jnp.maximum(m_old, jnp.max(s, axis=-1, keepdims=True))
        alpha = jnp.exp2(m_old - m_new)
        p = jnp.exp2(s - m_new)
        l_ref[...] = alpha * l_ref[...] + jnp.sum(p, axis=-1, keepdims=True)
        acc_ref[...] = alpha * acc_ref[...] + _dot(p.astype(BF16), vals)
        m_ref[...] = m_new

    ql = ql_ref[...]
    qr = qr_ref[...]
    qr_cache = qr[:, :QK_ROPE]
    cks = [kv_refs[k][0].astype(BF16) for k in range(ppc)]
    ss = [_dot_t(ql, cks[k]) + _dot(qr_cache, kr_refs[k][0].astype(BF16)) for k in range(ppc)]
    update(jnp.concatenate(ss, axis=1), jnp.concatenate(cks, axis=0))

    @pl.when(c == pl.num_programs(1) - 1)
    def _():
        pad = PAGE_SIZE - ts
        ckn = jnp.concatenate([ckvn_ref[...], jnp.zeros((pad, KV_LORA), F32)], axis=0).astype(BF16)
        krn = jnp.concatenate([krn_ref[...], jnp.zeros((pad, ROPE_PAD), F32)], axis=0).astype(BF16)
        s = _dot_t(ql, ckn) + _dot_t(qr, krn)
        t_q = lax.broadcasted_iota(jnp.int32, (rows, PAGE_SIZE), 0) & (ts - 1)
        t_k = lax.broadcasted_iota(jnp.int32, (rows, PAGE_SIZE), 1)
        update(jnp.where(t_k <= t_q, s, NEG_BIG), ckn)
        o = acc_ref[...] / l_ref[...]
        for hh in range(MLA_HEADS):
            o_ref[:, hh * KV_LORA:(hh + 1) * KV_LORA] = o[hh * ts:(hh + 1) * ts]


def _mla_attn_sample(qlat, qrope, ckv_new, kr_new, cache_kv, cache_kr_t, page_table, ts, b_all):
    ppc = 32
    n_pages = page_table.shape[1]
    lw, rw = MLA_HEADS * KV_LORA, MLA_HEADS * ROPE_PAD
    rows = MLA_HEADS * ts
    pt = page_table.reshape(-1)

    def page_spec(shape, k):
        return pl.BlockSpec((1,) + shape, lambda b, c, pt_ref: (pt_ref[b * n_pages + c * ppc + k], 0, 0))

    stream = lambda w: pl.BlockSpec((ts, w), lambda b, c, pt_ref: (b, 0))
    grid_spec = pltpu.PrefetchScalarGridSpec(
        num_scalar_prefetch=1,
        grid=(b_all, n_pages // ppc),
        in_specs=[stream(lw), stream(rw), stream(KV_LORA), stream(ROPE_PAD)]
        + [page_spec((PAGE_SIZE, KV_LORA), k) for k in range(ppc)]
        + [page_spec((QK_ROPE, PAGE_SIZE), k) for k in range(ppc)],
        out_specs=stream(lw),
        scratch_shapes=[pltpu.VMEM((rows, 1), F32), pltpu.VMEM((rows, 1), F32),
                        pltpu.VMEM((rows, KV_LORA), F32),
                        pltpu.VMEM((rows, KV_LORA), BF16), pltpu.VMEM((rows, ROPE_PAD), BF16)],
    )
    return pl.pallas_call(
        functools.partial(_mla_attn_sample_kernel, ts=ts, ppc=ppc),
        grid_spec=grid_spec,
        out_shape=jax.ShapeDtypeStruct((b_all * ts, lw), F32),
        compiler_params=_params("parallel", "arbitrary"),
        name="mla_attn_sample",
    )(pt, qlat, qrope, ckv_new, kr_new, *([cache_kv] * ppc), *([cache_kr_t] * ppc))


def _mla_out_kernel(o_ref, wuv_ref, wo_ref, g_ref, h_ref, out_ref):
    vs = []
    for hh in range(MLA_HEADS):
        vs.append(_dot(o_ref[:, hh * KV_LORA:(hh + 1) * KV_LORA].astype(BF16), wuv_ref[hh]).astype(BF16))
    y = _dot(jnp.concatenate(vs, axis=1), wo_ref[...])
    out_ref[...] = h_ref[...] + _rms(y, g_ref[...])


def _mla_out(o, wuv, wo, g, h, name):
    rows = h.shape[0]
    tm = 512
    lw = MLA_HEADS * KV_LORA
    return pl.pallas_call(
        _mla_out_kernel,
        grid=(rows // tm,),
        in_specs=[pl.BlockSpec((tm, lw), lambda i: (i, 0)),
                  pl.BlockSpec(wuv.shape, lambda i: (0, 0, 0)),
                  pl.BlockSpec(wo.shape, lambda i: (0, 0)),
                  pl.BlockSpec((1, D_MODEL), lambda i: (0, 0)),
                  pl.BlockSpec((tm, D_MODEL), lambda i: (i, 0))],
        out_specs=pl.BlockSpec((tm, D_MODEL), lambda i: (i, 0)),
        out_shape=jax.ShapeDtypeStruct((rows, D_MODEL), F32),
        compiler_params=_params("parallel"),
        name=name,
    )(o, wuv, wo, g, h)


def _rope_tables(pos):
    half = QK_ROPE // 2
    inv = ROPE_BASE ** (-jnp.arange(half, dtype=F32) / half)
    ang = pos.astype(F32)[:, None] * inv[None, :]
    cos = jnp.cos(ang)
    sin = jnp.sin(ang)
    zeros = jnp.zeros((pos.shape[0], ROPE_PAD - QK_ROPE), F32)
    return jnp.concatenate([cos, cos, zeros], axis=1), jnp.concatenate([sin, sin, zeros], axis=1)


def _rot_cols(w):
    half = QK_ROPE // 2
    return jnp.concatenate([-w[..., half:], w[..., :half]], axis=-1)


def _pad_rope(w):
    pad = [(0, 0)] * (w.ndim - 1) + [(0, ROPE_PAD - QK_ROPE)]
    return jnp.pad(w, pad)


def kernel(x_prompt, x_sample, cache_ssm_re, cache_ssm_im, cache_kv_latent, cache_k_rope, cache_mem_k, cache_mem_v, page_table, mem_prompt, norm_mix_pre, norm_mix_post, norm_mem_pre, norm_mem_post, norm_mlp_pre, norm_mlp_post, mem_in_norm, w_mem_q, w_mem_k, w_mem_v, w_mem_o, w_mlp_up, w_mlp_down, ssm_a_re, ssm_a_im, ssm_log_dt, ssm_b_re, ssm_b_im, ssm_c_re, ssm_c_im, ssm_d, w_glu, kv_in_norm, w_dkv, kv_latent_norm, w_kr, w_uk, w_uv, w_dq, q_norm, w_uq, w_o):
    depth = norm_mix_pre.shape[0]
    n_a = ssm_a_re.shape[0]
    bp, sp, _ = x_prompt.shape
    bs, ts, _ = x_sample.shape
    past_len = page_table.shape[1] * PAGE_SIZE

    hp = x_prompt.astype(F32).reshape(bp * sp, D_MODEL)
    hs = x_sample.astype(F32).reshape(bs * ts, D_MODEL)

    mem_rows = mem_prompt.astype(F32).reshape(bp * MEM_TOKENS, D_MODEL)
    mk, mv, mkb, mvb = _mem_kv(mem_rows, mem_in_norm.astype(F32).reshape(depth, 1, D_MODEL),
                               w_mem_k.astype(BF16), w_mem_v.astype(BF16))
    cmk = _mem_cache_view(cache_mem_k.astype(F32))
    cmv = _mem_cache_view(cache_mem_v.astype(F32))

    ssm_p_re, ssm_p_im, ssm_s_re, ssm_s_im = [], [], [], []
    ckv_p = kr_p = ckv_s = kr_s = None
    for i in range(depth):
        if i < n_a:
            abar_re, abar_im, bbar_re, bbar_im = _ssm_discretise(
                ssm_a_re[i], ssm_a_im[i], ssm_log_dt[i], ssm_b_re[i], ssm_b_im[i])
            ng = SSM_GROUPS // SSM_KBLOCKS
            bdre = _block_diag(bbar_re.reshape(SSM_KBLOCKS, ng, SSM_GROUP, SSM_STATE)).astype(BF16)
            bdim = _block_diag(bbar_im.reshape(SSM_KBLOCKS, ng, SSM_GROUP, SSM_STATE)).astype(BF16)
            c_t = lambda c: c.astype(F32).reshape(SSM_KBLOCKS, ng, SSM_GROUP, SSM_STATE).transpose(0, 1, 3, 2)
            cre = _block_diag(c_t(ssm_c_re[i])).astype(BF16)
            ncim = _block_diag(-c_t(ssm_c_im[i])).astype(BF16)
            s5_w = (_row(norm_mix_pre[i]), bdre, bdim, abar_re.reshape(1, SSM_WIDTH), abar_im.reshape(1, SSM_WIDTH))
            s5_w2 = (cre, ncim, _row(ssm_d[i]), w_glu[i].astype(BF16), _row(norm_mix_post[i]))
            zeros = jnp.zeros((bp, SSM_WIDTH), F32)
            hp3, fpr, fpi = _s5_layer(hp.reshape(bp, sp, D_MODEL), 32, bp, 40, *s5_w, zeros, zeros, *s5_w2,
                                      name="s5_prompt")
            hs3, fsr, fsi = _s5_layer(hs.reshape(bs, ts, D_MODEL), ts, 32, ts, *s5_w,
                                      cache_ssm_re[i].astype(F32).reshape(bs, SSM_WIDTH),
                                      cache_ssm_im[i].astype(F32).reshape(bs, SSM_WIDTH), *s5_w2, name="s5_sample")
            hp = hp3.reshape(bp * sp, D_MODEL)
            hs = hs3.reshape(bs * ts, D_MODEL)
            ssm_p_re.append(fpr.reshape(bp, SSM_GROUPS, SSM_STATE))
            ssm_p_im.append(fpi.reshape(bp, SSM_GROUPS, SSM_STATE))
            ssm_s_re.append(fsr.reshape(bs, SSM_GROUPS, SSM_STATE))
            ssm_s_im.append(fsi.reshape(bs, SSM_GROUPS, SSM_STATE))
        else:
            j = i - n_a
            wuq = w_uq[j].astype(F32)
            w_nope = wuq[:, :, :QK_NOPE].reshape(Q_LORA, MLA_HEADS * QK_NOPE)
            w_r = wuq[:, :, QK_NOPE:]
            w_rope = _pad_rope(w_r).reshape(Q_LORA, MLA_HEADS * ROPE_PAD)
            w_rope_rot = _pad_rope(_rot_cols(w_r)).reshape(Q_LORA, MLA_HEADS * ROPE_PAD)
            wuq_all = jnp.concatenate([w_nope, w_rope, w_rope_rot], axis=1).astype(BF16)
            ukt = w_uk.astype(F32).transpose(1, 2, 0).astype(BF16)
            wkr = w_kr.astype(F32)
            wkr2 = jnp.concatenate([_pad_rope(wkr), _pad_rope(_rot_cols(wkr))], axis=1).astype(BF16)
            pre_w = (_row(norm_mix_pre[i]), _row(kv_in_norm), w_dq[j].astype(BF16), _row(q_norm[j]), wuq_all, ukt,
                     w_dkv.astype(BF16), _row(kv_latent_norm), wkr2)
            cos_p, sin_p = _rope_tables(jnp.arange(sp, dtype=jnp.int32))
            cos_s, sin_s = _rope_tables(past_len + jnp.tile(jnp.arange(ts, dtype=jnp.int32), 512 // ts))
            qlat_p, qrope_p, kvcat_p, ckv_pj, kr_pj = _mla_pre(hp, cos_p, sin_p, *pre_w, BF16, name="mla_pre_prompt")
            qlat_s, qrope_s, _, ckv_sj, kr_sj = _mla_pre(hs, cos_s, sin_s, *pre_w, F32, name="mla_pre_sample")
            if i == n_a:
                ckv_p, kr_p, ckv_s, kr_s, kvcat_shared = ckv_pj, kr_pj, ckv_sj, kr_sj, kvcat_p
            o_p = _mla_attn_prompt(qlat_p, qrope_p, kvcat_shared, sp, bp)
            o_s = _mla_attn_sample(qlat_s, qrope_s, ckv_s, kr_s, cache_kv_latent.astype(F32),
                                   jnp.swapaxes(cache_k_rope.astype(F32), 1, 2), page_table, ts, bs)
            wuv = w_uv.astype(F32).transpose(1, 0, 2).astype(BF16)
            wo = w_o[j].astype(BF16)
            hp = _mla_out(o_p, wuv, wo, _row(norm_mix_post[i]), hp, name="mla_out_prompt")
            hs = _mla_out(o_s, wuv, wo, _row(norm_mix_post[i]), hs, name="mla_out_sample")

        wq = w_mem_q[i].astype(BF16)
        wo_mem = w_mem_o[i].astype(BF16)
        hp = _mem_attn_prompt(hp, sp, bp, i, _row(norm_mem_pre[i]), wq, mkb, mvb, wo_mem, _row(norm_mem_post[i]))
        q_s = _norm_matmul(hs, _row(norm_mem_pre[i]), wq, MEM_SCALE, F32, name="mem_q_sample")
        o_s = _mem_attn_sample(q_s, ts, bs, i, cmk, cmv)
        hs = _matmul_norm_res(o_s, wo_mem, _row(norm_mem_post[i]), hs, name="mem_out_sample")

        wup = w_mlp_up[i].astype(BF16)
        wdn = w_mlp_down[i].astype(BF16)
        hp = _mlp(hp, _row(norm_mlp_pre[i]), wup, wdn, _row(norm_mlp_post[i]), name="mlp_prompt")
        hs = _mlp(hs, _row(norm_mlp_pre[i]), wup, wdn, _row(norm_mlp_post[i]), name="mlp_sample")

    return (hp.reshape(bp, sp, D_MODEL), hs.reshape(bs, ts, D_MODEL),
            jnp.stack(ssm_p_re), jnp.stack(ssm_p_im), jnp.stack(ssm_s_re), jnp.stack(ssm_s_im),
            ckv_p.reshape(bp, sp, KV_LORA), kr_p[:, :QK_ROPE].reshape(bp, sp, QK_ROPE),
            ckv_s.reshape(bs, ts, KV_LORA), kr_s[:, :QK_ROPE].reshape(bs, ts, QK_ROPE),
            mk.reshape(depth, bp, MEM_TOKENS, MEM_HEADS, MEM_HEAD_DIM),
            mv.reshape(depth, bp, MEM_TOKENS, MEM_HEADS, MEM_HEAD_DIM))
```

```python
import functools
import math

import jax
import jax.numpy as jnp
from jax import lax
from jax.experimental import pallas as pl
from jax.experimental.pallas import tpu as pltpu

F32 = jnp.float32
BF16 = jnp.bfloat16

LANES = 128
SUBLANES = 8
D_MODEL = 1024
PAGE_SIZE = 128
SSM_GROUP = 16
SSM_GROUPS = D_MODEL // SSM_GROUP
SSM_STATE = 64
SSM_WIDTH = SSM_GROUPS * SSM_STATE
SSM_KBLOCKS = 4
SSM_SLABS = SSM_WIDTH // LANES
MLA_HEADS = 8
QK_NOPE = 128
QK_ROPE = 64
ROPE_PAD = LANES
V_HEAD = 128
KV_LORA = 256
Q_LORA = 384
ROPE_BASE = 10000.0
MLA_SCALE = (QK_NOPE + QK_ROPE) ** -0.5
MLA_Q_SCALE = MLA_SCALE * math.log2(math.e)
MEM_TOKENS = 256
MEM_HEADS = 4
MEM_HEAD_DIM = D_MODEL // MEM_HEADS
MEM_SCALE = MEM_HEAD_DIM ** -0.5
D_FF = 4 * D_MODEL
EPS = 1e-6
NEG_BIG = -1e30

VMEM_LIMIT_BYTES = 56 * 1024 * 1024


def _params(*semantics):
    return pltpu.CompilerParams(dimension_semantics=semantics, vmem_limit_bytes=VMEM_LIMIT_BYTES)


def _dot(a, b):
    return jnp.dot(a, b, preferred_element_type=F32)


def _dot_t(a, b):
    return lax.dot_general(a, b, (((1,), (1,)), ((), ())), preferred_element_type=F32)


def _rms(x, g):
    return x * lax.rsqrt(jnp.mean(x * x, axis=-1, keepdims=True) + EPS) * g


def _row(v):
    return v.reshape(1, -1).astype(F32)


def _lane_tile(x, width):
    return jnp.concatenate([x] * (width // LANES), axis=1) if width > LANES else x


def _ssm_disc_kernel(ar_ref, ai_ref, ldt_ref, br_ref, bi_ref,
                     abar_re_ref, abar_im_ref, bbar_re_ref, bbar_im_ref):
    ar = ar_ref[...]
    ai = ai_ref[...]
    dt = jnp.exp(ldt_ref[...])
    mag = jnp.exp(ar * dt)
    abar_re = mag * jnp.cos(ai * dt)
    abar_im = mag * jnp.sin(ai * dt)
    den = ar * ar + ai * ai
    nr = abar_re - 1.0
    ni = abar_im
    coef_re = (nr * ar + ni * ai) / den
    coef_im = (ni * ar - nr * ai) / den
    br = br_ref[...]
    bi = bi_ref[...]
    abar_re_ref[...] = abar_re
    abar_im_ref[...] = abar_im
    bbar_re_ref[...] = coef_re * br - coef_im * bi
    bbar_im_ref[...] = coef_re * bi + coef_im * br


def _ssm_discretise(a_re, a_im, log_dt, b_re, b_im):
    g, n, k = SSM_GROUPS, SSM_STATE, SSM_GROUP
    rep = lambda a: jnp.repeat(a.astype(F32), k, axis=0)
    brt = b_re.astype(F32).transpose(0, 2, 1).reshape(g * k, n)
    bit = b_im.astype(F32).transpose(0, 2, 1).reshape(g * k, n)
    ldt = jnp.repeat(log_dt.astype(F32), k, axis=0).reshape(g * k, 1)
    shp = jax.ShapeDtypeStruct((g * k, n), F32)
    abar_re, abar_im, bbar_re, bbar_im = pl.pallas_call(
        _ssm_disc_kernel, out_shape=(shp, shp, shp, shp), name="ssm_discretise",
    )(rep(a_re), rep(a_im), ldt, brt, bit)
    return abar_re[::k], abar_im[::k], bbar_re, bbar_im


def _block_diag(x):
    kb, ng, r, c = x.shape
    eye = jnp.eye(ng, dtype=x.dtype)
    return (x[:, :, :, None, :] * eye[None, :, None, :, None]).reshape(kb, ng * r, ng * c)


def _s5_kernel(x_ref, gpre_ref, bdre_ref, bdim_ref, are_ref, aim_ref, h0re_ref, h0im_ref,
               cre_ref, ncim_ref, dskip_ref, wglu_ref, gpost_ref,
               out_ref, fre_ref, fim_ref, *bu_refs, tt, bb, pitch):
    bure_refs = bu_refs[:SSM_KBLOCKS]
    buim_refs = bu_refs[SSM_KBLOCKS:]
    ti = pl.program_id(1)
    rows = tt * bb
    spk = SSM_SLABS // SSM_KBLOCKS

    @pl.when(ti == 0)
    def _():
        fre_ref[...] = h0re_ref[...]
        fim_ref[...] = h0im_ref[...]

    x = x_ref[...].reshape(rows, D_MODEL)
    xn = _rms(x, gpre_ref[...])
    xb = xn.astype(BF16)
    kw = D_MODEL // SSM_KBLOCKS

    def project_b(kb):
        xk = xb[:, kb * kw:(kb + 1) * kw]
        for w_ref, dst_ref in ((bdre_ref, bure_refs[kb]), (bdim_ref, buim_refs[kb])):
            res = _dot(xk, w_ref[kb])
            for s in range(spk):
                piece = res[:, s * LANES:(s + 1) * LANES]
                if pitch == tt:
                    dst_ref[s] = piece
                else:
                    for b in range(bb):
                        dst_ref[s, b * pitch:b * pitch + tt, :] = piece[b * tt:(b + 1) * tt]

    def scan(kb, s):
        bure_ref, buim_ref = bure_refs[kb], buim_refs[kb]
        lanes = slice((kb * spk + s) * LANES, (kb * spk + s + 1) * LANES)
        a_re = jnp.broadcast_to(are_ref[:, lanes], (SUBLANES, LANES))
        a_im = jnp.broadcast_to(aim_ref[:, lanes], (SUBLANES, LANES))
        for vb in range(bb // SUBLANES):
            b0 = vb * SUBLANES
            h_re = fre_ref[b0:b0 + SUBLANES, lanes]
            h_im = fim_ref[b0:b0 + SUBLANES, lanes]
            for t in range(tt):
                rws = pl.ds(b0 * pitch + t, SUBLANES, stride=pitch)
                h_re, h_im = (a_re * h_re - a_im * h_im + bure_ref[s, rws, :],
                              a_re * h_im + a_im * h_re + buim_ref[s, rws, :])
                bure_ref[s, rws, :] = h_re
                buim_ref[s, rws, :] = h_im
            fre_ref[b0:b0 + SUBLANES, lanes] = h_re
            fim_ref[b0:b0 + SUBLANES, lanes] = h_im

    def gather(ref):
        cols = []
        for s in range(spk):
            if pitch == tt:
                cols.append(ref[s])
            else:
                cols.append(jnp.concatenate([ref[s, b * pitch:b * pitch + tt, :] for b in range(bb)], axis=0))
        return jnp.concatenate(cols, axis=1).astype(BF16)

    ys = []
    for kb in range(SSM_KBLOCKS):
        project_b(kb)
        for s in range(spk):
            scan(kb, s)
        ys.append(_dot(gather(bure_refs[kb]), cre_ref[kb]) + _dot(gather(buim_refs[kb]), ncim_ref[kb]))
    y = jnp.concatenate(ys, axis=1) + dskip_ref[...] * xn
    g = 0.5 * y * (1.0 + lax.erf(y * (2.0 ** -0.5)))
    z = _dot(g.astype(BF16), wglu_ref[...])
    o = z[:, :D_MODEL] * (1.0 / (1.0 + jnp.exp(-z[:, D_MODEL:])))
    out_ref[...] = (x + _rms(o, gpost_ref[...])).reshape(bb, tt, D_MODEL)


def _s5_layer(x3, tt, bb, pitch, gpre, bdre, bdim, a_re, a_im, h0_re, h0_im,
              cre, ncim, dskip, wglu, gpost, name):
    b_all, t_len, _ = x3.shape
    grid = (b_all // bb, t_len // tt)
    const2 = lambda bj, ti: (0, 0)
    const3 = lambda bj, ti: (0, 0, 0)
    slab = pltpu.VMEM((SSM_SLABS // SSM_KBLOCKS, bb * pitch, LANES), F32)
    return pl.pallas_call(
        functools.partial(_s5_kernel, tt=tt, bb=bb, pitch=pitch),
        grid=grid,
        in_specs=[
            pl.BlockSpec((bb, tt, D_MODEL), lambda bj, ti: (bj, ti, 0)),
            pl.BlockSpec((1, D_MODEL), const2),
            pl.BlockSpec(bdre.shape, const3),
            pl.BlockSpec(bdim.shape, const3),
            pl.BlockSpec((1, SSM_WIDTH), const2),
            pl.BlockSpec((1, SSM_WIDTH), const2),
            pl.BlockSpec((bb, SSM_WIDTH), lambda bj, ti: (bj, 0)),
            pl.BlockSpec((bb, SSM_WIDTH), lambda bj, ti: (bj, 0)),
            pl.BlockSpec(cre.shape, const3),
            pl.BlockSpec(ncim.shape, const3),
            pl.BlockSpec((1, D_MODEL), const2),
            pl.BlockSpec(wglu.shape, const2),
            pl.BlockSpec((1, D_MODEL), const2),
        ],
        out_specs=(
            pl.BlockSpec((bb, tt, D_MODEL), lambda bj, ti: (bj, ti, 0)),
            pl.BlockSpec((bb, SSM_WIDTH), lambda bj, ti: (bj, 0)),
            pl.BlockSpec((bb, SSM_WIDTH), lambda bj, ti: (bj, 0)),
        ),
        out_shape=(
            jax.ShapeDtypeStruct((b_all, t_len, D_MODEL), F32),
            jax.ShapeDtypeStruct((b_all, SSM_WIDTH), F32),
            jax.ShapeDtypeStruct((b_all, SSM_WIDTH), F32),
        ),
        scratch_shapes=[slab] * (2 * SSM_KBLOCKS),
        compiler_params=_params("parallel", "arbitrary"),
        name=name,
    )(x3, gpre, bdre, bdim, a_re, a_im, h0_re, h0_im, cre, ncim, dskip, wglu, gpost)


def _mem_kv_kernel(x_ref, g_ref, wk_ref, wv_ref, k_ref, v_ref, kb_ref, vb_ref):
    xn = _rms(x_ref[...], g_ref[0]).astype(BF16)
    k = _dot(xn, wk_ref[0])
    v = _dot(xn, wv_ref[0])
    k_ref[0] = k
    v_ref[0] = v
    kb_ref[0] = k.astype(BF16)
    vb_ref[0] = v.astype(BF16)


def _mem_kv(mem, g, wk, wv):
    rows = mem.shape[0]
    nl = g.shape[0]
    tm = 512
    blk = pl.BlockSpec((1, tm, D_MODEL), lambda l, i: (l, i, 0))
    wspec = pl.BlockSpec((1, D_MODEL, D_MODEL), lambda l, i: (l, 0, 0))
    return pl.pallas_call(
        _mem_kv_kernel,
        grid=(nl, rows // tm),
        in_specs=[pl.BlockSpec((tm, D_MODEL), lambda l, i: (i, 0)),
                  pl.BlockSpec((1, 1, D_MODEL), lambda l, i: (l, 0, 0)), wspec, wspec],
        out_specs=(blk, blk, blk, blk),
        out_shape=(jax.ShapeDtypeStruct((nl, rows, D_MODEL), F32),) * 2
        + (jax.ShapeDtypeStruct((nl, rows, D_MODEL), BF16),) * 2,
        compiler_params=_params("parallel", "parallel"),
        name="mem_kv",
    )(mem, g, wk, wv)


def _softmax_rows(s):
    m = jnp.max(s, axis=-1, keepdims=True)
    p = jnp.exp(s - m)
    return p / jnp.sum(p, axis=-1, keepdims=True)


def _mem_attn_prompt_kernel(h_ref, gpre_ref, wq_ref, k_ref, v_ref, wo_ref, gpost_ref, out_ref):
    x = h_ref[...]
    xn = _rms(x, gpre_ref[...]).astype(BF16)
    q = (_dot(xn, wq_ref[...]) * MEM_SCALE).astype(BF16)
    os = []
    for hh in range(MEM_HEADS):
        cols = slice(hh * MEM_HEAD_DIM, (hh + 1) * MEM_HEAD_DIM)
        p = _softmax_rows(_dot_t(q[:, cols], k_ref[0, :, cols]))
        os.append(_dot(p.astype(BF16), v_ref[0, :, cols]).astype(BF16))
    y = _dot(jnp.concatenate(os, axis=1), wo_ref[...])
    out_ref[...] = x + _rms(y, gpost_ref[...])


def _mem_attn_prompt(h, t_len, b_all, layer, gpre, wq, kb, vb, wo, gpost):
    tq = 512
    nq = t_len // tq
    const2 = lambda b, i: (0, 0)
    hspec = pl.BlockSpec((tq, D_MODEL), lambda b, i: (b * nq + i, 0))
    kvspec = pl.BlockSpec((1, MEM_TOKENS, D_MODEL), lambda b, i: (layer, b, 0))
    wspec = pl.BlockSpec((D_MODEL, D_MODEL), const2)
    gspec = pl.BlockSpec((1, D_MODEL), const2)
    return pl.pallas_call(
        _mem_attn_prompt_kernel,
        grid=(b_all, nq),
        in_specs=[hspec, gspec, wspec, kvspec, kvspec, wspec, gspec],
        out_specs=hspec,
        out_shape=jax.ShapeDtypeStruct(h.shape, F32),
        compiler_params=_params("parallel", "parallel"),
        name="mem_attn_prompt",
    )(h, gpre, wq, kb, vb, wo, gpost)


def _mem_cache_view(cache):
    nl, nb, m, nh, dh = cache.shape
    chunks = dh // LANES
    return cache.reshape(nl, nb, m, nh, chunks, LANES).transpose(0, 1, 2, 4, 3, 5).reshape(
        nl, nb, m * chunks * nh, LANES)


def _mem_attn_sample_kernel(q_ref, k_ref, v_ref, o_ref, *, nb, ts):
    chunks = MEM_HEAD_DIM // LANES
    period = chunks * MEM_HEADS

    def head(ref, g, hh):
        parts = [ref[0, g, pl.ds(c * MEM_HEADS + hh, MEM_TOKENS, stride=period), :] for c in range(chunks)]
        return jnp.concatenate(parts, axis=1).astype(BF16)

    pairs = [(g, hh) for g in range(nb) for hh in range(MEM_HEADS)]
    ss = [_dot_t(q_ref[g * ts:(g + 1) * ts, hh * MEM_HEAD_DIM:(hh + 1) * MEM_HEAD_DIM].astype(BF16),
                 head(k_ref, g, hh)) for g, hh in pairs]
    p = _softmax_rows(jnp.concatenate(ss, axis=0))
    for i, (g, hh) in enumerate(pairs):
        o_ref[g * ts:(g + 1) * ts, hh * MEM_HEAD_DIM:(hh + 1) * MEM_HEAD_DIM] = _dot(
            p[i * ts:(i + 1) * ts].astype(BF16), head(v_ref, g, hh))


def _mem_attn_sample(q, ts, b_all, layer, cache_k, cache_v):
    nb = 4
    qspec = pl.BlockSpec((nb * ts, D_MODEL), lambda j: (j, 0))
    cspec = pl.BlockSpec((1, nb) + cache_k.shape[2:], lambda j: (layer, j, 0, 0))
    return pl.pallas_call(
        functools.partial(_mem_attn_sample_kernel, nb=nb, ts=ts),
        grid=(b_all // nb,),
        in_specs=[qspec, cspec, cspec],
        out_specs=qspec,
        out_shape=jax.ShapeDtypeStruct(q.shape, F32),
        compiler_params=_params("parallel"),
        name="mem_attn_sample",
    )(q, cache_k, cache_v)


def _norm_matmul_kernel(x_ref, g_ref, w_ref, o_ref, *, scale):
    xn = _rms(x_ref[...], g_ref[...]).astype(BF16)
    o_ref[...] = (_dot(xn, w_ref[...]) * scale).astype(o_ref.dtype)


def _norm_matmul(x, g, w, scale, out_dtype, name):
    rows = x.shape[0]
    tm = 512
    n = w.shape[1]
    return pl.pallas_call(
        functools.partial(_norm_matmul_kernel, scale=scale),
        grid=(rows // tm,),
        in_specs=[pl.BlockSpec((tm, D_MODEL), lambda i: (i, 0)),
                  pl.BlockSpec((1, D_MODEL), lambda i: (0, 0)),
                  pl.BlockSpec(w.shape, lambda i: (0, 0))],
        out_specs=pl.BlockSpec((tm, n), lambda i: (i, 0)),
        out_shape=jax.ShapeDtypeStruct((rows, n), out_dtype),
        compiler_params=_params("parallel"),
        name=name,
    )(x, g, w)


def _matmul_norm_res_kernel(a_ref, w_ref, g_ref, h_ref, o_ref):
    y = _dot(a_ref[...].astype(BF16), w_ref[...])
    o_ref[...] = h_ref[...] + _rms(y, g_ref[...])


def _matmul_norm_res(a, w, g, h, name):
    rows, k = a.shape
    tm = 512
    return pl.pallas_call(
        _matmul_norm_res_kernel,
        grid=(rows // tm,),
        in_specs=[pl.BlockSpec((tm, k), lambda i: (i, 0)),
                  pl.BlockSpec(w.shape, lambda i: (0, 0)),
                  pl.BlockSpec((1, D_MODEL), lambda i: (0, 0)),
                  pl.BlockSpec((tm, D_MODEL), lambda i: (i, 0))],
        out_specs=pl.BlockSpec((tm, D_MODEL), lambda i: (i, 0)),
        out_shape=jax.ShapeDtypeStruct((rows, D_MODEL), F32),
        compiler_params=_params("parallel"),
        name=name,
    )(a, w, g, h)


def _mlp_kernel(x_ref, gpre_ref, wup_ref, wdn_ref, gpost_ref, o_ref, *, tf):
    x = x_ref[...]
    xn = _rms(x, gpre_ref[...]).astype(BF16)
    acc = None
    for c in range(D_FF // tf):
        hid = jnp.maximum(_dot(xn, wup_ref[:, c * tf:(c + 1) * tf]), 0.0)
        part = _dot((hid * hid).astype(BF16), wdn_ref[c * tf:(c + 1) * tf, :])
        acc = part if acc is None else acc + part
    o_ref[...] = x + _rms(acc, gpost_ref[...])


def _resident(shape):
    return pl.BlockSpec(shape, lambda *_: (0,) * len(shape), pipeline_mode=pl.Buffered(1))


def _mlp(h, gpre, wup, wdn, gpost, name):
    rows = h.shape[0]
    tm, tf = 512, 1024
    return pl.pallas_call(
        functools.partial(_mlp_kernel, tf=tf),
        grid=(rows // tm,),
        in_specs=[pl.BlockSpec((tm, D_MODEL), lambda i: (i, 0)),
                  _resident((1, D_MODEL)), _resident(wup.shape), _resident(wdn.shape), _resident((1, D_MODEL))],
        out_specs=pl.BlockSpec((tm, D_MODEL), lambda i: (i, 0)),
        out_shape=jax.ShapeDtypeStruct((rows, D_MODEL), F32),
        compiler_params=_params("parallel"),
        name=name,
    )(h, gpre, wup, wdn, gpost)


def _mla_pre_kernel(h_ref, cos_ref, sin_ref, gmix_ref, gkv_ref, wdq_ref, qn_ref, wuq_ref, ukt_ref,
                    wdkv_ref, glat_ref, wkr_ref,
                    qlat_ref, qrope_ref, kvcat_ref, ckv_ref, kr_ref):
    x = h_ref[...]
    y = x * lax.rsqrt(jnp.mean(x * x, axis=-1, keepdims=True) + EPS)
    cos = cos_ref[...]
    sin = sin_ref[...]
    hn = (y * gkv_ref[...]).astype(BF16)
    ckv = _rms(_dot(hn, wdkv_ref[...]), glat_ref[...])
    kr2 = _dot(hn, wkr_ref[...])
    kr = kr2[:, :ROPE_PAD] * cos + kr2[:, ROPE_PAD:] * sin
    ckv_ref[...] = ckv
    kr_ref[...] = kr
    kvcat_ref[...] = jnp.concatenate([ckv, kr], axis=1).astype(BF16)
    xn = (y * gmix_ref[...]).astype(BF16)
    cq = _rms(_dot(xn, wdq_ref[...]), qn_ref[...]).astype(BF16)
    q = _dot(cq, wuq_ref[...])
    nope_w = MLA_HEADS * QK_NOPE
    rope_w = MLA_HEADS * ROPE_PAD
    qr = (q[:, nope_w:nope_w + rope_w] * _lane_tile(cos, rope_w)
          + q[:, nope_w + rope_w:] * _lane_tile(sin, rope_w))
    qrope_ref[...] = (qr * MLA_Q_SCALE).astype(qrope_ref.dtype)
    for hh in range(MLA_HEADS):
        qn_h = q[:, hh * QK_NOPE:(hh + 1) * QK_NOPE].astype(BF16)
        ql = _dot(qn_h, ukt_ref[hh])
        qlat_ref[:, hh * KV_LORA:(hh + 1) * KV_LORA] = (ql * MLA_Q_SCALE).astype(qlat_ref.dtype)


def _mla_pre(h, cos, sin, gmix, gkv, wdq, qn, wuq, ukt, wdkv, glat, wkr, q_dtype, name):
    rows = h.shape[0]
    tm = 512
    period = cos.shape[0] // tm
    row = lambda w: pl.BlockSpec((tm, w), lambda i: (i, 0))
    tab = pl.BlockSpec((tm, ROPE_PAD), lambda i: (i % period, 0))
    full = lambda a: pl.BlockSpec(a.shape, lambda i: (0,) * a.ndim)
    return pl.pallas_call(
        _mla_pre_kernel,
        grid=(rows // tm,),
        in_specs=[row(D_MODEL), tab, tab, full(gmix), full(gkv), full(wdq), full(qn),
                  full(wuq), full(ukt), full(wdkv), full(glat), full(wkr)],
        out_specs=(row(MLA_HEADS * KV_LORA), row(MLA_HEADS * ROPE_PAD), row(KV_LORA + ROPE_PAD),
                   row(KV_LORA), row(ROPE_PAD)),
        out_shape=(jax.ShapeDtypeStruct((rows, MLA_HEADS * KV_LORA), q_dtype),
                   jax.ShapeDtypeStruct((rows, MLA_HEADS * ROPE_PAD), q_dtype),
                   jax.ShapeDtypeStruct((rows, KV_LORA + ROPE_PAD), BF16),
                   jax.ShapeDtypeStruct((rows, KV_LORA), F32),
                   jax.ShapeDtypeStruct((rows, ROPE_PAD), F32)),
        compiler_params=_params("parallel"),
        name=name,
    )(h, cos, sin, gmix, gkv, wdq, qn, wuq, ukt, wdkv, glat, wkr)


def _mla_attn_prompt_kernel(qlat_ref, qrope_ref, kv_ref, o_ref, m_ref, l_ref, acc_ref, *, tq):
    qi = pl.program_id(1)
    m_ref[...] = jnp.full_like(m_ref, NEG_BIG)
    l_ref[...] = jnp.zeros_like(l_ref)
    acc_ref[...] = jnp.zeros_like(acc_ref)

    def block(j, diagonal):
        k0 = pl.multiple_of(j * tq, tq)
        ck = kv_ref[pl.ds(k0, tq), :KV_LORA]
        kr = kv_ref[pl.ds(k0, tq), KV_LORA:]
        for hh in range(MLA_HEADS):
            s = (_dot_t(qlat_ref[:, hh * KV_LORA:(hh + 1) * KV_LORA], ck)
                 + _dot_t(qrope_ref[:, hh * ROPE_PAD:(hh + 1) * ROPE_PAD], kr))
            if diagonal:
                visible = (lax.broadcasted_iota(jnp.int32, (tq, tq), 1)
                           <= lax.broadcasted_iota(jnp.int32, (tq, tq), 0))
                s = jnp.where(visible, s, NEG_BIG)
            m_old = m_ref[hh]
            m_new = jnp.maximum(m_old, jnp.max(s, axis=-1, keepdims=True))
            alpha = jnp.exp2(m_old - m_new)
            p = jnp.exp2(s - _lane_tile(m_new, tq))
            l_ref[hh] = alpha * l_ref[hh] + jnp.sum(p, axis=-1, keepdims=True)
            acc_ref[hh] = _lane_tile(alpha, KV_LORA) * acc_ref[hh] + _dot(p.astype(BF16), ck)
            m_ref[hh] = m_new

    def body(j, _):
        block(j, False)
        return 0

    lax.fori_loop(0, qi, body, 0)
    block(qi, True)
    for hh in range(MLA_HEADS):
        o = acc_ref[hh] / _lane_tile(l_ref[hh], KV_LORA)
        o_ref[:, hh * KV_LORA:(hh + 1) * KV_LORA] = o.astype(BF16)


def _mla_attn_prompt(qlat, qrope, kvcat, t_len, b_all):
    tq = 512
    nq = t_len // tq
    lw, rw, kw = MLA_HEADS * KV_LORA, MLA_HEADS * ROPE_PAD, KV_LORA + ROPE_PAD
    return pl.pallas_call(
        functools.partial(_mla_attn_prompt_kernel, tq=tq),
        grid=(b_all, nq),
        in_specs=[pl.BlockSpec((tq, lw), lambda b, i: (b * nq + i, 0)),
                  pl.BlockSpec((tq, rw), lambda b, i: (b * nq + i, 0)),
                  pl.BlockSpec((t_len, kw), lambda b, i: (b, 0))],
        out_specs=pl.BlockSpec((tq, lw), lambda b, i: (b * nq + i, 0)),
        out_shape=jax.ShapeDtypeStruct((b_all * t_len, lw), BF16),
        scratch_shapes=[pltpu.VMEM((MLA_HEADS, tq, LANES), F32), pltpu.VMEM((MLA_HEADS, tq, LANES), F32),
                        pltpu.VMEM((MLA_HEADS, tq, KV_LORA), F32)],
        compiler_params=_params("parallel", "parallel"),
        name="mla_attn_prompt",
    )(qlat, qrope, kvcat)


def _stack_heads(ref, width):
    return jnp.concatenate([ref[:, hh * width:(hh + 1) * width] for hh in range(MLA_HEADS)], axis=0)


def _mla_attn_sample_kernel(pt_ref, qlat_ref, qrope_ref, ckvn_ref, krn_ref, kv_hbm, kr_hbm,
                            o_ref, kvbuf, krbuf, sem, s_ref, kvb_ref, *, ts, n_pages):
    b = pl.program_id(0)
    slot = b % 2
    rows = MLA_HEADS * ts
    past = n_pages * PAGE_SIZE

    def page_copies(batch, sl, page_of):
        cps = []
        for p in range(n_pages):
            page = page_of(batch, p)
            cps.append(pltpu.make_async_copy(kv_hbm.at[page], kvbuf.at[sl, p], sem.at[0, sl]))
            cps.append(pltpu.make_async_copy(kr_hbm.at[page], krbuf.at[sl, p], sem.at[1, sl]))
        return cps

    table_page = lambda batch, p: pt_ref[batch * n_pages + p]

    @pl.when(b == 0)
    def _():
        for cp in page_copies(b, slot, table_page):
            cp.start()

    @pl.when(b + 1 < pl.num_programs(0))
    def _():
        for cp in page_copies(b + 1, 1 - slot, table_page):
            cp.start()

    for cp in page_copies(b, slot, lambda batch, p: 0):
        cp.wait()

    ql = _stack_heads(qlat_ref, KV_LORA).astype(BF16)
    qr = _stack_heads(qrope_ref, ROPE_PAD).astype(BF16)
    qr_cache = qr[:, :QK_ROPE]
    pair = 2 * PAGE_SIZE
    for g in range(n_pages // 2):
        ck = kvbuf[slot, 2 * g:2 * g + 2].reshape(pair, KV_LORA).astype(BF16)
        kr = jnp.concatenate([krbuf[slot, 2 * g], krbuf[slot, 2 * g + 1]], axis=1).astype(BF16)
        s_ref[:, g * pair:(g + 1) * pair] = _dot_t(ql, ck) + _dot(qr_cache, kr)
        kvb_ref[g * pair:(g + 1) * pair, :] = ck

    pad = PAGE_SIZE - ts
    ckn = jnp.concatenate([ckvn_ref[...], jnp.zeros((pad, KV_LORA), F32)], axis=0).astype(BF16)
    krn = jnp.concatenate([krn_ref[...], jnp.zeros((pad, ROPE_PAD), F32)], axis=0).astype(BF16)
    s_new = _dot_t(ql, ckn) + _dot_t(qr, krn)
    t_q = lax.broadcasted_iota(jnp.int32, (rows, PAGE_SIZE), 0) & (ts - 1)
    t_k = lax.broadcasted_iota(jnp.int32, (rows, PAGE_SIZE), 1)
    s_ref[:, past:] = jnp.where(t_k <= t_q, s_new, NEG_BIG)
    kvb_ref[past:, :] = ckn

    s = s_ref[...]
    p = jnp.exp2(s - jnp.max(s, axis=-1, keepdims=True))
    o = _dot(p.astype(BF16), kvb_ref[...]) / jnp.sum(p, axis=-1, keepdims=True)
    for hh in range(MLA_HEADS):
        o_ref[:, hh * KV_LORA:(hh + 1) * KV_LORA] = o[hh * ts:(hh + 1) * ts]


def _mla_attn_sample(qlat, qrope, ckv_new, kr_new, cache_kv, cache_kr_t, page_table, ts, b_all):
    n_pages = page_table.shape[1]
    lw, rw = MLA_HEADS * KV_LORA, MLA_HEADS * ROPE_PAD
    rows = MLA_HEADS * ts
    keys = (n_pages + 1) * PAGE_SIZE
    pt = page_table.reshape(-1)
    stream = lambda w: pl.BlockSpec((ts, w), lambda b, pt_ref: (b, 0))
    hbm = pl.BlockSpec(memory_space=pl.ANY)
    grid_spec = pltpu.PrefetchScalarGridSpec(
        num_scalar_prefetch=1,
        grid=(b_all,),
        in_specs=[stream(lw), stream(rw), stream(KV_LORA), stream(ROPE_PAD), hbm, hbm],
        out_specs=stream(lw),
        scratch_shapes=[pltpu.VMEM((2, n_pages, PAGE_SIZE, KV_LORA), F32),
                        pltpu.VMEM((2, n_pages, QK_ROPE, PAGE_SIZE), F32),
                        pltpu.SemaphoreType.DMA((2, 2)),
                        pltpu.VMEM((rows, keys), F32),
                        pltpu.VMEM((keys, KV_LORA), BF16)],
    )
    return pl.pallas_call(
        functools.partial(_mla_attn_sample_kernel, ts=ts, n_pages=n_pages),
        grid_spec=grid_spec,
        out_shape=jax.ShapeDtypeStruct((b_all * ts, lw), F32),
        compiler_params=_params("arbitrary"),
        name="mla_attn_sample",
    )(pt, qlat, qrope, ckv_new, kr_new, cache_kv, cache_kr_t)


def _mla_out_kernel(o_ref, wuv_ref, wo_ref, g_ref, h_ref, out_ref):
    vs = []
    for hh in range(MLA_HEADS):
        vs.append(_dot(o_ref[:, hh * KV_LORA:(hh + 1) * KV_LORA].astype(BF16), wuv_ref[hh]).astype(BF16))
    y = _dot(jnp.concatenate(vs, axis=1), wo_ref[...])
    out_ref[...] = h_ref[...] + _rms(y, g_ref[...])


def _mla_out(o, wuv, wo, g, h, name):
    rows = h.shape[0]
    tm = 512
    lw = MLA_HEADS * KV_LORA
    return pl.pallas_call(
        _mla_out_kernel,
        grid=(rows // tm,),
        in_specs=[pl.BlockSpec((tm, lw), lambda i: (i, 0)),
                  pl.BlockSpec(wuv.shape, lambda i: (0, 0, 0)),
                  pl.BlockSpec(wo.shape, lambda i: (0, 0)),
                  pl.BlockSpec((1, D_MODEL), lambda i: (0, 0)),
                  pl.BlockSpec((tm, D_MODEL), lambda i: (i, 0))],
        out_specs=pl.BlockSpec((tm, D_MODEL), lambda i: (i, 0)),
        out_shape=jax.ShapeDtypeStruct((rows, D_MODEL), F32),
        compiler_params=_params("parallel"),
        name=name,
    )(o, wuv, wo, g, h)


def _rope_tables(pos):
    half = QK_ROPE // 2
    inv = ROPE_BASE ** (-jnp.arange(half, dtype=F32) / half)
    ang = pos.astype(F32)[:, None] * inv[None, :]
    cos = jnp.cos(ang)
    sin = jnp.sin(ang)
    zeros = jnp.zeros((pos.shape[0], ROPE_PAD - QK_ROPE), F32)
    return jnp.concatenate([cos, cos, zeros], axis=1), jnp.concatenate([sin, sin, zeros], axis=1)


def _rot_cols(w):
    half = QK_ROPE // 2
    return jnp.concatenate([-w[..., half:], w[..., :half]], axis=-1)


def _pad_rope(w):
    pad = [(0, 0)] * (w.ndim - 1) + [(0, ROPE_PAD - QK_ROPE)]
    return jnp.pad(w, pad)


def kernel(x_prompt, x_sample, cache_ssm_re, cache_ssm_im, cache_kv_latent, cache_k_rope, cache_mem_k, cache_mem_v, page_table, mem_prompt, norm_mix_pre, norm_mix_post, norm_mem_pre, norm_mem_post, norm_mlp_pre, norm_mlp_post, mem_in_norm, w_mem_q, w_mem_k, w_mem_v, w_mem_o, w_mlp_up, w_mlp_down, ssm_a_re, ssm_a_im, ssm_log_dt, ssm_b_re, ssm_b_im, ssm_c_re, ssm_c_im, ssm_d, w_glu, kv_in_norm, w_dkv, kv_latent_norm, w_kr, w_uk, w_uv, w_dq, q_norm, w_uq, w_o):
    depth = norm_mix_pre.shape[0]
    n_a = ssm_a_re.shape[0]
    bp, sp, _ = x_prompt.shape
    bs, ts, _ = x_sample.shape
    past_len = page_table.shape[1] * PAGE_SIZE

    hp = x_prompt.astype(F32).reshape(bp * sp, D_MODEL)
    hs = x_sample.astype(F32).reshape(bs * ts, D_MODEL)

    mem_rows = mem_prompt.astype(F32).reshape(bp * MEM_TOKENS, D_MODEL)
    mk, mv, mkb, mvb = _mem_kv(mem_rows, mem_in_norm.astype(F32).reshape(depth, 1, D_MODEL),
                               w_mem_k.astype(BF16), w_mem_v.astype(BF16))
    cmk = _mem_cache_view(cache_mem_k.astype(F32))
    cmv = _mem_cache_view(cache_mem_v.astype(F32))

    ssm_p_re, ssm_p_im, ssm_s_re, ssm_s_im = [], [], [], []
    ckv_p = kr_p = ckv_s = kr_s = None
    for i in range(depth):
        if i < n_a:
            abar_re, abar_im, bbar_re, bbar_im = _ssm_discretise(
                ssm_a_re[i], ssm_a_im[i], ssm_log_dt[i], ssm_b_re[i], ssm_b_im[i])
            ng = SSM_GROUPS // SSM_KBLOCKS
            bdre = _block_diag(bbar_re.reshape(SSM_KBLOCKS, ng, SSM_GROUP, SSM_STATE)).astype(BF16)
            bdim = _block_diag(bbar_im.reshape(SSM_KBLOCKS, ng, SSM_GROUP, SSM_STATE)).astype(BF16)
            c_t = lambda c: c.astype(F32).reshape(SSM_KBLOCKS, ng, SSM_GROUP, SSM_STATE).transpose(0, 1, 3, 2)
            cre = _block_diag(c_t(ssm_c_re[i])).astype(BF16)
            ncim = _block_diag(-c_t(ssm_c_im[i])).astype(BF16)
            s5_w = (_row(norm_mix_pre[i]), bdre, bdim, abar_re.reshape(1, SSM_WIDTH), abar_im.reshape(1, SSM_WIDTH))
            s5_w2 = (cre, ncim, _row(ssm_d[i]), w_glu[i].astype(BF16), _row(norm_mix_post[i]))
            zeros = jnp.zeros((bp, SSM_WIDTH), F32)
            hp3, fpr, fpi = _s5_layer(hp.reshape(bp, sp, D_MODEL), 32, bp, 40, *s5_w, zeros, zeros, *s5_w2,
                                      name="s5_prompt")
            hs3, fsr, fsi = _s5_layer(hs.reshape(bs, ts, D_MODEL), ts, 32, ts, *s5_w,
                                      cache_ssm_re[i].astype(F32).reshape(bs, SSM_WIDTH),
                                      cache_ssm_im[i].astype(F32).reshape(bs, SSM_WIDTH), *s5_w2, name="s5_sample")
            hp = hp3.reshape(bp * sp, D_MODEL)
            hs = hs3.reshape(bs * ts, D_MODEL)
            ssm_p_re.append(fpr.reshape(bp, SSM_GROUPS, SSM_STATE))
            ssm_p_im.append(fpi.reshape(bp, SSM_GROUPS, SSM_STATE))
            ssm_s_re.append(fsr.reshape(bs, SSM_GROUPS, SSM_STATE))
            ssm_s_im.append(fsi.reshape(bs, SSM_GROUPS, SSM_STATE))
        else:
            j = i - n_a
            wuq = w_uq[j].astype(F32)
            w_nope = wuq[:, :, :QK_NOPE].reshape(Q_LORA, MLA_HEADS * QK_NOPE)
            w_r = wuq[:, :, QK_NOPE:]
            w_rope = _pad_rope(w_r).reshape(Q_LORA, MLA_HEADS * ROPE_PAD)
            w_rope_rot = _pad_rope(_rot_cols(w_r)).reshape(Q_LORA, MLA_HEADS * ROPE_PAD)
            wuq_all = jnp.concatenate([w_nope, w_rope, w_rope_rot], axis=1).astype(BF16)
            ukt = w_uk.astype(F32).transpose(1, 2, 0).astype(BF16)
            wkr = w_kr.astype(F32)
            wkr2 = jnp.concatenate([_pad_rope(wkr), _pad_rope(_rot_cols(wkr))], axis=1).astype(BF16)
            pre_w = (_row(norm_mix_pre[i]), _row(kv_in_norm), w_dq[j].astype(BF16), _row(q_norm[j]), wuq_all, ukt,
                     w_dkv.astype(BF16), _row(kv_latent_norm), wkr2)
            cos_p, sin_p = _rope_tables(jnp.arange(sp, dtype=jnp.int32))
            cos_s, sin_s = _rope_tables(past_len + jnp.tile(jnp.arange(ts, dtype=jnp.int32), 512 // ts))
            qlat_p, qrope_p, kvcat_p, ckv_pj, kr_pj = _mla_pre(hp, cos_p, sin_p, *pre_w, BF16, name="mla_pre_prompt")
            qlat_s, qrope_s, _, ckv_sj, kr_sj = _mla_pre(hs, cos_s, sin_s, *pre_w, F32, name="mla_pre_sample")
            if i == n_a:
                ckv_p, kr_p, ckv_s, kr_s, kvcat_shared = ckv_pj, kr_pj, ckv_sj, kr_sj, kvcat_p
            o_p = _mla_attn_prompt(qlat_p, qrope_p, kvcat_shared, sp, bp)
            o_s = _mla_attn_sample(qlat_s, qrope_s, ckv_s, kr_s, cache_kv_latent.astype(F32),
                                   jnp.swapaxes(cache_k_rope.astype(F32), 1, 2), page_table, ts, bs)
            wuv = w_uv.astype(F32).transpose(1, 0, 2).astype(BF16)
            wo = w_o[j].astype(BF16)
            hp = _mla_out(o_p, wuv, wo, _row(norm_mix_post[i]), hp, name="mla_out_prompt")
            hs = _mla_out(o_s, wuv, wo, _row(norm_mix_post[i]), hs, name="mla_out_sample")

        wq = w_mem_q[i].astype(BF16)
        wo_mem = w_mem_o[i].astype(BF16)
        hp = _mem_attn_prompt(hp, sp, bp, i, _row(norm_mem_pre[i]), wq, mkb, mvb, wo_mem, _row(norm_mem_post[i]))
        q_s = _norm_matmul(hs, _row(norm_mem_pre[i]), wq, MEM_SCALE, F32, name="mem_q_sample")
        o_s = _mem_attn_sample(q_s, ts, bs, i, cmk, cmv)
        hs = _matmul_norm_res(o_s, wo_mem, _row(norm_mem_post[i]), hs, name="mem_out_sample")

        wup = w_mlp_up[i].astype(BF16)
        wdn = w_mlp_down[i].astype(BF16)
        hp = _mlp(hp, _row(norm_mlp_pre[i]), wup, wdn, _row(norm_mlp_post[i]), name="mlp_prompt")
        hs = _mlp(hs, _row(norm_mlp_pre[i]), wup, wdn, _row(norm_mlp_post[i]), name="mlp_sample")

    return (hp.reshape(bp, sp, D_MODEL), hs.reshape(bs, ts, D_MODEL),
            jnp.stack(ssm_p_re), jnp.stack(ssm_p_im), jnp.stack(ssm_s_re), jnp.stack(ssm_s_im),
            ckv_p.reshape(bp, sp, KV_LORA), kr_p[:, :QK_ROPE].reshape(bp, sp, QK_ROPE),
            ckv_s.reshape(bs, ts, KV_LORA), kr_s[:, :QK_ROPE].reshape(bs, ts, QK_ROPE),
            mk.reshape(depth, bp, MEM_TOKENS, MEM_HEADS, MEM_HEAD_DIM),
            mv.reshape(depth, bp, MEM_TOKENS, MEM_HEADS, MEM_HEAD_DIM))
```

```python
import functools
import math

import jax
import jax.numpy as jnp
from jax import lax
from jax.experimental import pallas as pl
from jax.experimental.pallas import tpu as pltpu

F32 = jnp.float32
BF16 = jnp.bfloat16

LANES = 128
SUBLANES = 8
D_MODEL = 1024
PAGE_SIZE = 128
SSM_GROUP = 16
SSM_GROUPS = D_MODEL // SSM_GROUP
SSM_STATE = 64
SSM_WIDTH = SSM_GROUPS * SSM_STATE
SSM_KBLOCKS = 4
SSM_SLABS = SSM_WIDTH // LANES
MLA_HEADS = 8
QK_NOPE = 128
QK_ROPE = 64
ROPE_PAD = LANES
V_HEAD = 128
KV_LORA = 256
Q_LORA = 384
ROPE_BASE = 10000.0
MLA_SCALE = (QK_NOPE + QK_ROPE) ** -0.5
MLA_Q_SCALE = MLA_SCALE * math.log2(math.e)
MEM_TOKENS = 256
MEM_HEADS = 4
MEM_HEAD_DIM = D_MODEL // MEM_HEADS
MEM_SCALE = MEM_HEAD_DIM ** -0.5
D_FF = 4 * D_MODEL
EPS = 1e-6
NEG_BIG = -1e30

VMEM_LIMIT_BYTES = 56 * 1024 * 1024


def _params(*semantics):
    return pltpu.CompilerParams(dimension_semantics=semantics, vmem_limit_bytes=VMEM_LIMIT_BYTES)


def _dot(a, b):
    return jnp.dot(a, b, preferred_element_type=F32)


def _dot_t(a, b):
    return lax.dot_general(a, b, (((1,), (1,)), ((), ())), preferred_element_type=F32)


def _rms(x, g):
    return x * lax.rsqrt(jnp.mean(x * x, axis=-1, keepdims=True) + EPS) * g


def _row(v):
    return v.reshape(1, -1).astype(F32)


def _lane_tile(x, width):
    return jnp.concatenate([x] * (width // LANES), axis=1) if width > LANES else x


def _ssm_disc_kernel(ar_ref, ai_ref, ldt_ref, br_ref, bi_ref,
                     abar_re_ref, abar_im_ref, bbar_re_ref, bbar_im_ref):
    ar = ar_ref[...]
    ai = ai_ref[...]
    dt = jnp.exp(ldt_ref[...])
    mag = jnp.exp(ar * dt)
    abar_re = mag * jnp.cos(ai * dt)
    abar_im = mag * jnp.sin(ai * dt)
    den = ar * ar + ai * ai
    nr = abar_re - 1.0
    ni = abar_im
    coef_re = (nr * ar + ni * ai) / den
    coef_im = (ni * ar - nr * ai) / den
    br = br_ref[...]
    bi = bi_ref[...]
    abar_re_ref[...] = abar_re
    abar_im_ref[...] = abar_im
    bbar_re_ref[...] = coef_re * br - coef_im * bi
    bbar_im_ref[...] = coef_re * bi + coef_im * br


def _ssm_discretise(a_re, a_im, log_dt, b_re, b_im):
    g, n, k = SSM_GROUPS, SSM_STATE, SSM_GROUP
    rep = lambda a: jnp.repeat(a.astype(F32), k, axis=0)
    brt = b_re.astype(F32).transpose(0, 2, 1).reshape(g * k, n)
    bit = b_im.astype(F32).transpose(0, 2, 1).reshape(g * k, n)
    ldt = jnp.repeat(log_dt.astype(F32), k, axis=0).reshape(g * k, 1)
    shp = jax.ShapeDtypeStruct((g * k, n), F32)
    abar_re, abar_im, bbar_re, bbar_im = pl.pallas_call(
        _ssm_disc_kernel, out_shape=(shp, shp, shp, shp), name="ssm_discretise",
    )(rep(a_re), rep(a_im), ldt, brt, bit)
    return abar_re[::k], abar_im[::k], bbar_re, bbar_im


def _block_diag(x):
    kb, ng, r, c = x.shape
    eye = jnp.eye(ng, dtype=x.dtype)
    return (x[:, :, :, None, :] * eye[None, :, None, :, None]).reshape(kb, ng * r, ng * c)


def _s5_kernel(x_ref, gpre_ref, bdre_ref, bdim_ref, are_ref, aim_ref, h0re_ref, h0im_ref,
               cre_ref, ncim_ref, dskip_ref, wglu_ref, gpost_ref,
               out_ref, fre_ref, fim_ref, xs_ref, bure_ref, buim_ref, rs_ref, *, tt, bb, xpitch, rpitch):
    ti = pl.program_id(1)
    rows = tt * bb
    nsl = D_MODEL // LANES
    spk = SSM_SLABS // SSM_KBLOCKS

    @pl.when(ti == 0)
    def _():
        fre_ref[...] = h0re_ref[...]
        fim_ref[...] = h0im_ref[...]

    for b in range(bb):
        for s in range(nsl):
            xs_ref[s, b * xpitch:b * xpitch + tt, :] = x_ref[b, :, s * LANES:(s + 1) * LANES]
    x = jnp.concatenate(
        [jnp.concatenate([xs_ref[s, pl.ds(t, bb, stride=xpitch), :] for s in range(nsl)], axis=1)
         for t in range(tt)], axis=0)
    xn = _rms(x, gpre_ref[...])
    xb = xn.astype(BF16)
    kw = D_MODEL // SSM_KBLOCKS
    sw = SSM_WIDTH // SSM_KBLOCKS

    def scan(slab):
        lanes = slice(slab * LANES, (slab + 1) * LANES)
        a_re = jnp.broadcast_to(are_ref[:, lanes], (SUBLANES, LANES))
        a_im = jnp.broadcast_to(aim_ref[:, lanes], (SUBLANES, LANES))
        for vb in range(bb // SUBLANES):
            b0 = vb * SUBLANES
            h_re = fre_ref[b0:b0 + SUBLANES, lanes]
            h_im = fim_ref[b0:b0 + SUBLANES, lanes]
            for t in range(tt):
                rws = slice(t * bb + b0, t * bb + b0 + SUBLANES)
                h_re, h_im = (a_re * h_re - a_im * h_im + bure_ref[rws, lanes],
                              a_re * h_im + a_im * h_re + buim_ref[rws, lanes])
                bure_ref[rws, lanes] = h_re
                buim_ref[rws, lanes] = h_im
            fre_ref[b0:b0 + SUBLANES, lanes] = h_re
            fim_ref[b0:b0 + SUBLANES, lanes] = h_im

    ys = []
    for kb in range(SSM_KBLOCKS):
        cols = slice(kb * sw, (kb + 1) * sw)
        xk = xb[:, kb * kw:(kb + 1) * kw]
        bure_ref[:, cols] = _dot(xk, bdre_ref[kb])
        buim_ref[:, cols] = _dot(xk, bdim_ref[kb])
        for s in range(spk):
            scan(kb * spk + s)
        ys.append(_dot(bure_ref[:, cols].astype(BF16), cre_ref[kb])
                  + _dot(buim_ref[:, cols].astype(BF16), ncim_ref[kb]))
    y = jnp.concatenate(ys, axis=1) + dskip_ref[...] * xn
    g = 0.5 * y * (1.0 + lax.erf(y * (2.0 ** -0.5)))
    z = _dot(g.astype(BF16), wglu_ref[...])
    o = z[:, :D_MODEL] * (1.0 / (1.0 + jnp.exp(-z[:, D_MODEL:])))
    r = _rms(o, gpost_ref[...])
    for s in range(nsl):
        for t in range(tt):
            rs_ref[s, t * rpitch:t * rpitch + bb, :] = r[t * bb:(t + 1) * bb, s * LANES:(s + 1) * LANES]
    for b in range(bb):
        rb = jnp.concatenate([rs_ref[s, pl.ds(b, tt, stride=rpitch), :] for s in range(nsl)], axis=1)
        out_ref[b] = x_ref[b] + rb


def _odd_pitch(n):
    groups = -(-n // SUBLANES)
    return SUBLANES * (groups if groups % 2 else groups + 1)


def _s5_layer(x3, tt, bb, gpre, bdre, bdim, a_re, a_im, h0_re, h0_im,
              cre, ncim, dskip, wglu, gpost, name):
    b_all, t_len, _ = x3.shape
    grid = (b_all // bb, t_len // tt)
    const2 = lambda bj, ti: (0, 0)
    const3 = lambda bj, ti: (0, 0, 0)
    xpitch, rpitch = _odd_pitch(tt), _odd_pitch(bb)
    nsl = D_MODEL // LANES
    bu = pltpu.VMEM((tt * bb, SSM_WIDTH), F32)
    return pl.pallas_call(
        functools.partial(_s5_kernel, tt=tt, bb=bb, xpitch=xpitch, rpitch=rpitch),
        grid=grid,
        in_specs=[
            pl.BlockSpec((bb, tt, D_MODEL), lambda bj, ti: (bj, ti, 0)),
            pl.BlockSpec((1, D_MODEL), const2),
            pl.BlockSpec(bdre.shape, const3),
            pl.BlockSpec(bdim.shape, const3),
            pl.BlockSpec((1, SSM_WIDTH), const2),
            pl.BlockSpec((1, SSM_WIDTH), const2),
            pl.BlockSpec((bb, SSM_WIDTH), lambda bj, ti: (bj, 0)),
            pl.BlockSpec((bb, SSM_WIDTH), lambda bj, ti: (bj, 0)),
            pl.BlockSpec(cre.shape, const3),
            pl.BlockSpec(ncim.shape, const3),
            pl.BlockSpec((1, D_MODEL), const2),
            pl.BlockSpec(wglu.shape, const2),
            pl.BlockSpec((1, D_MODEL), const2),
        ],
        out_specs=(
            pl.BlockSpec((bb, tt, D_MODEL), lambda bj, ti: (bj, ti, 0)),
            pl.BlockSpec((bb, SSM_WIDTH), lambda bj, ti: (bj, 0)),
            pl.BlockSpec((bb, SSM_WIDTH), lambda bj, ti: (bj, 0)),
        ),
        out_shape=(
            jax.ShapeDtypeStruct((b_all, t_len, D_MODEL), F32),
            jax.ShapeDtypeStruct((b_all, SSM_WIDTH), F32),
            jax.ShapeDtypeStruct((b_all, SSM_WIDTH), F32),
        ),
        scratch_shapes=[pltpu.VMEM((nsl, bb * xpitch, LANES), F32), bu, bu,
                        pltpu.VMEM((nsl, tt * rpitch, LANES), F32)],
        compiler_params=_params("parallel", "arbitrary"),
        name=name,
    )(x3, gpre, bdre, bdim, a_re, a_im, h0_re, h0_im, cre, ncim, dskip, wglu, gpost)


def _mem_kv_kernel(x_ref, g_ref, wk_ref, wv_ref, k_ref, v_ref, kb_ref, vb_ref):
    xn = _rms(x_ref[...], g_ref[0]).astype(BF16)
    k = _dot(xn, wk_ref[0])
    v = _dot(xn, wv_ref[0])
    k_ref[0] = k
    v_ref[0] = v
    kb_ref[0] = k.astype(BF16)
    vb_ref[0] = v.astype(BF16)


def _mem_kv(mem, g, wk, wv):
    rows = mem.shape[0]
    nl = g.shape[0]
    tm = 512
    blk = pl.BlockSpec((1, tm, D_MODEL), lambda l, i: (l, i, 0))
    wspec = pl.BlockSpec((1, D_MODEL, D_MODEL), lambda l, i: (l, 0, 0))
    return pl.pallas_call(
        _mem_kv_kernel,
        grid=(nl, rows // tm),
        in_specs=[pl.BlockSpec((tm, D_MODEL), lambda l, i: (i, 0)),
                  pl.BlockSpec((1, 1, D_MODEL), lambda l, i: (l, 0, 0)), wspec, wspec],
        out_specs=(blk, blk, blk, blk),
        out_shape=(jax.ShapeDtypeStruct((nl, rows, D_MODEL), F32),) * 2
        + (jax.ShapeDtypeStruct((nl, rows, D_MODEL), BF16),) * 2,
        compiler_params=_params("parallel", "parallel"),
        name="mem_kv",
    )(mem, g, wk, wv)


def _softmax_rows(s):
    m = jnp.max(s, axis=-1, keepdims=True)
    p = jnp.exp(s - m)
    return p / jnp.sum(p, axis=-1, keepdims=True)


def _row_halves(n):
    return (slice(0, n // 2), slice(n // 2, n))


def _mem_attn_prompt_kernel(h_ref, gpre_ref, wq_ref, k_ref, v_ref, wo_ref, gpost_ref, out_ref):
    for rws in _row_halves(h_ref.shape[0]):
        x = h_ref[rws, :]
        xn = _rms(x, gpre_ref[...]).astype(BF16)
        q = (_dot(xn, wq_ref[...]) * MEM_SCALE).astype(BF16)
        os = []
        for hh in range(MEM_HEADS):
            cols = slice(hh * MEM_HEAD_DIM, (hh + 1) * MEM_HEAD_DIM)
            p = _softmax_rows(_dot_t(q[:, cols], k_ref[0, :, cols]))
            os.append(_dot(p.astype(BF16), v_ref[0, :, cols]).astype(BF16))
        y = _dot(jnp.concatenate(os, axis=1), wo_ref[...])
        out_ref[rws, :] = x + _rms(y, gpost_ref[...])


def _mem_attn_prompt(h, t_len, b_all, layer, gpre, wq, kb, vb, wo, gpost):
    tq = 1024
    nq = t_len // tq
    const2 = lambda b, i: (0, 0)
    hspec = pl.BlockSpec((tq, D_MODEL), lambda b, i: (b * nq + i, 0))
    kvspec = pl.BlockSpec((1, MEM_TOKENS, D_MODEL), lambda b, i: (layer, b, 0))
    wspec = pl.BlockSpec((D_MODEL, D_MODEL), const2)
    gspec = pl.BlockSpec((1, D_MODEL), const2)
    return pl.pallas_call(
        _mem_attn_prompt_kernel,
        grid=(b_all, nq),
        in_specs=[hspec, gspec, wspec, kvspec, kvspec, wspec, gspec],
        out_specs=hspec,
        out_shape=jax.ShapeDtypeStruct(h.shape, F32),
        compiler_params=_params("parallel", "parallel"),
        name="mem_attn_prompt",
    )(h, gpre, wq, kb, vb, wo, gpost)


def _mem_cache_view(cache):
    nl, nb, m, nh, dh = cache.shape
    chunks = dh // LANES
    return cache.reshape(nl, nb, m, nh, chunks, LANES).transpose(0, 1, 2, 4, 3, 5).reshape(
        nl, nb, m * chunks * nh, LANES)


def _mem_attn_sample_kernel(q_ref, k_ref, v_ref, o_ref, *, nb, ts):
    chunks = MEM_HEAD_DIM // LANES
    period = chunks * MEM_HEADS

    def head(ref, g, hh):
        parts = [ref[0, g, pl.ds(c * MEM_HEADS + hh, MEM_TOKENS, stride=period), :] for c in range(chunks)]
        return jnp.concatenate(parts, axis=1).astype(BF16)

    pairs = [(g, hh) for g in range(nb) for hh in range(MEM_HEADS)]
    ss = [_dot_t(q_ref[g * ts:(g + 1) * ts, hh * MEM_HEAD_DIM:(hh + 1) * MEM_HEAD_DIM].astype(BF16),
                 head(k_ref, g, hh)) for g, hh in pairs]
    p = _softmax_rows(jnp.concatenate(ss, axis=0))
    for i, (g, hh) in enumerate(pairs):
        o_ref[g * ts:(g + 1) * ts, hh * MEM_HEAD_DIM:(hh + 1) * MEM_HEAD_DIM] = _dot(
            p[i * ts:(i + 1) * ts].astype(BF16), head(v_ref, g, hh))


def _mem_attn_sample(q, ts, b_all, layer, cache_k, cache_v):
    nb = 4
    qspec = pl.BlockSpec((nb * ts, D_MODEL), lambda j: (j, 0))
    cspec = pl.BlockSpec((1, nb) + cache_k.shape[2:], lambda j: (layer, j, 0, 0))
    return pl.pallas_call(
        functools.partial(_mem_attn_sample_kernel, nb=nb, ts=ts),
        grid=(b_all // nb,),
        in_specs=[qspec, cspec, cspec],
        out_specs=qspec,
        out_shape=jax.ShapeDtypeStruct(q.shape, F32),
        compiler_params=_params("parallel"),
        name="mem_attn_sample",
    )(q, cache_k, cache_v)


def _norm_matmul_kernel(x_ref, g_ref, w_ref, o_ref, *, scale):
    xn = _rms(x_ref[...], g_ref[...]).astype(BF16)
    o_ref[...] = (_dot(xn, w_ref[...]) * scale).astype(o_ref.dtype)


def _norm_matmul(x, g, w, scale, out_dtype, name):
    rows = x.shape[0]
    tm = 512
    n = w.shape[1]
    return pl.pallas_call(
        functools.partial(_norm_matmul_kernel, scale=scale),
        grid=(rows // tm,),
        in_specs=[pl.BlockSpec((tm, D_MODEL), lambda i: (i, 0)),
                  pl.BlockSpec((1, D_MODEL), lambda i: (0, 0)),
                  pl.BlockSpec(w.shape, lambda i: (0, 0))],
        out_specs=pl.BlockSpec((tm, n), lambda i: (i, 0)),
        out_shape=jax.ShapeDtypeStruct((rows, n), out_dtype),
        compiler_params=_params("parallel"),
        name=name,
    )(x, g, w)


def _matmul_norm_res_kernel(a_ref, w_ref, g_ref, h_ref, o_ref):
    y = _dot(a_ref[...].astype(BF16), w_ref[...])
    o_ref[...] = h_ref[...] + _rms(y, g_ref[...])


def _matmul_norm_res(a, w, g, h, name):
    rows, k = a.shape
    tm = 512
    return pl.pallas_call(
        _matmul_norm_res_kernel,
        grid=(rows // tm,),
        in_specs=[pl.BlockSpec((tm, k), lambda i: (i, 0)),
                  pl.BlockSpec(w.shape, lambda i: (0, 0)),
                  pl.BlockSpec((1, D_MODEL), lambda i: (0, 0)),
                  pl.BlockSpec((tm, D_MODEL), lambda i: (i, 0))],
        out_specs=pl.BlockSpec((tm, D_MODEL), lambda i: (i, 0)),
        out_shape=jax.ShapeDtypeStruct((rows, D_MODEL), F32),
        compiler_params=_params("parallel"),
        name=name,
    )(a, w, g, h)


def _mlp_kernel(x_ref, gpre_ref, wup_ref, wdn_ref, gpost_ref, o_ref, *, tf):
    x = x_ref[...]
    xn = _rms(x, gpre_ref[...]).astype(BF16)
    acc = None
    for c in range(D_FF // tf):
        hid = jnp.maximum(_dot(xn, wup_ref[:, c * tf:(c + 1) * tf]), 0.0)
        part = _dot((hid * hid).astype(BF16), wdn_ref[c * tf:(c + 1) * tf, :])
        acc = part if acc is None else acc + part
    o_ref[...] = x + _rms(acc, gpost_ref[...])


def _resident(shape):
    return pl.BlockSpec(shape, lambda *_: (0,) * len(shape), pipeline_mode=pl.Buffered(1))


def _mlp(h, gpre, wup, wdn, gpost, name):
    rows = h.shape[0]
    tm, tf = 512, 1024
    return pl.pallas_call(
        functools.partial(_mlp_kernel, tf=tf),
        grid=(rows // tm,),
        in_specs=[pl.BlockSpec((tm, D_MODEL), lambda i: (i, 0)),
                  _resident((1, D_MODEL)), _resident(wup.shape), _resident(wdn.shape), _resident((1, D_MODEL))],
        out_specs=pl.BlockSpec((tm, D_MODEL), lambda i: (i, 0)),
        out_shape=jax.ShapeDtypeStruct((rows, D_MODEL), F32),
        compiler_params=_params("parallel"),
        name=name,
    )(h, gpre, wup, wdn, gpost)


def _mla_pre_kernel(h_ref, cos_ref, sin_ref, gmix_ref, gkv_ref, wdq_ref, qn_ref, wuq_ref, ukt_ref,
                    wdkv_ref, glat_ref, wkr_ref,
                    qlat_ref, qrope_ref, kvcat_ref, ckv_ref, kr_ref):
    x = h_ref[...]
    y = x * lax.rsqrt(jnp.mean(x * x, axis=-1, keepdims=True) + EPS)
    cos = cos_ref[...]
    sin = sin_ref[...]
    hn = (y * gkv_ref[...]).astype(BF16)
    ckv = _rms(_dot(hn, wdkv_ref[...]), glat_ref[...])
    kr2 = _dot(hn, wkr_ref[...])
    kr = kr2[:, :ROPE_PAD] * cos + kr2[:, ROPE_PAD:] * sin
    ckv_ref[...] = ckv
    kr_ref[...] = kr
    kvcat_ref[...] = jnp.concatenate([ckv, kr], axis=1).astype(BF16)
    xn = (y * gmix_ref[...]).astype(BF16)
    cq = _rms(_dot(xn, wdq_ref[...]), qn_ref[...]).astype(BF16)
    q = _dot(cq, wuq_ref[...])
    nope_w = MLA_HEADS * QK_NOPE
    rope_w = MLA_HEADS * ROPE_PAD
    qr = (q[:, nope_w:nope_w + rope_w] * _lane_tile(cos, rope_w)
          + q[:, nope_w + rope_w:] * _lane_tile(sin, rope_w))
    qrope_ref[...] = (qr * MLA_Q_SCALE).astype(qrope_ref.dtype)
    for hh in range(MLA_HEADS):
        qn_h = q[:, hh * QK_NOPE:(hh + 1) * QK_NOPE].astype(BF16)
        ql = _dot(qn_h, ukt_ref[hh])
        qlat_ref[:, hh * KV_LORA:(hh + 1) * KV_LORA] = (ql * MLA_Q_SCALE).astype(qlat_ref.dtype)


def _mla_pre(h, cos, sin, gmix, gkv, wdq, qn, wuq, ukt, wdkv, glat, wkr, q_dtype, name):
    rows = h.shape[0]
    tm = 512
    period = cos.shape[0] // tm
    row = lambda w: pl.BlockSpec((tm, w), lambda i: (i, 0))
    tab = pl.BlockSpec((tm, ROPE_PAD), lambda i: (i % period, 0))
    full = lambda a: pl.BlockSpec(a.shape, lambda i: (0,) * a.ndim)
    return pl.pallas_call(
        _mla_pre_kernel,
        grid=(rows // tm,),
        in_specs=[row(D_MODEL), tab, tab, full(gmix), full(gkv), full(wdq), full(qn),
                  full(wuq), full(ukt), full(wdkv), full(glat), full(wkr)],
        out_specs=(row(MLA_HEADS * KV_LORA), row(MLA_HEADS * ROPE_PAD), row(KV_LORA + ROPE_PAD),
                   row(KV_LORA), row(ROPE_PAD)),
        out_shape=(jax.ShapeDtypeStruct((rows, MLA_HEADS * KV_LORA), q_dtype),
                   jax.ShapeDtypeStruct((rows, MLA_HEADS * ROPE_PAD), q_dtype),
                   jax.ShapeDtypeStruct((rows, KV_LORA + ROPE_PAD), BF16),
                   jax.ShapeDtypeStruct((rows, KV_LORA), F32),
                   jax.ShapeDtypeStruct((rows, ROPE_PAD), F32)),
        compiler_params=_params("parallel"),
        name=name,
    )(h, cos, sin, gmix, gkv, wdq, qn, wuq, ukt, wdkv, glat, wkr)


def _mla_attn_prompt_kernel(qlat_ref, qrope_ref, kv_ref, o_ref, m_ref, l_ref, acc_ref, *, tq):
    qi = pl.program_id(1)
    m_ref[...] = jnp.full_like(m_ref, NEG_BIG)
    l_ref[...] = jnp.zeros_like(l_ref)
    acc_ref[...] = jnp.zeros_like(acc_ref)

    def block(j, diagonal):
        k0 = pl.multiple_of(j * tq, tq)
        ck = kv_ref[pl.ds(k0, tq), :KV_LORA]
        kr = kv_ref[pl.ds(k0, tq), KV_LORA:]
        for hh in range(MLA_HEADS):
            s = (_dot_t(qlat_ref[:, hh * KV_LORA:(hh + 1) * KV_LORA], ck)
                 + _dot_t(qrope_ref[:, hh * ROPE_PAD:(hh + 1) * ROPE_PAD], kr))
            if diagonal:
                visible = (lax.broadcasted_iota(jnp.int32, (tq, tq), 1)
                           <= lax.broadcasted_iota(jnp.int32, (tq, tq), 0))
                s = jnp.where(visible, s, NEG_BIG)
            m_old = m_ref[hh]
            m_new = jnp.maximum(m_old, jnp.max(s, axis=-1, keepdims=True))
            alpha = jnp.exp2(m_old - m_new)
            p = jnp.exp2(s - _lane_tile(m_new, tq))
            l_ref[hh] = alpha * l_ref[hh] + jnp.sum(p, axis=-1, keepdims=True)
            acc_ref[hh] = _lane_tile(alpha, KV_LORA) * acc_ref[hh] + _dot(p.astype(BF16), ck)
            m_ref[hh] = m_new

    def body(j, _):
        block(j, False)
        return 0

    lax.fori_loop(0, qi, body, 0)
    block(qi, True)
    for hh in range(MLA_HEADS):
        o = acc_ref[hh] / _lane_tile(l_ref[hh], KV_LORA)
        o_ref[:, hh * KV_LORA:(hh + 1) * KV_LORA] = o.astype(BF16)


def _mla_attn_prompt(qlat, qrope, kvcat, t_len, b_all):
    tq = 512
    nq = t_len // tq
    lw, rw, kw = MLA_HEADS * KV_LORA, MLA_HEADS * ROPE_PAD, KV_LORA + ROPE_PAD
    return pl.pallas_call(
        functools.partial(_mla_attn_prompt_kernel, tq=tq),
        grid=(b_all, nq),
        in_specs=[pl.BlockSpec((tq, lw), lambda b, i: (b * nq + i, 0)),
                  pl.BlockSpec((tq, rw), lambda b, i: (b * nq + i, 0)),
                  pl.BlockSpec((t_len, kw), lambda b, i: (b, 0))],
        out_specs=pl.BlockSpec((tq, lw), lambda b, i: (b * nq + i, 0)),
        out_shape=jax.ShapeDtypeStruct((b_all * t_len, lw), BF16),
        scratch_shapes=[pltpu.VMEM((MLA_HEADS, tq, LANES), F32), pltpu.VMEM((MLA_HEADS, tq, LANES), F32),
                        pltpu.VMEM((MLA_HEADS, tq, KV_LORA), F32)],
        compiler_params=_params("parallel", "parallel"),
        name="mla_attn_prompt",
    )(qlat, qrope, kvcat)


def _stack_heads(ref, width):
    return jnp.concatenate([ref[:, hh * width:(hh + 1) * width] for hh in range(MLA_HEADS)], axis=0)


def _mla_attn_sample_kernel(pt_ref, qlat_ref, qrope_ref, ckvn_ref, krn_ref, kv_hbm, kr_hbm,
                            o_ref, kvbuf, krbuf, sem, s_ref, kvb_ref, *, ts, n_pages, parts):
    b = pl.program_id(0)
    slot = b % 2
    rows = MLA_HEADS * ts
    past = n_pages * PAGE_SIZE

    def page_copies(batch, sl, page_of):
        cps = []
        for p in range(n_pages):
            page = page_of(batch, p)
            cps.append(pltpu.make_async_copy(kv_hbm.at[page], kvbuf.at[sl, p], sem.at[0, sl]))
            cps.append(pltpu.make_async_copy(kr_hbm.at[page], krbuf.at[sl, p], sem.at[1, sl]))
        return cps

    table_page = lambda batch, p: pt_ref[batch * n_pages + p]
    last = pl.num_programs(0) - 1

    @pl.when(b == 0)
    def _():
        for cp in page_copies(b, slot, table_page):
            cp.start()

    for cp in page_copies(b, slot, lambda batch, p: 0):
        cp.wait()

    for cp in page_copies(jnp.minimum(b + 1, last), 1 - slot, table_page):
        cp.start()

    ql = _stack_heads(qlat_ref, KV_LORA).astype(BF16)
    qr = _stack_heads(qrope_ref, ROPE_PAD).astype(BF16)
    qr_cache = qr[:, :QK_ROPE]
    pair = 2 * PAGE_SIZE
    pairs_per_part = n_pages // 2 // parts
    span = pairs_per_part * pair
    ms, ls, os = [], [], []
    for part in range(parts):
        for g in range(part * pairs_per_part, (part + 1) * pairs_per_part):
            ck = kvbuf[slot, 2 * g:2 * g + 2].reshape(pair, KV_LORA).astype(BF16)
            kr = jnp.concatenate([krbuf[slot, 2 * g], krbuf[slot, 2 * g + 1]], axis=1).astype(BF16)
            s_ref[:, g * pair:(g + 1) * pair] = _dot_t(ql, ck) + _dot(qr_cache, kr)
            kvb_ref[g * pair:(g + 1) * pair, :] = ck
        hi = (part + 1) * span
        if part == parts - 1:
            pad = PAGE_SIZE - ts
            ckn = jnp.concatenate([ckvn_ref[...], jnp.zeros((pad, KV_LORA), F32)], axis=0).astype(BF16)
            krn = jnp.concatenate([krn_ref[...], jnp.zeros((pad, ROPE_PAD), F32)], axis=0).astype(BF16)
            s_new = _dot_t(ql, ckn) + _dot_t(qr, krn)
            t_q = lax.broadcasted_iota(jnp.int32, (rows, PAGE_SIZE), 0) & (ts - 1)
            t_k = lax.broadcasted_iota(jnp.int32, (rows, PAGE_SIZE), 1)
            s_ref[:, past:] = jnp.where(t_k <= t_q, s_new, NEG_BIG)
            kvb_ref[past:, :] = ckn
            hi = past + PAGE_SIZE
        s = s_ref[:, part * span:hi]
        m = jnp.max(s, axis=-1, keepdims=True)
        p = jnp.exp2(s - m)
        ms.append(m)
        ls.append(jnp.sum(p, axis=-1, keepdims=True))
        os.append(_dot(p.astype(BF16), kvb_ref[part * span:hi, :]))
    m_all = functools.reduce(jnp.maximum, ms)
    ws = [jnp.exp2(m - m_all) for m in ms]
    o = sum(w * o_part for w, o_part in zip(ws, os)) / sum(w * l for w, l in zip(ws, ls))
    for hh in range(MLA_HEADS):
        o_ref[:, hh * KV_LORA:(hh + 1) * KV_LORA] = o[hh * ts:(hh + 1) * ts]

    @pl.when(b == last)
    def _():
        for cp in page_copies(b, 1 - slot, lambda batch, p: 0):
            cp.wait()


def _mla_attn_sample(qlat, qrope, ckv_new, kr_new, cache_kv, cache_kr_t, page_table, ts, b_all):
    n_pages = page_table.shape[1]
    lw, rw = MLA_HEADS * KV_LORA, MLA_HEADS * ROPE_PAD
    rows = MLA_HEADS * ts
    keys = (n_pages + 1) * PAGE_SIZE
    pt = page_table.reshape(-1)
    stream = lambda w: pl.BlockSpec((ts, w), lambda b, pt_ref: (b, 0))
    hbm = pl.BlockSpec(memory_space=pl.ANY)
    grid_spec = pltpu.PrefetchScalarGridSpec(
        num_scalar_prefetch=1,
        grid=(b_all,),
        in_specs=[stream(lw), stream(rw), stream(KV_LORA), stream(ROPE_PAD), hbm, hbm],
        out_specs=stream(lw),
        scratch_shapes=[pltpu.VMEM((2, n_pages, PAGE_SIZE, KV_LORA), F32),
                        pltpu.VMEM((2, n_pages, QK_ROPE, PAGE_SIZE), F32),
                        pltpu.SemaphoreType.DMA((2, 2)),
                        pltpu.VMEM((rows, keys), F32),
                        pltpu.VMEM((keys, KV_LORA), BF16)],
    )
    return pl.pallas_call(
        functools.partial(_mla_attn_sample_kernel, ts=ts, n_pages=n_pages, parts=4),
        grid_spec=grid_spec,
        out_shape=jax.ShapeDtypeStruct((b_all * ts, lw), F32),
        compiler_params=_params("arbitrary"),
        name="mla_attn_sample",
    )(pt, qlat, qrope, ckv_new, kr_new, cache_kv, cache_kr_t)


def _mla_out_kernel(o_ref, wuv_ref, wo_ref, g_ref, h_ref, out_ref):
    for rws in _row_halves(h_ref.shape[0]):
        vs = []
        for hh in range(MLA_HEADS):
            vs.append(_dot(o_ref[rws, hh * KV_LORA:(hh + 1) * KV_LORA].astype(BF16), wuv_ref[hh]).astype(BF16))
        y = _dot(jnp.concatenate(vs, axis=1), wo_ref[...])
        out_ref[rws, :] = h_ref[rws, :] + _rms(y, g_ref[...])


def _mla_out(o, wuv, wo, g, h, name):
    rows = h.shape[0]
    tm = 1024
    lw = MLA_HEADS * KV_LORA
    return pl.pallas_call(
        _mla_out_kernel,
        grid=(rows // tm,),
        in_specs=[pl.BlockSpec((tm, lw), lambda i: (i, 0)),
                  pl.BlockSpec(wuv.shape, lambda i: (0, 0, 0)),
                  pl.BlockSpec(wo.shape, lambda i: (0, 0)),
                  pl.BlockSpec((1, D_MODEL), lambda i: (0, 0)),
                  pl.BlockSpec((tm, D_MODEL), lambda i: (i, 0))],
        out_specs=pl.BlockSpec((tm, D_MODEL), lambda i: (i, 0)),
        out_shape=jax.ShapeDtypeStruct((rows, D_MODEL), F32),
        compiler_params=_params("parallel"),
        name=name,
    )(o, wuv, wo, g, h)


def _rope_tables(pos):
    half = QK_ROPE // 2
    inv = ROPE_BASE ** (-jnp.arange(half, dtype=F32) / half)
    ang = pos.astype(F32)[:, None] * inv[None, :]
    cos = jnp.cos(ang)
    sin = jnp.sin(ang)
    zeros = jnp.zeros((pos.shape[0], ROPE_PAD - QK_ROPE), F32)
    return jnp.concatenate([cos, cos, zeros], axis=1), jnp.concatenate([sin, sin, zeros], axis=1)


def _rot_cols(w):
    half = QK_ROPE // 2
    return jnp.concatenate([-w[..., half:], w[..., :half]], axis=-1)


def _pad_rope(w):
    pad = [(0, 0)] * (w.ndim - 1) + [(0, ROPE_PAD - QK_ROPE)]
    return jnp.pad(w, pad)


def kernel(x_prompt, x_sample, cache_ssm_re, cache_ssm_im, cache_kv_latent, cache_k_rope, cache_mem_k, cache_mem_v, page_table, mem_prompt, norm_mix_pre, norm_mix_post, norm_mem_pre, norm_mem_post, norm_mlp_pre, norm_mlp_post, mem_in_norm, w_mem_q, w_mem_k, w_mem_v, w_mem_o, w_mlp_up, w_mlp_down, ssm_a_re, ssm_a_im, ssm_log_dt, ssm_b_re, ssm_b_im, ssm_c_re, ssm_c_im, ssm_d, w_glu, kv_in_norm, w_dkv, kv_latent_norm, w_kr, w_uk, w_uv, w_dq, q_norm, w_uq, w_o):
    depth = norm_mix_pre.shape[0]
    n_a = ssm_a_re.shape[0]
    bp, sp, _ = x_prompt.shape
    bs, ts, _ = x_sample.shape
    past_len = page_table.shape[1] * PAGE_SIZE

    hp = x_prompt.astype(F32).reshape(bp * sp, D_MODEL)
    hs = x_sample.astype(F32).reshape(bs * ts, D_MODEL)

    mem_rows = mem_prompt.astype(F32).reshape(bp * MEM_TOKENS, D_MODEL)
    mk, mv, mkb, mvb = _mem_kv(mem_rows, mem_in_norm.astype(F32).reshape(depth, 1, D_MODEL),
                               w_mem_k.astype(BF16), w_mem_v.astype(BF16))
    cmk = _mem_cache_view(cache_mem_k.astype(F32))
    cmv = _mem_cache_view(cache_mem_v.astype(F32))

    ssm_p_re, ssm_p_im, ssm_s_re, ssm_s_im = [], [], [], []
    ckv_p = kr_p = ckv_s = kr_s = None
    for i in range(depth):
        if i < n_a:
            abar_re, abar_im, bbar_re, bbar_im = _ssm_discretise(
                ssm_a_re[i], ssm_a_im[i], ssm_log_dt[i], ssm_b_re[i], ssm_b_im[i])
            ng = SSM_GROUPS // SSM_KBLOCKS
            bdre = _block_diag(bbar_re.reshape(SSM_KBLOCKS, ng, SSM_GROUP, SSM_STATE)).astype(BF16)
            bdim = _block_diag(bbar_im.reshape(SSM_KBLOCKS, ng, SSM_GROUP, SSM_STATE)).astype(BF16)
            c_t = lambda c: c.astype(F32).reshape(SSM_KBLOCKS, ng, SSM_GROUP, SSM_STATE).transpose(0, 1, 3, 2)
            cre = _block_diag(c_t(ssm_c_re[i])).astype(BF16)
            ncim = _block_diag(-c_t(ssm_c_im[i])).astype(BF16)
            s5_w = (_row(norm_mix_pre[i]), bdre, bdim, abar_re.reshape(1, SSM_WIDTH), abar_im.reshape(1, SSM_WIDTH))
            s5_w2 = (cre, ncim, _row(ssm_d[i]), w_glu[i].astype(BF16), _row(norm_mix_post[i]))
            zeros = jnp.zeros((bp, SSM_WIDTH), F32)
            hp3, fpr, fpi = _s5_layer(hp.reshape(bp, sp, D_MODEL), 32, bp, *s5_w, zeros, zeros, *s5_w2,
                                      name="s5_prompt")
            hs3, fsr, fsi = _s5_layer(hs.reshape(bs, ts, D_MODEL), ts, 32, *s5_w,
                                      cache_ssm_re[i].astype(F32).reshape(bs, SSM_WIDTH),
                                      cache_ssm_im[i].astype(F32).reshape(bs, SSM_WIDTH), *s5_w2, name="s5_sample")
            hp = hp3.reshape(bp * sp, D_MODEL)
            hs = hs3.reshape(bs * ts, D_MODEL)
            ssm_p_re.append(fpr.reshape(bp, SSM_GROUPS, SSM_STATE))
            ssm_p_im.append(fpi.reshape(bp, SSM_GROUPS, SSM_STATE))
            ssm_s_re.append(fsr.reshape(bs, SSM_GROUPS, SSM_STATE))
            ssm_s_im.append(fsi.reshape(bs, SSM_GROUPS, SSM_STATE))
        else:
            j = i - n_a
            wuq = w_uq[j].astype(F32)
            w_nope = wuq[:, :, :QK_NOPE].reshape(Q_LORA, MLA_HEADS * QK_NOPE)
            w_r = wuq[:, :, QK_NOPE:]
            w_rope = _pad_rope(w_r).reshape(Q_LORA, MLA_HEADS * ROPE_PAD)
            w_rope_rot = _pad_rope(_rot_cols(w_r)).reshape(Q_LORA, MLA_HEADS * ROPE_PAD)
            wuq_all = jnp.concatenate([w_nope, w_rope, w_rope_rot], axis=1).astype(BF16)
            ukt = w_uk.astype(F32).transpose(1, 2, 0).astype(BF16)
            wkr = w_kr.astype(F32)
            wkr2 = jnp.concatenate([_pad_rope(wkr), _pad_rope(_rot_cols(wkr))], axis=1).astype(BF16)
            pre_w = (_row(norm_mix_pre[i]), _row(kv_in_norm), w_dq[j].astype(BF16), _row(q_norm[j]), wuq_all, ukt,
                     w_dkv.astype(BF16), _row(kv_latent_norm), wkr2)
            cos_p, sin_p = _rope_tables(jnp.arange(sp, dtype=jnp.int32))
            cos_s, sin_s = _rope_tables(past_len + jnp.tile(jnp.arange(ts, dtype=jnp.int32), 512 // ts))
            qlat_p, qrope_p, kvcat_p, ckv_pj, kr_pj = _mla_pre(hp, cos_p, sin_p, *pre_w, BF16, name="mla_pre_prompt")
            qlat_s, qrope_s, _, ckv_sj, kr_sj = _mla_pre(hs, cos_s, sin_s, *pre_w, F32, name="mla_pre_sample")
            if i == n_a:
                ckv_p, kr_p, ckv_s, kr_s, kvcat_shared = ckv_pj, kr_pj, ckv_sj, kr_sj, kvcat_p
            o_p = _mla_attn_prompt(qlat_p, qrope_p, kvcat_shared, sp, bp)
            o_s = _mla_attn_sample(qlat_s, qrope_s, ckv_s, kr_s, cache_kv_latent.astype(F32),
                                   jnp.swapaxes(cache_k_rope.astype(F32), 1, 2), page_table, ts, bs)
            wuv = w_uv.astype(F32).transpose(1, 0, 2).astype(BF16)
            wo = w_o[j].astype(BF16)
            hp = _mla_out(o_p, wuv, wo, _row(norm_mix_post[i]), hp, name="mla_out_prompt")
            hs = _mla_out(o_s, wuv, wo, _row(norm_mix_post[i]), hs, name="mla_out_sample")

        wq = w_mem_q[i].astype(BF16)
        wo_mem = w_mem_o[i].astype(BF16)
        hp = _mem_attn_prompt(hp, sp, bp, i, _row(norm_mem_pre[i]), wq, mkb, mvb, wo_mem, _row(norm_mem_post[i]))
        q_s = _norm_matmul(hs, _row(norm_mem_pre[i]), wq, MEM_SCALE, F32, name="mem_q_sample")
        o_s = _mem_attn_sample(q_s, ts, bs, i, cmk, cmv)
        hs = _matmul_norm_res(o_s, wo_mem, _row(norm_mem_post[i]), hs, name="mem_out_sample")

        wup = w_mlp_up[i].astype(BF16)
        wdn = w_mlp_down[i].astype(BF16)
        hp = _mlp(hp, _row(norm_mlp_pre[i]), wup, wdn, _row(norm_mlp_post[i]), name="mlp_prompt")
        hs = _mlp(hs, _row(norm_mlp_pre[i]), wup, wdn, _row(norm_mlp_post[i]), name="mlp_sample")

    return (hp.reshape(bp, sp, D_MODEL), hs.reshape(bs, ts, D_MODEL),
            jnp.stack(ssm_p_re), jnp.stack(ssm_p_im), jnp.stack(ssm_s_re), jnp.stack(ssm_s_im),
            ckv_p.reshape(bp, sp, KV_LORA), kr_p[:, :QK_ROPE].reshape(bp, sp, QK_ROPE),
            ckv_s.reshape(bs, ts, KV_LORA), kr_s[:, :QK_ROPE].reshape(bs, ts, QK_ROPE),
            mk.reshape(depth, bp, MEM_TOKENS, MEM_HEADS, MEM_HEAD_DIM),
            mv.reshape(depth, bp, MEM_TOKENS, MEM_HEADS, MEM_HEAD_DIM))
```

```python
import functools
import math

import jax
import jax.numpy as jnp
from jax import lax
from jax.experimental import pallas as pl
from jax.experimental.pallas import tpu as pltpu

F32 = jnp.float32
BF16 = jnp.bfloat16

LANES = 128
SUBLANES = 8
D_MODEL = 1024
PAGE_SIZE = 128
SSM_GROUP = 16
SSM_GROUPS = D_MODEL // SSM_GROUP
SSM_STATE = 64
SSM_WIDTH = SSM_GROUPS * SSM_STATE
SSM_KBLOCKS = 4
SSM_SLABS = SSM_WIDTH // LANES
MLA_HEADS = 8
QK_NOPE = 128
QK_ROPE = 64
ROPE_PAD = LANES
V_HEAD = 128
KV_LORA = 256
Q_LORA = 384
ROPE_BASE = 10000.0
MLA_SCALE = (QK_NOPE + QK_ROPE) ** -0.5
MLA_Q_SCALE = MLA_SCALE * math.log2(math.e)
MEM_TOKENS = 256
MEM_HEADS = 4
MEM_HEAD_DIM = D_MODEL // MEM_HEADS
MEM_SCALE = MEM_HEAD_DIM ** -0.5
D_FF = 4 * D_MODEL
EPS = 1e-6
NEG_BIG = -1e30

VMEM_LIMIT_BYTES = 56 * 1024 * 1024


def _params(*semantics):
    return pltpu.CompilerParams(dimension_semantics=semantics, vmem_limit_bytes=VMEM_LIMIT_BYTES)


def _dot(a, b):
    return jnp.dot(a, b, preferred_element_type=F32)


def _dot_t(a, b):
    return lax.dot_general(a, b, (((1,), (1,)), ((), ())), preferred_element_type=F32)


def _rms(x, g):
    return x * lax.rsqrt(jnp.mean(x * x, axis=-1, keepdims=True) + EPS) * g


def _row(v):
    return v.reshape(1, -1).astype(F32)


def _lane_tile(x, width):
    return jnp.concatenate([x] * (width // LANES), axis=1) if width > LANES else x


def _ssm_disc_kernel(ar_ref, ai_ref, ldt_ref, br_ref, bi_ref,
                     abar_re_ref, abar_im_ref, bbar_re_ref, bbar_im_ref):
    ar = ar_ref[...]
    ai = ai_ref[...]
    dt = jnp.exp(ldt_ref[...])
    mag = jnp.exp(ar * dt)
    abar_re = mag * jnp.cos(ai * dt)
    abar_im = mag * jnp.sin(ai * dt)
    den = ar * ar + ai * ai
    nr = abar_re - 1.0
    ni = abar_im
    coef_re = (nr * ar + ni * ai) / den
    coef_im = (ni * ar - nr * ai) / den
    br = br_ref[...]
    bi = bi_ref[...]
    abar_re_ref[...] = abar_re
    abar_im_ref[...] = abar_im
    bbar_re_ref[...] = coef_re * br - coef_im * bi
    bbar_im_ref[...] = coef_re * bi + coef_im * br


def _ssm_discretise(a_re, a_im, log_dt, b_re, b_im):
    g, n, k = SSM_GROUPS, SSM_STATE, SSM_GROUP
    rep = lambda a: jnp.repeat(a.astype(F32), k, axis=0)
    brt = b_re.astype(F32).transpose(0, 2, 1).reshape(g * k, n)
    bit = b_im.astype(F32).transpose(0, 2, 1).reshape(g * k, n)
    ldt = jnp.repeat(log_dt.astype(F32), k, axis=0).reshape(g * k, 1)
    shp = jax.ShapeDtypeStruct((g * k, n), F32)
    abar_re, abar_im, bbar_re, bbar_im = pl.pallas_call(
        _ssm_disc_kernel, out_shape=(shp, shp, shp, shp), name="ssm_discretise",
    )(rep(a_re), rep(a_im), ldt, brt, bit)
    return abar_re[::k], abar_im[::k], bbar_re, bbar_im


def _block_diag(x):
    kb, ng, r, c = x.shape
    eye = jnp.eye(ng, dtype=x.dtype)
    return (x[:, :, :, None, :] * eye[None, :, None, :, None]).reshape(kb, ng * r, ng * c)


def _s5_kernel(x_ref, gpre_ref, bdre_ref, bdim_ref, are_ref, aim_ref, h0re_ref, h0im_ref,
               cre_ref, ncim_ref, dskip_ref, wglu_ref, gpost_ref,
               out_ref, fre_ref, fim_ref, xs_ref, bure_ref, buim_ref, rs_ref, *, tt, bb, xpitch, rpitch):
    ti = pl.program_id(1)
    rows = tt * bb
    nsl = D_MODEL // LANES
    spk = SSM_SLABS // SSM_KBLOCKS

    @pl.when(ti == 0)
    def _():
        fre_ref[...] = h0re_ref[...]
        fim_ref[...] = h0im_ref[...]

    for b in range(bb):
        for s in range(nsl):
            xs_ref[s, b * xpitch:b * xpitch + tt, :] = x_ref[b, :, s * LANES:(s + 1) * LANES]
    x = jnp.concatenate(
        [jnp.concatenate([xs_ref[s, pl.ds(t, bb, stride=xpitch), :] for s in range(nsl)], axis=1)
         for t in range(tt)], axis=0)
    xn = _rms(x, gpre_ref[...])
    xb = xn.astype(BF16)
    kw = D_MODEL // SSM_KBLOCKS
    sw = SSM_WIDTH // SSM_KBLOCKS

    def scan(slab):
        lanes = slice(slab * LANES, (slab + 1) * LANES)
        a_re = jnp.broadcast_to(are_ref[:, lanes], (SUBLANES, LANES))
        a_im = jnp.broadcast_to(aim_ref[:, lanes], (SUBLANES, LANES))
        for vb in range(bb // SUBLANES):
            b0 = vb * SUBLANES
            h_re = fre_ref[b0:b0 + SUBLANES, lanes]
            h_im = fim_ref[b0:b0 + SUBLANES, lanes]
            for t in range(tt):
                rws = slice(t * bb + b0, t * bb + b0 + SUBLANES)
                h_re, h_im = (a_re * h_re - a_im * h_im + bure_ref[rws, lanes],
                              a_re * h_im + a_im * h_re + buim_ref[rws, lanes])
                bure_ref[rws, lanes] = h_re
                buim_ref[rws, lanes] = h_im
            fre_ref[b0:b0 + SUBLANES, lanes] = h_re
            fim_ref[b0:b0 + SUBLANES, lanes] = h_im

    ys = []
    for kb in range(SSM_KBLOCKS):
        cols = slice(kb * sw, (kb + 1) * sw)
        xk = xb[:, kb * kw:(kb + 1) * kw]
        bure_ref[:, cols] = _dot(xk, bdre_ref[kb])
        buim_ref[:, cols] = _dot(xk, bdim_ref[kb])
        for s in range(spk):
            scan(kb * spk + s)
        ys.append(_dot(bure_ref[:, cols].astype(BF16), cre_ref[kb])
                  + _dot(buim_ref[:, cols].astype(BF16), ncim_ref[kb]))
    y = jnp.concatenate(ys, axis=1) + dskip_ref[...] * xn
    g = 0.5 * y * (1.0 + lax.erf(y * (2.0 ** -0.5)))
    z = _dot(g.astype(BF16), wglu_ref[...])
    o = z[:, :D_MODEL] * (1.0 / (1.0 + jnp.exp(-z[:, D_MODEL:])))
    r = _rms(o, gpost_ref[...])
    for s in range(nsl):
        for t in range(tt):
            rs_ref[s, t * rpitch:t * rpitch + bb, :] = r[t * bb:(t + 1) * bb, s * LANES:(s + 1) * LANES]
    for b in range(bb):
        rb = jnp.concatenate([rs_ref[s, pl.ds(b, tt, stride=rpitch), :] for s in range(nsl)], axis=1)
        out_ref[b] = x_ref[b] + rb


def _odd_pitch(n):
    groups = -(-n // SUBLANES)
    return SUBLANES * (groups if groups % 2 else groups + 1)


def _s5_layer(x3, tt, bb, gpre, bdre, bdim, a_re, a_im, h0_re, h0_im,
              cre, ncim, dskip, wglu, gpost, name):
    b_all, t_len, _ = x3.shape
    grid = (b_all // bb, t_len // tt)
    const2 = lambda bj, ti: (0, 0)
    const3 = lambda bj, ti: (0, 0, 0)
    xpitch, rpitch = _odd_pitch(tt), _odd_pitch(bb)
    nsl = D_MODEL // LANES
    bu = pltpu.VMEM((tt * bb, SSM_WIDTH), F32)
    return pl.pallas_call(
        functools.partial(_s5_kernel, tt=tt, bb=bb, xpitch=xpitch, rpitch=rpitch),
        grid=grid,
        in_specs=[
            pl.BlockSpec((bb, tt, D_MODEL), lambda bj, ti: (bj, ti, 0)),
            pl.BlockSpec((1, D_MODEL), const2),
            pl.BlockSpec(bdre.shape, const3),
            pl.BlockSpec(bdim.shape, const3),
            pl.BlockSpec((1, SSM_WIDTH), const2),
            pl.BlockSpec((1, SSM_WIDTH), const2),
            pl.BlockSpec((bb, SSM_WIDTH), lambda bj, ti: (bj, 0)),
            pl.BlockSpec((bb, SSM_WIDTH), lambda bj, ti: (bj, 0)),
            pl.BlockSpec(cre.shape, const3),
            pl.BlockSpec(ncim.shape, const3),
            pl.BlockSpec((1, D_MODEL), const2),
            pl.BlockSpec(wglu.shape, const2),
            pl.BlockSpec((1, D_MODEL), const2),
        ],
        out_specs=(
            pl.BlockSpec((bb, tt, D_MODEL), lambda bj, ti: (bj, ti, 0)),
            pl.BlockSpec((bb, SSM_WIDTH), lambda bj, ti: (bj, 0)),
            pl.BlockSpec((bb, SSM_WIDTH), lambda bj, ti: (bj, 0)),
        ),
        out_shape=(
            jax.ShapeDtypeStruct((b_all, t_len, D_MODEL), F32),
            jax.ShapeDtypeStruct((b_all, SSM_WIDTH), F32),
            jax.ShapeDtypeStruct((b_all, SSM_WIDTH), F32),
        ),
        scratch_shapes=[pltpu.VMEM((nsl, bb * xpitch, LANES), F32), bu, bu,
                        pltpu.VMEM((nsl, tt * rpitch, LANES), F32)],
        compiler_params=_params("parallel", "arbitrary"),
        name=name,
    )(x3, gpre, bdre, bdim, a_re, a_im, h0_re, h0_im, cre, ncim, dskip, wglu, gpost)


def _mem_kv_kernel(x_ref, g_ref, wk_ref, wv_ref, k_ref, v_ref, kb_ref, vb_ref):
    xn = _rms(x_ref[...], g_ref[0]).astype(BF16)
    k = _dot(xn, wk_ref[0])
    v = _dot(xn, wv_ref[0])
    tm = k.shape[0]
    chunks = MEM_HEAD_DIM // LANES
    period = chunks * MEM_HEADS
    for hh in range(MEM_HEADS):
        for c in range(chunks):
            lanes = slice(hh * MEM_HEAD_DIM + c * LANES, hh * MEM_HEAD_DIM + (c + 1) * LANES)
            k_ref[0, pl.ds(c * MEM_HEADS + hh, tm, stride=period), :] = k[:, lanes]
            v_ref[0, pl.ds(c * MEM_HEADS + hh, tm, stride=period), :] = v[:, lanes]
    kb_ref[0] = k.astype(BF16)
    vb_ref[0] = v.astype(BF16)


def _mem_kv(mem, g, wk, wv):
    rows = mem.shape[0]
    nl = g.shape[0]
    tm = 512
    period = D_MODEL // LANES
    blk = pl.BlockSpec((1, tm, D_MODEL), lambda l, i: (l, i, 0))
    nat = pl.BlockSpec((1, tm * period, LANES), lambda l, i: (l, i, 0))
    wspec = pl.BlockSpec((1, D_MODEL, D_MODEL), lambda l, i: (l, 0, 0))
    return pl.pallas_call(
        _mem_kv_kernel,
        grid=(nl, rows // tm),
        in_specs=[pl.BlockSpec((tm, D_MODEL), lambda l, i: (i, 0)),
                  pl.BlockSpec((1, 1, D_MODEL), lambda l, i: (l, 0, 0)), wspec, wspec],
        out_specs=(nat, nat, blk, blk),
        out_shape=(jax.ShapeDtypeStruct((nl, rows * period, LANES), F32),) * 2
        + (jax.ShapeDtypeStruct((nl, rows, D_MODEL), BF16),) * 2,
        compiler_params=_params("parallel", "parallel"),
        name="mem_kv",
    )(mem, g, wk, wv)


def _softmax_rows(s):
    m = jnp.max(s, axis=-1, keepdims=True)
    p = jnp.exp(s - m)
    return p / jnp.sum(p, axis=-1, keepdims=True)


def _row_halves(n):
    return (slice(0, n // 2), slice(n // 2, n))


def _mem_attn_prompt_kernel(h_ref, gpre_ref, wq_ref, k_ref, v_ref, wo_ref, gpost_ref, out_ref):
    for rws in _row_halves(h_ref.shape[0]):
        x = h_ref[rws, :]
        xn = _rms(x, gpre_ref[...]).astype(BF16)
        q = (_dot(xn, wq_ref[...]) * MEM_SCALE).astype(BF16)
        os = []
        for hh in range(MEM_HEADS):
            cols = slice(hh * MEM_HEAD_DIM, (hh + 1) * MEM_HEAD_DIM)
            p = _softmax_rows(_dot_t(q[:, cols], k_ref[0, :, cols]))
            os.append(_dot(p.astype(BF16), v_ref[0, :, cols]).astype(BF16))
        y = _dot(jnp.concatenate(os, axis=1), wo_ref[...])
        out_ref[rws, :] = x + _rms(y, gpost_ref[...])


def _mem_attn_prompt(h, t_len, b_all, layer, gpre, wq, kb, vb, wo, gpost):
    tq = 1024
    nq = t_len // tq
    const2 = lambda b, i: (0, 0)
    hspec = pl.BlockSpec((tq, D_MODEL), lambda b, i: (b * nq + i, 0))
    kvspec = pl.BlockSpec((1, MEM_TOKENS, D_MODEL), lambda b, i: (layer, b, 0))
    wspec = pl.BlockSpec((D_MODEL, D_MODEL), const2)
    gspec = pl.BlockSpec((1, D_MODEL), const2)
    return pl.pallas_call(
        _mem_attn_prompt_kernel,
        grid=(b_all, nq),
        in_specs=[hspec, gspec, wspec, kvspec, kvspec, wspec, gspec],
        out_specs=hspec,
        out_shape=jax.ShapeDtypeStruct(h.shape, F32),
        compiler_params=_params("parallel", "parallel"),
        name="mem_attn_prompt",
    )(h, gpre, wq, kb, vb, wo, gpost)


def _mem_cache_view(cache):
    nl, nb, m, nh, dh = cache.shape
    chunks = dh // LANES
    return cache.reshape(nl, nb, m, nh, chunks, LANES).transpose(0, 1, 2, 4, 3, 5).reshape(
        nl, nb, m * chunks * nh, LANES)


def _mem_cache_unview(x, nb):
    nl = x.shape[0]
    chunks = MEM_HEAD_DIM // LANES
    return x.reshape(nl, nb, MEM_TOKENS, chunks, MEM_HEADS, LANES).transpose(0, 1, 2, 4, 3, 5).reshape(
        nl, nb, MEM_TOKENS, MEM_HEADS, MEM_HEAD_DIM)


def _mem_attn_sample_kernel(q_ref, k_ref, v_ref, o_ref, *, nb, ts):
    chunks = MEM_HEAD_DIM // LANES
    period = chunks * MEM_HEADS

    def head(ref, g, hh):
        parts = [ref[0, g, pl.ds(c * MEM_HEADS + hh, MEM_TOKENS, stride=period), :] for c in range(chunks)]
        return jnp.concatenate(parts, axis=1).astype(BF16)

    pairs = [(g, hh) for g in range(nb) for hh in range(MEM_HEADS)]
    ss = [_dot_t(q_ref[g * ts:(g + 1) * ts, hh * MEM_HEAD_DIM:(hh + 1) * MEM_HEAD_DIM].astype(BF16),
                 head(k_ref, g, hh)) for g, hh in pairs]
    p = _softmax_rows(jnp.concatenate(ss, axis=0))
    for i, (g, hh) in enumerate(pairs):
        o_ref[g * ts:(g + 1) * ts, hh * MEM_HEAD_DIM:(hh + 1) * MEM_HEAD_DIM] = _dot(
            p[i * ts:(i + 1) * ts].astype(BF16), head(v_ref, g, hh))


def _mem_attn_sample(q, ts, b_all, layer, cache_k, cache_v):
    nb = 4
    qspec = pl.BlockSpec((nb * ts, D_MODEL), lambda j: (j, 0))
    cspec = pl.BlockSpec((1, nb) + cache_k.shape[2:], lambda j: (layer, j, 0, 0))
    return pl.pallas_call(
        functools.partial(_mem_attn_sample_kernel, nb=nb, ts=ts),
        grid=(b_all // nb,),
        in_specs=[qspec, cspec, cspec],
        out_specs=qspec,
        out_shape=jax.ShapeDtypeStruct(q.shape, F32),
        compiler_params=_params("parallel"),
        name="mem_attn_sample",
    )(q, cache_k, cache_v)


def _norm_matmul_kernel(x_ref, g_ref, w_ref, o_ref, *, scale):
    xn = _rms(x_ref[...], g_ref[...]).astype(BF16)
    o_ref[...] = (_dot(xn, w_ref[...]) * scale).astype(o_ref.dtype)


def _norm_matmul(x, g, w, scale, out_dtype, name):
    rows = x.shape[0]
    tm = 512
    n = w.shape[1]
    return pl.pallas_call(
        functools.partial(_norm_matmul_kernel, scale=scale),
        grid=(rows // tm,),
        in_specs=[pl.BlockSpec((tm, D_MODEL), lambda i: (i, 0)),
                  pl.BlockSpec((1, D_MODEL), lambda i: (0, 0)),
                  pl.BlockSpec(w.shape, lambda i: (0, 0))],
        out_specs=pl.BlockSpec((tm, n), lambda i: (i, 0)),
        out_shape=jax.ShapeDtypeStruct((rows, n), out_dtype),
        compiler_params=_params("parallel"),
        name=name,
    )(x, g, w)


def _matmul_norm_res_kernel(a_ref, w_ref, g_ref, h_ref, o_ref):
    y = _dot(a_ref[...].astype(BF16), w_ref[...])
    o_ref[...] = h_ref[...] + _rms(y, g_ref[...])


def _matmul_norm_res(a, w, g, h, name):
    rows, k = a.shape
    tm = 512
    return pl.pallas_call(
        _matmul_norm_res_kernel,
        grid=(rows // tm,),
        in_specs=[pl.BlockSpec((tm, k), lambda i: (i, 0)),
                  pl.BlockSpec(w.shape, lambda i: (0, 0)),
                  pl.BlockSpec((1, D_MODEL), lambda i: (0, 0)),
                  pl.BlockSpec((tm, D_MODEL), lambda i: (i, 0))],
        out_specs=pl.BlockSpec((tm, D_MODEL), lambda i: (i, 0)),
        out_shape=jax.ShapeDtypeStruct((rows, D_MODEL), F32),
        compiler_params=_params("parallel"),
        name=name,
    )(a, w, g, h)


def _mlp_kernel(x_ref, gpre_ref, wup_ref, wdn_ref, gpost_ref, o_ref, *, tf):
    x = x_ref[...]
    xn = _rms(x, gpre_ref[...]).astype(BF16)
    acc = None
    for c in range(D_FF // tf):
        hid = jnp.maximum(_dot(xn, wup_ref[:, c * tf:(c + 1) * tf]), 0.0)
        part = _dot((hid * hid).astype(BF16), wdn_ref[c * tf:(c + 1) * tf, :])
        acc = part if acc is None else acc + part
    o_ref[...] = x + _rms(acc, gpost_ref[...])


def _resident(shape):
    return pl.BlockSpec(shape, lambda *_: (0,) * len(shape), pipeline_mode=pl.Buffered(1))


def _mlp(h, gpre, wup, wdn, gpost, name):
    rows = h.shape[0]
    tm, tf = 512, 1024
    return pl.pallas_call(
        functools.partial(_mlp_kernel, tf=tf),
        grid=(rows // tm,),
        in_specs=[pl.BlockSpec((tm, D_MODEL), lambda i: (i, 0)),
                  _resident((1, D_MODEL)), _resident(wup.shape), _resident(wdn.shape), _resident((1, D_MODEL))],
        out_specs=pl.BlockSpec((tm, D_MODEL), lambda i: (i, 0)),
        out_shape=jax.ShapeDtypeStruct((rows, D_MODEL), F32),
        compiler_params=_params("parallel"),
        name=name,
    )(h, gpre, wup, wdn, gpost)


def _mla_pre_kernel(h_ref, cos_ref, sin_ref, gmix_ref, gkv_ref, wdq_ref, qn_ref, wuq_ref, ukt_ref,
                    wdkv_ref, glat_ref, wkr_ref,
                    qlat_ref, qrope_ref, kvcat_ref, ckv_ref, kr_ref):
    x = h_ref[...]
    y = x * lax.rsqrt(jnp.mean(x * x, axis=-1, keepdims=True) + EPS)
    cos = cos_ref[...]
    sin = sin_ref[...]
    hn = (y * gkv_ref[...]).astype(BF16)
    ckv = _rms(_dot(hn, wdkv_ref[...]), glat_ref[...])
    kr2 = _dot(hn, wkr_ref[...])
    kr = kr2[:, :ROPE_PAD] * cos + kr2[:, ROPE_PAD:] * sin
    ckv_ref[...] = ckv
    kr_ref[...] = kr
    kvcat_ref[...] = jnp.concatenate([ckv, kr], axis=1).astype(BF16)
    xn = (y * gmix_ref[...]).astype(BF16)
    cq = _rms(_dot(xn, wdq_ref[...]), qn_ref[...]).astype(BF16)
    q = _dot(cq, wuq_ref[...])
    nope_w = MLA_HEADS * QK_NOPE
    rope_w = MLA_HEADS * ROPE_PAD
    qr = (q[:, nope_w:nope_w + rope_w] * _lane_tile(cos, rope_w)
          + q[:, nope_w + rope_w:] * _lane_tile(sin, rope_w))
    qrope_ref[...] = (qr * MLA_Q_SCALE).astype(qrope_ref.dtype)
    for hh in range(MLA_HEADS):
        qn_h = q[:, hh * QK_NOPE:(hh + 1) * QK_NOPE].astype(BF16)
        ql = _dot(qn_h, ukt_ref[hh])
        qlat_ref[:, hh * KV_LORA:(hh + 1) * KV_LORA] = (ql * MLA_Q_SCALE).astype(qlat_ref.dtype)


def _mla_pre(h, cos, sin, gmix, gkv, wdq, qn, wuq, ukt, wdkv, glat, wkr, q_dtype, name):
    rows = h.shape[0]
    tm = 512
    period = cos.shape[0] // tm
    row = lambda w: pl.BlockSpec((tm, w), lambda i: (i, 0))
    tab = pl.BlockSpec((tm, ROPE_PAD), lambda i: (i % period, 0))
    full = lambda a: pl.BlockSpec(a.shape, lambda i: (0,) * a.ndim)
    return pl.pallas_call(
        _mla_pre_kernel,
        grid=(rows // tm,),
        in_specs=[row(D_MODEL), tab, tab, full(gmix), full(gkv), full(wdq), full(qn),
                  full(wuq), full(ukt), full(wdkv), full(glat), full(wkr)],
        out_specs=(row(MLA_HEADS * KV_LORA), row(MLA_HEADS * ROPE_PAD), row(KV_LORA + ROPE_PAD),
                   row(KV_LORA), row(ROPE_PAD)),
        out_shape=(jax.ShapeDtypeStruct((rows, MLA_HEADS * KV_LORA), q_dtype),
                   jax.ShapeDtypeStruct((rows, MLA_HEADS * ROPE_PAD), q_dtype),
                   jax.ShapeDtypeStruct((rows, KV_LORA + ROPE_PAD), BF16),
                   jax.ShapeDtypeStruct((rows, KV_LORA), F32),
                   jax.ShapeDtypeStruct((rows, ROPE_PAD), F32)),
        compiler_params=_params("parallel"),
        name=name,
    )(h, cos, sin, gmix, gkv, wdq, qn, wuq, ukt, wdkv, glat, wkr)


def _mla_attn_prompt_kernel(qlat_ref, qrope_ref, kv_ref, o_ref, m_ref, l_ref, acc_ref, *, tq):
    qi = pl.program_id(1)
    m_ref[...] = jnp.full_like(m_ref, NEG_BIG)
    l_ref[...] = jnp.zeros_like(l_ref)
    acc_ref[...] = jnp.zeros_like(acc_ref)

    def block(j, r0, nr, nk, diagonal):
        k0 = pl.multiple_of(j * tq, tq)
        rws = slice(r0, r0 + nr)
        ck = kv_ref[pl.ds(k0, nk), :KV_LORA]
        kr = kv_ref[pl.ds(k0, nk), KV_LORA:]
        for hh in range(MLA_HEADS):
            s = (_dot_t(qlat_ref[rws, hh * KV_LORA:(hh + 1) * KV_LORA], ck)
                 + _dot_t(qrope_ref[rws, hh * ROPE_PAD:(hh + 1) * ROPE_PAD], kr))
            if diagonal:
                visible = (lax.broadcasted_iota(jnp.int32, (nr, nk), 1)
                           <= lax.broadcasted_iota(jnp.int32, (nr, nk), 0) + r0)
                s = jnp.where(visible, s, NEG_BIG)
            m_old = m_ref[hh, rws]
            m_new = jnp.maximum(m_old, jnp.max(s, axis=-1, keepdims=True))
            alpha = jnp.exp2(m_old - m_new)
            p = jnp.exp2(s - _lane_tile(m_new, nk))
            l_ref[hh, rws] = alpha * l_ref[hh, rws] + jnp.sum(p, axis=-1, keepdims=True)
            acc_ref[hh, rws] = _lane_tile(alpha, KV_LORA) * acc_ref[hh, rws] + _dot(p.astype(BF16), ck)
            m_ref[hh, rws] = m_new

    def body(j, _):
        block(j, 0, tq, tq, False)
        return 0

    lax.fori_loop(0, qi, body, 0)
    block(qi, 0, tq, tq, True)
    for hh in range(MLA_HEADS):
        o = acc_ref[hh] / _lane_tile(l_ref[hh], KV_LORA)
        o_ref[:, hh * KV_LORA:(hh + 1) * KV_LORA] = o.astype(BF16)


def _mla_attn_prompt(qlat, qrope, kvcat, t_len, b_all):
    tq = 512
    nq = t_len // tq
    lw, rw, kw = MLA_HEADS * KV_LORA, MLA_HEADS * ROPE_PAD, KV_LORA + ROPE_PAD
    return pl.pallas_call(
        functools.partial(_mla_attn_prompt_kernel, tq=tq),
        grid=(b_all, nq),
        in_specs=[pl.BlockSpec((tq, lw), lambda b, i: (b * nq + i, 0)),
                  pl.BlockSpec((tq, rw), lambda b, i: (b * nq + i, 0)),
                  pl.BlockSpec((t_len, kw), lambda b, i: (b, 0))],
        out_specs=pl.BlockSpec((tq, lw), lambda b, i: (b * nq + i, 0)),
        out_shape=jax.ShapeDtypeStruct((b_all * t_len, lw), BF16),
        scratch_shapes=[pltpu.VMEM((MLA_HEADS, tq, LANES), F32), pltpu.VMEM((MLA_HEADS, tq, LANES), F32),
                        pltpu.VMEM((MLA_HEADS, tq, KV_LORA), F32)],
        compiler_params=_params("parallel", "parallel"),
        name="mla_attn_prompt",
    )(qlat, qrope, kvcat)


def _stack_heads(ref, width):
    return jnp.concatenate([ref[:, hh * width:(hh + 1) * width] for hh in range(MLA_HEADS)], axis=0)


def _mla_attn_sample_kernel(pt_ref, qlat_ref, qrope_ref, ckvn_ref, krn_ref, kv_hbm, kr_hbm,
                            o_ref, kvbuf, krbuf, sem, s_ref, kvb_ref, *, ts, n_pages, parts):
    slots = kvbuf.shape[0]
    ahead = slots - 1
    b = pl.program_id(0)
    slot = b % slots
    rows = MLA_HEADS * ts
    past = n_pages * PAGE_SIZE

    def page_copies(batch, sl, page_of):
        cps = []
        for p in range(n_pages):
            page = page_of(batch, p)
            cps.append(pltpu.make_async_copy(kv_hbm.at[page], kvbuf.at[sl, p], sem.at[0, sl]))
            cps.append(pltpu.make_async_copy(kr_hbm.at[page], krbuf.at[sl, p], sem.at[1, sl]))
        return cps

    table_page = lambda batch, p: pt_ref[batch * n_pages + p]
    last = pl.num_programs(0) - 1

    any_page = lambda batch, p: 0

    @pl.when(b == 0)
    def _():
        for d in range(ahead):
            for cp in page_copies(jnp.minimum(d, last), d, table_page):
                cp.start()

    for cp in page_copies(b, slot, any_page):
        cp.wait()

    for cp in page_copies(jnp.minimum(b + ahead, last), (b + ahead) % slots, table_page):
        cp.start()

    ql = _stack_heads(qlat_ref, KV_LORA).astype(BF16)
    qr = _stack_heads(qrope_ref, ROPE_PAD).astype(BF16)
    qr_cache = qr[:, :QK_ROPE]
    pair = 2 * PAGE_SIZE
    pairs_per_part = n_pages // 2 // parts
    span = pairs_per_part * pair
    ms, ls, os = [], [], []
    for part in range(parts):
        for g in range(part * pairs_per_part, (part + 1) * pairs_per_part):
            ck = kvbuf[slot, 2 * g:2 * g + 2].reshape(pair, KV_LORA).astype(BF16)
            kr = jnp.concatenate([krbuf[slot, 2 * g], krbuf[slot, 2 * g + 1]], axis=1).astype(BF16)
            s_ref[:, g * pair:(g + 1) * pair] = _dot_t(ql, ck) + _dot(qr_cache, kr)
            kvb_ref[g * pair:(g + 1) * pair, :] = ck
        hi = (part + 1) * span
        if part == parts - 1:
            pad = PAGE_SIZE - ts
            ckn = jnp.concatenate([ckvn_ref[...], jnp.zeros((pad, KV_LORA), F32)], axis=0).astype(BF16)
            krn = jnp.concatenate([krn_ref[...], jnp.zeros((pad, ROPE_PAD), F32)], axis=0).astype(BF16)
            s_new = _dot_t(ql, ckn) + _dot_t(qr, krn)
            t_q = lax.broadcasted_iota(jnp.int32, (rows, PAGE_SIZE), 0) & (ts - 1)
            t_k = lax.broadcasted_iota(jnp.int32, (rows, PAGE_SIZE), 1)
            s_ref[:, past:] = jnp.where(t_k <= t_q, s_new, NEG_BIG)
            kvb_ref[past:, :] = ckn
            hi = past + PAGE_SIZE
        s = s_ref[:, part * span:hi]
        m = jnp.max(s, axis=-1, keepdims=True)
        p = jnp.exp2(s - m)
        ms.append(m)
        ls.append(jnp.sum(p, axis=-1, keepdims=True))
        os.append(_dot(p.astype(BF16), kvb_ref[part * span:hi, :]))
    m_all = functools.reduce(jnp.maximum, ms)
    ws = [jnp.exp2(m - m_all) for m in ms]
    o = sum(w * o_part for w, o_part in zip(ws, os)) / sum(w * l for w, l in zip(ws, ls))
    for hh in range(MLA_HEADS):
        o_ref[:, hh * KV_LORA:(hh + 1) * KV_LORA] = o[hh * ts:(hh + 1) * ts]

    @pl.when(b == last)
    def _():
        for d in range(1, slots):
            for cp in page_copies(b, (b + d) % slots, any_page):
                cp.wait()


def _mla_attn_sample(qlat, qrope, ckv_new, kr_new, cache_kv, cache_kr_t, page_table, ts, b_all):
    slots = 3
    n_pages = page_table.shape[1]
    lw, rw = MLA_HEADS * KV_LORA, MLA_HEADS * ROPE_PAD
    rows = MLA_HEADS * ts
    keys = (n_pages + 1) * PAGE_SIZE
    pt = page_table.reshape(-1)
    stream = lambda w: pl.BlockSpec((ts, w), lambda b, pt_ref: (b, 0))
    hbm = pl.BlockSpec(memory_space=pl.ANY)
    grid_spec = pltpu.PrefetchScalarGridSpec(
        num_scalar_prefetch=1,
        grid=(b_all,),
        in_specs=[stream(lw), stream(rw), stream(KV_LORA), stream(ROPE_PAD), hbm, hbm],
        out_specs=stream(lw),
        scratch_shapes=[pltpu.VMEM((slots, n_pages, PAGE_SIZE, KV_LORA), F32),
                        pltpu.VMEM((slots, n_pages, QK_ROPE, PAGE_SIZE), F32),
                        pltpu.SemaphoreType.DMA((2, slots)),
                        pltpu.VMEM((rows, keys), F32),
                        pltpu.VMEM((keys, KV_LORA), BF16)],
    )
    return pl.pallas_call(
        functools.partial(_mla_attn_sample_kernel, ts=ts, n_pages=n_pages, parts=4),
        grid_spec=grid_spec,
        out_shape=jax.ShapeDtypeStruct((b_all * ts, lw), F32),
        compiler_params=_params("arbitrary"),
        name="mla_attn_sample",
    )(pt, qlat, qrope, ckv_new, kr_new, cache_kv, cache_kr_t)


def _mla_out_kernel(o_ref, wuv_ref, wo_ref, g_ref, h_ref, out_ref):
    for rws in _row_halves(h_ref.shape[0]):
        vs = []
        for hh in range(MLA_HEADS):
            vs.append(_dot(o_ref[rws, hh * KV_LORA:(hh + 1) * KV_LORA].astype(BF16), wuv_ref[hh]).astype(BF16))
        y = _dot(jnp.concatenate(vs, axis=1), wo_ref[...])
        out_ref[rws, :] = h_ref[rws, :] + _rms(y, g_ref[...])


def _mla_out(o, wuv, wo, g, h, name):
    rows = h.shape[0]
    tm = 1024
    lw = MLA_HEADS * KV_LORA
    return pl.pallas_call(
        _mla_out_kernel,
        grid=(rows // tm,),
        in_specs=[pl.BlockSpec((tm, lw), lambda i: (i, 0)),
                  pl.BlockSpec(wuv.shape, lambda i: (0, 0, 0)),
                  pl.BlockSpec(wo.shape, lambda i: (0, 0)),
                  pl.BlockSpec((1, D_MODEL), lambda i: (0, 0)),
                  pl.BlockSpec((tm, D_MODEL), lambda i: (i, 0))],
        out_specs=pl.BlockSpec((tm, D_MODEL), lambda i: (i, 0)),
        out_shape=jax.ShapeDtypeStruct((rows, D_MODEL), F32),
        compiler_params=_params("parallel"),
        name=name,
    )(o, wuv, wo, g, h)


def _rope_tables(pos):
    half = QK_ROPE // 2
    inv = ROPE_BASE ** (-jnp.arange(half, dtype=F32) / half)
    ang = pos.astype(F32)[:, None] * inv[None, :]
    cos = jnp.cos(ang)
    sin = jnp.sin(ang)
    zeros = jnp.zeros((pos.shape[0], ROPE_PAD - QK_ROPE), F32)
    return jnp.concatenate([cos, cos, zeros], axis=1), jnp.concatenate([sin, sin, zeros], axis=1)


def _rot_cols(w):
    half = QK_ROPE // 2
    return jnp.concatenate([-w[..., half:], w[..., :half]], axis=-1)


def _pad_rope(w):
    pad = [(0, 0)] * (w.ndim - 1) + [(0, ROPE_PAD - QK_ROPE)]
    return jnp.pad(w, pad)


def kernel(x_prompt, x_sample, cache_ssm_re, cache_ssm_im, cache_kv_latent, cache_k_rope, cache_mem_k, cache_mem_v, page_table, mem_prompt, norm_mix_pre, norm_mix_post, norm_mem_pre, norm_mem_post, norm_mlp_pre, norm_mlp_post, mem_in_norm, w_mem_q, w_mem_k, w_mem_v, w_mem_o, w_mlp_up, w_mlp_down, ssm_a_re, ssm_a_im, ssm_log_dt, ssm_b_re, ssm_b_im, ssm_c_re, ssm_c_im, ssm_d, w_glu, kv_in_norm, w_dkv, kv_latent_norm, w_kr, w_uk, w_uv, w_dq, q_norm, w_uq, w_o):
    depth = norm_mix_pre.shape[0]
    n_a = ssm_a_re.shape[0]
    bp, sp, _ = x_prompt.shape
    bs, ts, _ = x_sample.shape
    past_len = page_table.shape[1] * PAGE_SIZE

    hp = x_prompt.astype(F32).reshape(bp * sp, D_MODEL)
    hs = x_sample.astype(F32).reshape(bs * ts, D_MODEL)

    mem_rows = mem_prompt.astype(F32).reshape(bp * MEM_TOKENS, D_MODEL)
    mk, mv, mkb, mvb = _mem_kv(mem_rows, mem_in_norm.astype(F32).reshape(depth, 1, D_MODEL),
                               w_mem_k.astype(BF16), w_mem_v.astype(BF16))
    cmk = _mem_cache_view(cache_mem_k.astype(F32))
    cmv = _mem_cache_view(cache_mem_v.astype(F32))

    ssm_p_re, ssm_p_im, ssm_s_re, ssm_s_im = [], [], [], []
    ckv_p = kr_p = ckv_s = kr_s = None
    for i in range(depth):
        if i < n_a:
            abar_re, abar_im, bbar_re, bbar_im = _ssm_discretise(
                ssm_a_re[i], ssm_a_im[i], ssm_log_dt[i], ssm_b_re[i], ssm_b_im[i])
            ng = SSM_GROUPS // SSM_KBLOCKS
            bdre = _block_diag(bbar_re.reshape(SSM_KBLOCKS, ng, SSM_GROUP, SSM_STATE)).astype(BF16)
            bdim = _block_diag(bbar_im.reshape(SSM_KBLOCKS, ng, SSM_GROUP, SSM_STATE)).astype(BF16)
            c_t = lambda c: c.astype(F32).reshape(SSM_KBLOCKS, ng, SSM_GROUP, SSM_STATE).transpose(0, 1, 3, 2)
            cre = _block_diag(c_t(ssm_c_re[i])).astype(BF16)
            ncim = _block_diag(-c_t(ssm_c_im[i])).astype(BF16)
            s5_w = (_row(norm_mix_pre[i]), bdre, bdim, abar_re.reshape(1, SSM_WIDTH), abar_im.reshape(1, SSM_WIDTH))
            s5_w2 = (cre, ncim, _row(ssm_d[i]), w_glu[i].astype(BF16), _row(norm_mix_post[i]))
            zeros = jnp.zeros((bp, SSM_WIDTH), F32)
            hp3, fpr, fpi = _s5_layer(hp.reshape(bp, sp, D_MODEL), 32, bp, *s5_w, zeros, zeros, *s5_w2,
                                      name="s5_prompt")
            hs3, fsr, fsi = _s5_layer(hs.reshape(bs, ts, D_MODEL), ts, 32, *s5_w,
                                      cache_ssm_re[i].astype(F32).reshape(bs, SSM_WIDTH),
                                      cache_ssm_im[i].astype(F32).reshape(bs, SSM_WIDTH), *s5_w2, name="s5_sample")
            hp = hp3.reshape(bp * sp, D_MODEL)
            hs = hs3.reshape(bs * ts, D_MODEL)
            ssm_p_re.append(fpr.reshape(bp, SSM_GROUPS, SSM_STATE))
            ssm_p_im.append(fpi.reshape(bp, SSM_GROUPS, SSM_STATE))
            ssm_s_re.append(fsr.reshape(bs, SSM_GROUPS, SSM_STATE))
            ssm_s_im.append(fsi.reshape(bs, SSM_GROUPS, SSM_STATE))
        else:
            j = i - n_a
            wuq = w_uq[j].astype(F32)
            w_nope = wuq[:, :, :QK_NOPE].reshape(Q_LORA, MLA_HEADS * QK_NOPE)
            w_r = wuq[:, :, QK_NOPE:]
            w_rope = _pad_rope(w_r).reshape(Q_LORA, MLA_HEADS * ROPE_PAD)
            w_rope_rot = _pad_rope(_rot_cols(w_r)).reshape(Q_LORA, MLA_HEADS * ROPE_PAD)
            wuq_all = jnp.concatenate([w_nope, w_rope, w_rope_rot], axis=1).astype(BF16)
            ukt = w_uk.astype(F32).transpose(1, 2, 0).astype(BF16)
            wkr = w_kr.astype(F32)
            wkr2 = jnp.concatenate([_pad_rope(wkr), _pad_rope(_rot_cols(wkr))], axis=1).astype(BF16)
            pre_w = (_row(norm_mix_pre[i]), _row(kv_in_norm), w_dq[j].astype(BF16), _row(q_norm[j]), wuq_all, ukt,
                     w_dkv.astype(BF16), _row(kv_latent_norm), wkr2)
            cos_p, sin_p = _rope_tables(jnp.arange(sp, dtype=jnp.int32))
            cos_s, sin_s = _rope_tables(past_len + jnp.tile(jnp.arange(ts, dtype=jnp.int32), 512 // ts))
            qlat_p, qrope_p, kvcat_p, ckv_pj, kr_pj = _mla_pre(hp, cos_p, sin_p, *pre_w, BF16, name="mla_pre_prompt")
            qlat_s, qrope_s, _, ckv_sj, kr_sj = _mla_pre(hs, cos_s, sin_s, *pre_w, F32, name="mla_pre_sample")
            if i == n_a:
                ckv_p, kr_p, ckv_s, kr_s, kvcat_shared = ckv_pj, kr_pj, ckv_sj, kr_sj, kvcat_p
            o_p = _mla_attn_prompt(qlat_p, qrope_p, kvcat_shared, sp, bp)
            o_s = _mla_attn_sample(qlat_s, qrope_s, ckv_s, kr_s, cache_kv_latent.astype(F32),
                                   jnp.swapaxes(cache_k_rope.astype(F32), 1, 2), page_table, ts, bs)
            wuv = w_uv.astype(F32).transpose(1, 0, 2).astype(BF16)
            wo = w_o[j].astype(BF16)
            hp = _mla_out(o_p, wuv, wo, _row(norm_mix_post[i]), hp, name="mla_out_prompt")
            hs = _mla_out(o_s, wuv, wo, _row(norm_mix_post[i]), hs, name="mla_out_sample")

        wq = w_mem_q[i].astype(BF16)
        wo_mem = w_mem_o[i].astype(BF16)
        hp = _mem_attn_prompt(hp, sp, bp, i, _row(norm_mem_pre[i]), wq, mkb, mvb, wo_mem, _row(norm_mem_post[i]))
        q_s = _norm_matmul(hs, _row(norm_mem_pre[i]), wq, MEM_SCALE, F32, name="mem_q_sample")
        o_s = _mem_attn_sample(q_s, ts, bs, i, cmk, cmv)
        hs = _matmul_norm_res(o_s, wo_mem, _row(norm_mem_post[i]), hs, name="mem_out_sample")

        wup = w_mlp_up[i].astype(BF16)
        wdn = w_mlp_down[i].astype(BF16)
        hp = _mlp(hp, _row(norm_mlp_pre[i]), wup, wdn, _row(norm_mlp_post[i]), name="mlp_prompt")
        hs = _mlp(hs, _row(norm_mlp_pre[i]), wup, wdn, _row(norm_mlp_post[i]), name="mlp_sample")

    return (hp.reshape(bp, sp, D_MODEL), hs.reshape(bs, ts, D_MODEL),
            jnp.stack(ssm_p_re), jnp.stack(ssm_p_im), jnp.stack(ssm_s_re), jnp.stack(ssm_s_im),
            ckv_p.reshape(bp, sp, KV_LORA), kr_p[:, :QK_ROPE].reshape(bp, sp, QK_ROPE),
            ckv_s.reshape(bs, ts, KV_LORA), kr_s[:, :QK_ROPE].reshape(bs, ts, QK_ROPE),
            _mem_cache_unview(mk, bp), _mem_cache_unview(mv, bp))
```

```python
import functools
import math

import jax
import jax.numpy as jnp
from jax import lax
from jax.experimental import pallas as pl
from jax.experimental.pallas import tpu as pltpu

F32 = jnp.float32
BF16 = jnp.bfloat16

LANES = 128
SUBLANES = 8
D_MODEL = 1024
PAGE_SIZE = 128
SSM_GROUP = 16
SSM_GROUPS = D_MODEL // SSM_GROUP
SSM_STATE = 64
SSM_WIDTH = SSM_GROUPS * SSM_STATE
SSM_KBLOCKS = 4
SSM_SLABS = SSM_WIDTH // LANES
MLA_HEADS = 8
QK_NOPE = 128
QK_ROPE = 64
ROPE_PAD = LANES
V_HEAD = 128
KV_LORA = 256
Q_LORA = 384
ROPE_BASE = 10000.0
MLA_SCALE = (QK_NOPE + QK_ROPE) ** -0.5
MLA_Q_SCALE = MLA_SCALE * math.log2(math.e)
MEM_TOKENS = 256
MEM_HEADS = 4
MEM_HEAD_DIM = D_MODEL // MEM_HEADS
MEM_SCALE = MEM_HEAD_DIM ** -0.5
D_FF = 4 * D_MODEL
EPS = 1e-6
NEG_BIG = -1e30

VMEM_LIMIT_BYTES = 56 * 1024 * 1024


def _params(*semantics):
    return pltpu.CompilerParams(dimension_semantics=semantics, vmem_limit_bytes=VMEM_LIMIT_BYTES)


def _dot(a, b):
    return jnp.dot(a, b, preferred_element_type=F32)


def _dot_t(a, b):
    return lax.dot_general(a, b, (((1,), (1,)), ((), ())), preferred_element_type=F32)


def _rms(x, g):
    return x * lax.rsqrt(jnp.mean(x * x, axis=-1, keepdims=True) + EPS) * g


def _row(v):
    return v.reshape(1, -1).astype(F32)


def _lane_tile(x, width):
    return jnp.concatenate([x] * (width // LANES), axis=1) if width > LANES else x


def _ssm_disc_kernel(ar_ref, ai_ref, ldt_ref, br_ref, bi_ref,
                     abar_re_ref, abar_im_ref, bbar_re_ref, bbar_im_ref):
    ar = ar_ref[...]
    ai = ai_ref[...]
    dt = jnp.exp(ldt_ref[...])
    mag = jnp.exp(ar * dt)
    abar_re = mag * jnp.cos(ai * dt)
    abar_im = mag * jnp.sin(ai * dt)
    den = ar * ar + ai * ai
    nr = abar_re - 1.0
    ni = abar_im
    coef_re = (nr * ar + ni * ai) / den
    coef_im = (ni * ar - nr * ai) / den
    br = br_ref[...]
    bi = bi_ref[...]
    abar_re_ref[...] = abar_re
    abar_im_ref[...] = abar_im
    bbar_re_ref[...] = coef_re * br - coef_im * bi
    bbar_im_ref[...] = coef_re * bi + coef_im * br


def _ssm_discretise(a_re, a_im, log_dt, b_re, b_im):
    g, n, k = SSM_GROUPS, SSM_STATE, SSM_GROUP
    rep = lambda a: jnp.repeat(a.astype(F32), k, axis=0)
    brt = b_re.astype(F32).transpose(0, 2, 1).reshape(g * k, n)
    bit = b_im.astype(F32).transpose(0, 2, 1).reshape(g * k, n)
    ldt = jnp.repeat(log_dt.astype(F32), k, axis=0).reshape(g * k, 1)
    shp = jax.ShapeDtypeStruct((g * k, n), F32)
    abar_re, abar_im, bbar_re, bbar_im = pl.pallas_call(
        _ssm_disc_kernel, out_shape=(shp, shp, shp, shp), name="ssm_discretise",
    )(rep(a_re), rep(a_im), ldt, brt, bit)
    return abar_re[::k], abar_im[::k], bbar_re, bbar_im


def _block_diag(x):
    kb, ng, r, c = x.shape
    eye = jnp.eye(ng, dtype=x.dtype)
    return (x[:, :, :, None, :] * eye[None, :, None, :, None]).reshape(kb, ng * r, ng * c)


def _s5_kernel(x_ref, gpre_ref, bdre_ref, bdim_ref, are_ref, aim_ref, h0re_ref, h0im_ref,
               cre_ref, ncim_ref, dskip_ref, wglu_ref, gpost_ref,
               out_ref, fre_ref, fim_ref, *scratch, tt, bb, nsub, xpitch, rpitch):
    ti = pl.program_id(1)

    @pl.when(ti == 0)
    def _():
        fre_ref[...] = h0re_ref[...]
        fim_ref[...] = h0im_ref[...]

    for sub in range(nsub):
        _s5_subtile(x_ref, gpre_ref, bdre_ref, bdim_ref, are_ref, aim_ref, cre_ref, ncim_ref, dskip_ref,
                    wglu_ref, gpost_ref, out_ref, fre_ref, fim_ref, *scratch[4 * sub:4 * sub + 4],
                    t0=sub * tt, tt=tt, bb=bb, xpitch=xpitch, rpitch=rpitch)


def _s5_subtile(x_ref, gpre_ref, bdre_ref, bdim_ref, are_ref, aim_ref, cre_ref, ncim_ref, dskip_ref,
                wglu_ref, gpost_ref, out_ref, fre_ref, fim_ref, xs_ref, bure_ref, buim_ref, rs_ref,
                *, t0, tt, bb, xpitch, rpitch):
    rows = tt * bb
    nsl = D_MODEL // LANES
    spk = SSM_SLABS // SSM_KBLOCKS
    steps = slice(t0, t0 + tt)

    for b in range(bb):
        for s in range(nsl):
            xs_ref[s, b * xpitch:b * xpitch + tt, :] = x_ref[b, steps, s * LANES:(s + 1) * LANES]
    x = jnp.concatenate(
        [jnp.concatenate([xs_ref[s, pl.ds(t, bb, stride=xpitch), :] for s in range(nsl)], axis=1)
         for t in range(tt)], axis=0)
    xn = _rms(x, gpre_ref[...])
    xb = xn.astype(BF16)
    kw = D_MODEL // SSM_KBLOCKS
    sw = SSM_WIDTH // SSM_KBLOCKS

    def scan(slab):
        lanes = slice(slab * LANES, (slab + 1) * LANES)
        a_re = jnp.broadcast_to(are_ref[:, lanes], (SUBLANES, LANES))
        a_im = jnp.broadcast_to(aim_ref[:, lanes], (SUBLANES, LANES))
        for vb in range(bb // SUBLANES):
            b0 = vb * SUBLANES
            h_re = fre_ref[b0:b0 + SUBLANES, lanes]
            h_im = fim_ref[b0:b0 + SUBLANES, lanes]
            for t in range(tt):
                rws = slice(t * bb + b0, t * bb + b0 + SUBLANES)
                h_re, h_im = (a_re * h_re - a_im * h_im + bure_ref[rws, lanes],
                              a_re * h_im + a_im * h_re + buim_ref[rws, lanes])
                bure_ref[rws, lanes] = h_re
                buim_ref[rws, lanes] = h_im
            fre_ref[b0:b0 + SUBLANES, lanes] = h_re
            fim_ref[b0:b0 + SUBLANES, lanes] = h_im

    ys = []
    for kb in range(SSM_KBLOCKS):
        cols = slice(kb * sw, (kb + 1) * sw)
        xk = xb[:, kb * kw:(kb + 1) * kw]
        bure_ref[:, cols] = _dot(xk, bdre_ref[kb])
        buim_ref[:, cols] = _dot(xk, bdim_ref[kb])
        for s in range(spk):
            scan(kb * spk + s)
        ys.append(_dot(bure_ref[:, cols].astype(BF16), cre_ref[kb])
                  + _dot(buim_ref[:, cols].astype(BF16), ncim_ref[kb]))
    y = jnp.concatenate(ys, axis=1) + dskip_ref[...] * xn
    g = 0.5 * y * (1.0 + lax.erf(y * (2.0 ** -0.5)))
    z = _dot(g.astype(BF16), wglu_ref[...])
    o = z[:, :D_MODEL] * (1.0 / (1.0 + jnp.exp(-z[:, D_MODEL:])))
    r = _rms(o, gpost_ref[...])
    for s in range(nsl):
        for t in range(tt):
            rs_ref[s, t * rpitch:t * rpitch + bb, :] = r[t * bb:(t + 1) * bb, s * LANES:(s + 1) * LANES]
    for b in range(bb):
        rb = jnp.concatenate([rs_ref[s, pl.ds(b, tt, stride=rpitch), :] for s in range(nsl)], axis=1)
        out_ref[b, steps, :] = x_ref[b, steps, :] + rb


def _odd_pitch(n):
    groups = -(-n // SUBLANES)
    return SUBLANES * (groups if groups % 2 else groups + 1)


def _s5_layer(x3, tt, nsub, bb, gpre, bdre, bdim, a_re, a_im, h0_re, h0_im,
              cre, ncim, dskip, wglu, gpost, name):
    b_all, t_len, _ = x3.shape
    grid = (b_all // bb, t_len // (tt * nsub))
    xpitch, rpitch = _odd_pitch(tt), _odd_pitch(bb)
    nsl = D_MODEL // LANES
    bu = pltpu.VMEM((tt * bb, SSM_WIDTH), F32)
    subtile_scratch = [pltpu.VMEM((nsl, bb * xpitch, LANES), F32), bu, bu,
                       pltpu.VMEM((nsl, tt * rpitch, LANES), F32)]
    return pl.pallas_call(
        functools.partial(_s5_kernel, tt=tt, bb=bb, nsub=nsub, xpitch=xpitch, rpitch=rpitch),
        grid=grid,
        in_specs=[
            pl.BlockSpec((bb, tt * nsub, D_MODEL), lambda bj, ti: (bj, ti, 0)),
            _resident((1, D_MODEL)),
            _resident(bdre.shape),
            _resident(bdim.shape),
            _resident((1, SSM_WIDTH)),
            _resident((1, SSM_WIDTH)),
            pl.BlockSpec((bb, SSM_WIDTH), lambda bj, ti: (bj, 0)),
            pl.BlockSpec((bb, SSM_WIDTH), lambda bj, ti: (bj, 0)),
            _resident(cre.shape),
            _resident(ncim.shape),
            _resident((1, D_MODEL)),
            _resident(wglu.shape),
            _resident((1, D_MODEL)),
        ],
        out_specs=(
            pl.BlockSpec((bb, tt * nsub, D_MODEL), lambda bj, ti: (bj, ti, 0)),
            pl.BlockSpec((bb, SSM_WIDTH), lambda bj, ti: (bj, 0)),
            pl.BlockSpec((bb, SSM_WIDTH), lambda bj, ti: (bj, 0)),
        ),
        out_shape=(
            jax.ShapeDtypeStruct((b_all, t_len, D_MODEL), F32),
            jax.ShapeDtypeStruct((b_all, SSM_WIDTH), F32),
            jax.ShapeDtypeStruct((b_all, SSM_WIDTH), F32),
        ),
        scratch_shapes=subtile_scratch * nsub,
        compiler_params=_params("parallel", "arbitrary"),
        name=name,
    )(x3, gpre, bdre, bdim, a_re, a_im, h0_re, h0_im, cre, ncim, dskip, wglu, gpost)


def _mem_kv_kernel(x_ref, g_ref, wk_ref, wv_ref, k_ref, v_ref, kb_ref, vb_ref):
    xn = _rms(x_ref[...], g_ref[0]).astype(BF16)
    k = _dot(xn, wk_ref[0])
    v = _dot(xn, wv_ref[0])
    tm = k.shape[0]
    chunks = MEM_HEAD_DIM // LANES
    period = chunks * MEM_HEADS
    for hh in range(MEM_HEADS):
        for c in range(chunks):
            lanes = slice(hh * MEM_HEAD_DIM + c * LANES, hh * MEM_HEAD_DIM + (c + 1) * LANES)
            k_ref[0, pl.ds(c * MEM_HEADS + hh, tm, stride=period), :] = k[:, lanes]
            v_ref[0, pl.ds(c * MEM_HEADS + hh, tm, stride=period), :] = v[:, lanes]
    kb_ref[0] = k.astype(BF16)
    vb_ref[0] = v.astype(BF16)


def _mem_kv(mem, g, wk, wv):
    rows = mem.shape[0]
    nl = g.shape[0]
    tm = 512
    period = D_MODEL // LANES
    blk = pl.BlockSpec((1, tm, D_MODEL), lambda l, i: (l, i, 0))
    nat = pl.BlockSpec((1, tm * period, LANES), lambda l, i: (l, i, 0))
    wspec = pl.BlockSpec((1, D_MODEL, D_MODEL), lambda l, i: (l, 0, 0))
    return pl.pallas_call(
        _mem_kv_kernel,
        grid=(nl, rows // tm),
        in_specs=[pl.BlockSpec((tm, D_MODEL), lambda l, i: (i, 0)),
                  pl.BlockSpec((1, 1, D_MODEL), lambda l, i: (l, 0, 0)), wspec, wspec],
        out_specs=(nat, nat, blk, blk),
        out_shape=(jax.ShapeDtypeStruct((nl, rows * period, LANES), F32),) * 2
        + (jax.ShapeDtypeStruct((nl, rows, D_MODEL), BF16),) * 2,
        compiler_params=_params("parallel", "parallel"),
        name="mem_kv",
    )(mem, g, wk, wv)


def _softmax_rows(s):
    m = jnp.max(s, axis=-1, keepdims=True)
    p = jnp.exp(s - m)
    return p / jnp.sum(p, axis=-1, keepdims=True)


def _row_halves(n):
    return (slice(0, n // 2), slice(n // 2, n))


def _mem_attn_prompt_kernel(h_ref, gpre_ref, wq_ref, k_ref, v_ref, wo_ref, gpost_ref, out_ref):
    for rws in _row_halves(h_ref.shape[0]):
        x = h_ref[rws, :]
        xn = _rms(x, gpre_ref[...]).astype(BF16)
        q = (_dot(xn, wq_ref[...]) * MEM_SCALE).astype(BF16)
        os = []
        for hh in range(MEM_HEADS):
            cols = slice(hh * MEM_HEAD_DIM, (hh + 1) * MEM_HEAD_DIM)
            p = _softmax_rows(_dot_t(q[:, cols], k_ref[0, :, cols]))
            os.append(_dot(p.astype(BF16), v_ref[0, :, cols]).astype(BF16))
        y = _dot(jnp.concatenate(os, axis=1), wo_ref[...])
        out_ref[rws, :] = x + _rms(y, gpost_ref[...])


def _mem_attn_prompt(h, t_len, b_all, layer, gpre, wq, kb, vb, wo, gpost):
    tq = 1024
    nq = t_len // tq
    const2 = lambda b, i: (0, 0)
    hspec = pl.BlockSpec((tq, D_MODEL), lambda b, i: (b * nq + i, 0))
    kvspec = pl.BlockSpec((1, MEM_TOKENS, D_MODEL), lambda b, i: (layer, b, 0))
    wspec = pl.BlockSpec((D_MODEL, D_MODEL), const2)
    gspec = pl.BlockSpec((1, D_MODEL), const2)
    return pl.pallas_call(
        _mem_attn_prompt_kernel,
        grid=(b_all, nq),
        in_specs=[hspec, gspec, wspec, kvspec, kvspec, wspec, gspec],
        out_specs=hspec,
        out_shape=jax.ShapeDtypeStruct(h.shape, F32),
        compiler_params=_params("parallel", "parallel"),
        name="mem_attn_prompt",
    )(h, gpre, wq, kb, vb, wo, gpost)


def _mem_cache_view(cache):
    nl, nb, m, nh, dh = cache.shape
    chunks = dh // LANES
    return cache.reshape(nl, nb, m, nh, chunks, LANES).transpose(0, 1, 2, 4, 3, 5).reshape(
        nl, nb, m * chunks * nh, LANES)


def _mem_cache_unview(x, nb):
    nl = x.shape[0]
    chunks = MEM_HEAD_DIM // LANES
    return x.reshape(nl, nb, MEM_TOKENS, chunks, MEM_HEADS, LANES).transpose(0, 1, 2, 4, 3, 5).reshape(
        nl, nb, MEM_TOKENS, MEM_HEADS, MEM_HEAD_DIM)


def _mem_attn_sample_kernel(q_ref, k_ref, v_ref, o_ref, *, nb, ts):
    chunks = MEM_HEAD_DIM // LANES
    period = chunks * MEM_HEADS

    def head(ref, g, hh):
        parts = [ref[0, g, pl.ds(c * MEM_HEADS + hh, MEM_TOKENS, stride=period), :] for c in range(chunks)]
        return jnp.concatenate(parts, axis=1).astype(BF16)

    pairs = [(g, hh) for g in range(nb) for hh in range(MEM_HEADS)]
    ss = [_dot_t(q_ref[g * ts:(g + 1) * ts, hh * MEM_HEAD_DIM:(hh + 1) * MEM_HEAD_DIM].astype(BF16),
                 head(k_ref, g, hh)) for g, hh in pairs]
    p = _softmax_rows(jnp.concatenate(ss, axis=0))
    for i, (g, hh) in enumerate(pairs):
        o_ref[g * ts:(g + 1) * ts, hh * MEM_HEAD_DIM:(hh + 1) * MEM_HEAD_DIM] = _dot(
            p[i * ts:(i + 1) * ts].astype(BF16), head(v_ref, g, hh))


def _mem_attn_sample(q, ts, b_all, layer, cache_k, cache_v):
    nb = 4
    qspec = pl.BlockSpec((nb * ts, D_MODEL), lambda j: (j, 0))
    cspec = pl.BlockSpec((1, nb) + cache_k.shape[2:], lambda j: (layer, j, 0, 0))
    return pl.pallas_call(
        functools.partial(_mem_attn_sample_kernel, nb=nb, ts=ts),
        grid=(b_all // nb,),
        in_specs=[qspec, cspec, cspec],
        out_specs=qspec,
        out_shape=jax.ShapeDtypeStruct(q.shape, F32),
        compiler_params=_params("parallel"),
        name="mem_attn_sample",
    )(q, cache_k, cache_v)


def _norm_matmul_kernel(x_ref, g_ref, w_ref, o_ref, *, scale):
    xn = _rms(x_ref[...], g_ref[...]).astype(BF16)
    o_ref[...] = (_dot(xn, w_ref[...]) * scale).astype(o_ref.dtype)


def _norm_matmul(x, g, w, scale, out_dtype, name):
    rows = x.shape[0]
    tm = 512
    n = w.shape[1]
    return pl.pallas_call(
        functools.partial(_norm_matmul_kernel, scale=scale),
        grid=(rows // tm,),
        in_specs=[pl.BlockSpec((tm, D_MODEL), lambda i: (i, 0)),
                  pl.BlockSpec((1, D_MODEL), lambda i: (0, 0)),
                  pl.BlockSpec(w.shape, lambda i: (0, 0))],
        out_specs=pl.BlockSpec((tm, n), lambda i: (i, 0)),
        out_shape=jax.ShapeDtypeStruct((rows, n), out_dtype),
        compiler_params=_params("parallel"),
        name=name,
    )(x, g, w)


def _matmul_norm_res_kernel(a_ref, w_ref, g_ref, h_ref, o_ref):
    y = _dot(a_ref[...].astype(BF16), w_ref[...])
    o_ref[...] = h_ref[...] + _rms(y, g_ref[...])


def _matmul_norm_res(a, w, g, h, name):
    rows, k = a.shape
    tm = 512
    return pl.pallas_call(
        _matmul_norm_res_kernel,
        grid=(rows // tm,),
        in_specs=[pl.BlockSpec((tm, k), lambda i: (i, 0)),
                  pl.BlockSpec(w.shape, lambda i: (0, 0)),
                  pl.BlockSpec((1, D_MODEL), lambda i: (0, 0)),
                  pl.BlockSpec((tm, D_MODEL), lambda i: (i, 0))],
        out_specs=pl.BlockSpec((tm, D_MODEL), lambda i: (i, 0)),
        out_shape=jax.ShapeDtypeStruct((rows, D_MODEL), F32),
        compiler_params=_params("parallel"),
        name=name,
    )(a, w, g, h)


def _mlp_kernel(x_ref, gpre_ref, wup_ref, wdn_ref, gpost_ref, o_ref, *, tf):
    x = x_ref[...]
    xn = _rms(x, gpre_ref[...]).astype(BF16)
    acc = None
    for c in range(D_FF // tf):
        hid = jnp.maximum(_dot(xn, wup_ref[:, c * tf:(c + 1) * tf]), 0.0)
        part = _dot((hid * hid).astype(BF16), wdn_ref[c * tf:(c + 1) * tf, :])
        acc = part if acc is None else acc + part
    o_ref[...] = x + _rms(acc, gpost_ref[...])


def _resident(shape):
    return pl.BlockSpec(shape, lambda *_: (0,) * len(shape), pipeline_mode=pl.Buffered(1))


def _mlp(h, gpre, wup, wdn, gpost, name):
    rows = h.shape[0]
    tm, tf = 512, 1024
    return pl.pallas_call(
        functools.partial(_mlp_kernel, tf=tf),
        grid=(rows // tm,),
        in_specs=[pl.BlockSpec((tm, D_MODEL), lambda i: (i, 0)),
                  _resident((1, D_MODEL)), _resident(wup.shape), _resident(wdn.shape), _resident((1, D_MODEL))],
        out_specs=pl.BlockSpec((tm, D_MODEL), lambda i: (i, 0)),
        out_shape=jax.ShapeDtypeStruct((rows, D_MODEL), F32),
        compiler_params=_params("parallel"),
        name=name,
    )(h, gpre, wup, wdn, gpost)


def _mla_pre_kernel(h_ref, cos_ref, sin_ref, gmix_ref, gkv_ref, wdq_ref, qn_ref, wuq_ref, ukt_ref,
                    wdkv_ref, glat_ref, wkr_ref,
                    qlat_ref, qrope_ref, kvcat_ref, ckv_ref, kr_ref):
    x = h_ref[...]
    y = x * lax.rsqrt(jnp.mean(x * x, axis=-1, keepdims=True) + EPS)
    cos = cos_ref[...]
    sin = sin_ref[...]
    hn = (y * gkv_ref[...]).astype(BF16)
    ckv = _rms(_dot(hn, wdkv_ref[...]), glat_ref[...])
    kr2 = _dot(hn, wkr_ref[...])
    kr = kr2[:, :ROPE_PAD] * cos + kr2[:, ROPE_PAD:] * sin
    ckv_ref[...] = ckv
    kr_ref[...] = kr
    kvcat_ref[...] = jnp.concatenate([ckv, kr], axis=1).astype(BF16)
    xn = (y * gmix_ref[...]).astype(BF16)
    cq = _rms(_dot(xn, wdq_ref[...]), qn_ref[...]).astype(BF16)
    q = _dot(cq, wuq_ref[...])
    nope_w = MLA_HEADS * QK_NOPE
    rope_w = MLA_HEADS * ROPE_PAD
    qr = (q[:, nope_w:nope_w + rope_w] * _lane_tile(cos, rope_w)
          + q[:, nope_w + rope_w:] * _lane_tile(sin, rope_w))
    qrope_ref[...] = (qr * MLA_Q_SCALE).astype(qrope_ref.dtype)
    for hh in range(MLA_HEADS):
        qn_h = q[:, hh * QK_NOPE:(hh + 1) * QK_NOPE].astype(BF16)
        ql = _dot(qn_h, ukt_ref[hh])
        qlat_ref[:, hh * KV_LORA:(hh + 1) * KV_LORA] = (ql * MLA_Q_SCALE).astype(qlat_ref.dtype)


def _mla_pre(h, cos, sin, gmix, gkv, wdq, qn, wuq, ukt, wdkv, glat, wkr, q_dtype, name):
    rows = h.shape[0]
    tm = 512
    period = cos.shape[0] // tm
    row = lambda w: pl.BlockSpec((tm, w), lambda i: (i, 0))
    tab = pl.BlockSpec((tm, ROPE_PAD), lambda i: (i % period, 0))
    full = lambda a: pl.BlockSpec(a.shape, lambda i: (0,) * a.ndim)
    return pl.pallas_call(
        _mla_pre_kernel,
        grid=(rows // tm,),
        in_specs=[row(D_MODEL), tab, tab, full(gmix), full(gkv), full(wdq), full(qn),
                  full(wuq), full(ukt), full(wdkv), full(glat), full(wkr)],
        out_specs=(row(MLA_HEADS * KV_LORA), row(MLA_HEADS * ROPE_PAD), row(KV_LORA + ROPE_PAD),
                   row(KV_LORA), row(ROPE_PAD)),
        out_shape=(jax.ShapeDtypeStruct((rows, MLA_HEADS * KV_LORA), q_dtype),
                   jax.ShapeDtypeStruct((rows, MLA_HEADS * ROPE_PAD), q_dtype),
                   jax.ShapeDtypeStruct((rows, KV_LORA + ROPE_PAD), BF16),
                   jax.ShapeDtypeStruct((rows, KV_LORA), F32),
                   jax.ShapeDtypeStruct((rows, ROPE_PAD), F32)),
        compiler_params=_params("parallel"),
        name=name,
    )(h, cos, sin, gmix, gkv, wdq, qn, wuq, ukt, wdkv, glat, wkr)


def _mla_attn_prompt_kernel(qlat_ref, qrope_ref, kv_ref, o_ref, m_ref, l_ref, acc_ref, *, tq):
    qi = pl.program_id(1)
    m_ref[...] = jnp.full_like(m_ref, NEG_BIG)
    l_ref[...] = jnp.zeros_like(l_ref)
    acc_ref[...] = jnp.zeros_like(acc_ref)

    def block(j, r0, nr, nk, diagonal):
        k0 = pl.multiple_of(j * tq, tq)
        rws = slice(r0, r0 + nr)
        ck = kv_ref[pl.ds(k0, nk), :KV_LORA]
        kr = kv_ref[pl.ds(k0, nk), KV_LORA:]
        for hh in range(MLA_HEADS):
            s = (_dot_t(qlat_ref[rws, hh * KV_LORA:(hh + 1) * KV_LORA], ck)
                 + _dot_t(qrope_ref[rws, hh * ROPE_PAD:(hh + 1) * ROPE_PAD], kr))
            if diagonal:
                visible = (lax.broadcasted_iota(jnp.int32, (nr, nk), 1)
                           <= lax.broadcasted_iota(jnp.int32, (nr, nk), 0) + r0)
                s = jnp.where(visible, s, NEG_BIG)
            m_old = m_ref[hh, rws]
            m_new = jnp.maximum(m_old, jnp.max(s, axis=-1, keepdims=True))
            alpha = jnp.exp2(m_old - m_new)
            p = jnp.exp2(s - _lane_tile(m_new, nk))
            l_ref[hh, rws] = alpha * l_ref[hh, rws] + jnp.sum(p, axis=-1, keepdims=True)
            acc_ref[hh, rws] = _lane_tile(alpha, KV_LORA) * acc_ref[hh, rws] + _dot(p.astype(BF16), ck)
            m_ref[hh, rws] = m_new

    def body(j, _):
        block(j, 0, tq, tq, False)
        return 0

    lax.fori_loop(0, qi, body, 0)
    block(qi, 0, tq, tq, True)
    for hh in range(MLA_HEADS):
        o = acc_ref[hh] / _lane_tile(l_ref[hh], KV_LORA)
        o_ref[:, hh * KV_LORA:(hh + 1) * KV_LORA] = o.astype(BF16)


def _mla_attn_prompt(qlat, qrope, kvcat, t_len, b_all):
    tq = 512
    nq = t_len // tq
    lw, rw, kw = MLA_HEADS * KV_LORA, MLA_HEADS * ROPE_PAD, KV_LORA + ROPE_PAD
    return pl.pallas_call(
        functools.partial(_mla_attn_prompt_kernel, tq=tq),
        grid=(b_all, nq),
        in_specs=[pl.BlockSpec((tq, lw), lambda b, i: (b * nq + i, 0)),
                  pl.BlockSpec((tq, rw), lambda b, i: (b * nq + i, 0)),
                  pl.BlockSpec((t_len, kw), lambda b, i: (b, 0))],
        out_specs=pl.BlockSpec((tq, lw), lambda b, i: (b * nq + i, 0)),
        out_shape=jax.ShapeDtypeStruct((b_all * t_len, lw), BF16),
        scratch_shapes=[pltpu.VMEM((MLA_HEADS, tq, LANES), F32), pltpu.VMEM((MLA_HEADS, tq, LANES), F32),
                        pltpu.VMEM((MLA_HEADS, tq, KV_LORA), F32)],
        compiler_params=_params("parallel", "parallel"),
        name="mla_attn_prompt",
    )(qlat, qrope, kvcat)


def _stack_heads(ref, width):
    return jnp.concatenate([ref[:, hh * width:(hh + 1) * width] for hh in range(MLA_HEADS)], axis=0)


def _mla_attn_sample_kernel(pt_ref, qlat_ref, qrope_ref, ckvn_ref, krn_ref, kv_hbm, kr_hbm,
                            o_ref, kvbuf, krbuf, sem, s_ref, kvb_ref, *, ts, n_pages, parts):
    slots = kvbuf.shape[0]
    ahead = slots - 1
    b = pl.program_id(0)
    slot = b % slots
    rows = MLA_HEADS * ts
    past = n_pages * PAGE_SIZE

    def page_copies(batch, sl, page_of):
        cps = []
        for p in range(n_pages):
            page = page_of(batch, p)
            cps.append(pltpu.make_async_copy(kv_hbm.at[page], kvbuf.at[sl, p], sem.at[0, sl]))
            cps.append(pltpu.make_async_copy(kr_hbm.at[page], krbuf.at[sl, p], sem.at[1, sl]))
        return cps

    def start_all(cps):
        for i, cp in enumerate(cps):
            cp.start(priority=(i // 2) % 2)

    table_page = lambda batch, p: pt_ref[batch * n_pages + p]
    last = pl.num_programs(0) - 1

    any_page = lambda batch, p: 0

    @pl.when(b == 0)
    def _():
        for d in range(ahead):
            start_all(page_copies(jnp.minimum(d, last), d, table_page))

    for cp in page_copies(b, slot, any_page):
        cp.wait()

    start_all(page_copies(jnp.minimum(b + ahead, last), (b + ahead) % slots, table_page))

    ql = _stack_heads(qlat_ref, KV_LORA).astype(BF16)
    qr = _stack_heads(qrope_ref, ROPE_PAD).astype(BF16)
    qr_cache = qr[:, :QK_ROPE]
    pair = 2 * PAGE_SIZE
    pairs_per_part = n_pages // 2 // parts
    span = pairs_per_part * pair
    ms, ls, os = [], [], []
    for part in range(parts):
        for g in range(part * pairs_per_part, (part + 1) * pairs_per_part):
            ck = kvbuf[slot, 2 * g:2 * g + 2].reshape(pair, KV_LORA).astype(BF16)
            kr = jnp.concatenate([krbuf[slot, 2 * g], krbuf[slot, 2 * g + 1]], axis=1).astype(BF16)
            s_ref[:, g * pair:(g + 1) * pair] = _dot_t(ql, ck) + _dot(qr_cache, kr)
            kvb_ref[g * pair:(g + 1) * pair, :] = ck
        hi = (part + 1) * span
        if part == parts - 1:
            pad = PAGE_SIZE - ts
            ckn = jnp.concatenate([ckvn_ref[...], jnp.zeros((pad, KV_LORA), F32)], axis=0).astype(BF16)
            krn = jnp.concatenate([krn_ref[...], jnp.zeros((pad, ROPE_PAD), F32)], axis=0).astype(BF16)
            s_new = _dot_t(ql, ckn) + _dot_t(qr, krn)
            t_q = lax.broadcasted_iota(jnp.int32, (rows, PAGE_SIZE), 0) & (ts - 1)
            t_k = lax.broadcasted_iota(jnp.int32, (rows, PAGE_SIZE), 1)
            s_ref[:, past:] = jnp.where(t_k <= t_q, s_new, NEG_BIG)
            kvb_ref[past:, :] = ckn
            hi = past + PAGE_SIZE
        s = s_ref[:, part * span:hi]
        m = jnp.max(s, axis=-1, keepdims=True)
        p = jnp.exp2(s - m)
        ms.append(m)
        ls.append(jnp.sum(p, axis=-1, keepdims=True))
        os.append(_dot(p.astype(BF16), kvb_ref[part * span:hi, :]))
    m_all = functools.reduce(jnp.maximum, ms)
    ws = [jnp.exp2(m - m_all) for m in ms]
    o = sum(w * o_part for w, o_part in zip(ws, os)) / sum(w * l for w, l in zip(ws, ls))
    for hh in range(MLA_HEADS):
        o_ref[:, hh * KV_LORA:(hh + 1) * KV_LORA] = o[hh * ts:(hh + 1) * ts]

    @pl.when(b == last)
    def _():
        for d in range(1, slots):
            for cp in page_copies(b, (b + d) % slots, any_page):
                cp.wait()


def _mla_attn_sample(qlat, qrope, ckv_new, kr_new, cache_kv, cache_kr_t, page_table, ts, b_all):
    slots = 3
    n_pages = page_table.shape[1]
    lw, rw = MLA_HEADS * KV_LORA, MLA_HEADS * ROPE_PAD
    rows = MLA_HEADS * ts
    keys = (n_pages + 1) * PAGE_SIZE
    pt = page_table.reshape(-1)
    stream = lambda w: pl.BlockSpec((ts, w), lambda b, pt_ref: (b, 0))
    hbm = pl.BlockSpec(memory_space=pl.ANY)
    grid_spec = pltpu.PrefetchScalarGridSpec(
        num_scalar_prefetch=1,
        grid=(b_all,),
        in_specs=[stream(lw), stream(rw), stream(KV_LORA), stream(ROPE_PAD), hbm, hbm],
        out_specs=stream(lw),
        scratch_shapes=[pltpu.VMEM((slots, n_pages, PAGE_SIZE, KV_LORA), F32),
                        pltpu.VMEM((slots, n_pages, QK_ROPE, PAGE_SIZE), F32),
                        pltpu.SemaphoreType.DMA((2, slots)),
                        pltpu.VMEM((rows, keys), F32),
                        pltpu.VMEM((keys, KV_LORA), BF16)],
    )
    return pl.pallas_call(
        functools.partial(_mla_attn_sample_kernel, ts=ts, n_pages=n_pages, parts=4),
        grid_spec=grid_spec,
        out_shape=jax.ShapeDtypeStruct((b_all * ts, lw), F32),
        compiler_params=_params("arbitrary"),
        name="mla_attn_sample",
    )(pt, qlat, qrope, ckv_new, kr_new, cache_kv, cache_kr_t)


def _mla_out_kernel(o_ref, wuv_ref, wo_ref, g_ref, h_ref, out_ref):
    for rws in _row_halves(h_ref.shape[0]):
        vs = []
        for hh in range(MLA_HEADS):
            vs.append(_dot(o_ref[rws, hh * KV_LORA:(hh + 1) * KV_LORA].astype(BF16), wuv_ref[hh]).astype(BF16))
        y = _dot(jnp.concatenate(vs, axis=1), wo_ref[...])
        out_ref[rws, :] = h_ref[rws, :] + _rms(y, g_ref[...])


def _mla_out(o, wuv, wo, g, h, name):
    rows = h.shape[0]
    tm = 1024
    lw = MLA_HEADS * KV_LORA
    return pl.pallas_call(
        _mla_out_kernel,
        grid=(rows // tm,),
        in_specs=[pl.BlockSpec((tm, lw), lambda i: (i, 0)),
                  pl.BlockSpec(wuv.shape, lambda i: (0, 0, 0)),
                  pl.BlockSpec(wo.shape, lambda i: (0, 0)),
                  pl.BlockSpec((1, D_MODEL), lambda i: (0, 0)),
                  pl.BlockSpec((tm, D_MODEL), lambda i: (i, 0))],
        out_specs=pl.BlockSpec((tm, D_MODEL), lambda i: (i, 0)),
        out_shape=jax.ShapeDtypeStruct((rows, D_MODEL), F32),
        compiler_params=_params("parallel"),
        name=name,
    )(o, wuv, wo, g, h)


def _rope_tables(pos):
    half = QK_ROPE // 2
    inv = ROPE_BASE ** (-jnp.arange(half, dtype=F32) / half)
    ang = pos.astype(F32)[:, None] * inv[None, :]
    cos = jnp.cos(ang)
    sin = jnp.sin(ang)
    zeros = jnp.zeros((pos.shape[0], ROPE_PAD - QK_ROPE), F32)
    return jnp.concatenate([cos, cos, zeros], axis=1), jnp.concatenate([sin, sin, zeros], axis=1)


def _rot_cols(w):
    half = QK_ROPE // 2
    return jnp.concatenate([-w[..., half:], w[..., :half]], axis=-1)


def _pad_rope(w):
    pad = [(0, 0)] * (w.ndim - 1) + [(0, ROPE_PAD - QK_ROPE)]
    return jnp.pad(w, pad)


def kernel(x_prompt, x_sample, cache_ssm_re, cache_ssm_im, cache_kv_latent, cache_k_rope, cache_mem_k, cache_mem_v, page_table, mem_prompt, norm_mix_pre, norm_mix_post, norm_mem_pre, norm_mem_post, norm_mlp_pre, norm_mlp_post, mem_in_norm, w_mem_q, w_mem_k, w_mem_v, w_mem_o, w_mlp_up, w_mlp_down, ssm_a_re, ssm_a_im, ssm_log_dt, ssm_b_re, ssm_b_im, ssm_c_re, ssm_c_im, ssm_d, w_glu, kv_in_norm, w_dkv, kv_latent_norm, w_kr, w_uk, w_uv, w_dq, q_norm, w_uq, w_o):
    depth = norm_mix_pre.shape[0]
    n_a = ssm_a_re.shape[0]
    bp, sp, _ = x_prompt.shape
    bs, ts, _ = x_sample.shape
    past_len = page_table.shape[1] * PAGE_SIZE

    hp = x_prompt.astype(F32).reshape(bp * sp, D_MODEL)
    hs = x_sample.astype(F32).reshape(bs * ts, D_MODEL)

    mem_rows = mem_prompt.astype(F32).reshape(bp * MEM_TOKENS, D_MODEL)
    mk, mv, mkb, mvb = _mem_kv(mem_rows, mem_in_norm.astype(F32).reshape(depth, 1, D_MODEL),
                               w_mem_k.astype(BF16), w_mem_v.astype(BF16))
    cmk = _mem_cache_view(cache_mem_k.astype(F32))
    cmv = _mem_cache_view(cache_mem_v.astype(F32))

    ssm_p_re, ssm_p_im, ssm_s_re, ssm_s_im = [], [], [], []
    ckv_p = kr_p = ckv_s = kr_s = None
    for i in range(depth):
        if i < n_a:
            abar_re, abar_im, bbar_re, bbar_im = _ssm_discretise(
                ssm_a_re[i], ssm_a_im[i], ssm_log_dt[i], ssm_b_re[i], ssm_b_im[i])
            ng = SSM_GROUPS // SSM_KBLOCKS
            bdre = _block_diag(bbar_re.reshape(SSM_KBLOCKS, ng, SSM_GROUP, SSM_STATE)).astype(BF16)
            bdim = _block_diag(bbar_im.reshape(SSM_KBLOCKS, ng, SSM_GROUP, SSM_STATE)).astype(BF16)
            c_t = lambda c: c.astype(F32).reshape(SSM_KBLOCKS, ng, SSM_GROUP, SSM_STATE).transpose(0, 1, 3, 2)
            cre = _block_diag(c_t(ssm_c_re[i])).astype(BF16)
            ncim = _block_diag(-c_t(ssm_c_im[i])).astype(BF16)
            s5_w = (_row(norm_mix_pre[i]), bdre, bdim, abar_re.reshape(1, SSM_WIDTH), abar_im.reshape(1, SSM_WIDTH))
            s5_w2 = (cre, ncim, _row(ssm_d[i]), w_glu[i].astype(BF16), _row(norm_mix_post[i]))
            zeros = jnp.zeros((bp, SSM_WIDTH), F32)
            hp3, fpr, fpi = _s5_layer(hp.reshape(bp, sp, D_MODEL), 32, 2, bp, *s5_w, zeros, zeros, *s5_w2,
                                      name="s5_prompt")
            hs3, fsr, fsi = _s5_layer(hs.reshape(bs, ts, D_MODEL), ts, 1, 32, *s5_w,
                                      cache_ssm_re[i].astype(F32).reshape(bs, SSM_WIDTH),
                                      cache_ssm_im[i].astype(F32).reshape(bs, SSM_WIDTH), *s5_w2, name="s5_sample")
            hp = hp3.reshape(bp * sp, D_MODEL)
            hs = hs3.reshape(bs * ts, D_MODEL)
            ssm_p_re.append(fpr.reshape(bp, SSM_GROUPS, SSM_STATE))
            ssm_p_im.append(fpi.reshape(bp, SSM_GROUPS, SSM_STATE))
            ssm_s_re.append(fsr.reshape(bs, SSM_GROUPS, SSM_STATE))
            ssm_s_im.append(fsi.reshape(bs, SSM_GROUPS, SSM_STATE))
        else:
            j = i - n_a
            wuq = w_uq[j].astype(F32)
            w_nope = wuq[:, :, :QK_NOPE].reshape(Q_LORA, MLA_HEADS * QK_NOPE)
            w_r = wuq[:, :, QK_NOPE:]
            w_rope = _pad_rope(w_r).reshape(Q_LORA, MLA_HEADS * ROPE_PAD)
            w_rope_rot = _pad_rope(_rot_cols(w_r)).reshape(Q_LORA, MLA_HEADS * ROPE_PAD)
            wuq_all = jnp.concatenate([w_nope, w_rope, w_rope_rot], axis=1).astype(BF16)
            ukt = w_uk.astype(F32).transpose(1, 2, 0).astype(BF16)
            wkr = w_kr.astype(F32)
            wkr2 = jnp.concatenate([_pad_rope(wkr), _pad_rope(_rot_cols(wkr))], axis=1).astype(BF16)
            pre_w = (_row(norm_mix_pre[i]), _row(kv_in_norm), w_dq[j].astype(BF16), _row(q_norm[j]), wuq_all, ukt,
                     w_dkv.astype(BF16), _row(kv_latent_norm), wkr2)
            cos_p, sin_p = _rope_tables(jnp.arange(sp, dtype=jnp.int32))
            cos_s, sin_s = _rope_tables(past_len + jnp.tile(jnp.arange(ts, dtype=jnp.int32), 512 // ts))
            qlat_p, qrope_p, kvcat_p, ckv_pj, kr_pj = _mla_pre(hp, cos_p, sin_p, *pre_w, BF16, name="mla_pre_prompt")
            qlat_s, qrope_s, _, ckv_sj, kr_sj = _mla_pre(hs, cos_s, sin_s, *pre_w, F32, name="mla_pre_sample")
            if i == n_a:
                ckv_p, kr_p, ckv_s, kr_s, kvcat_shared = ckv_pj, kr_pj, ckv_sj, kr_sj, kvcat_p
            o_p = _mla_attn_prompt(qlat_p, qrope_p, kvcat_shared, sp, bp)
            o_s = _mla_attn_sample(qlat_s, qrope_s, ckv_s, kr_s, cache_kv_latent.astype(F32),
                                   jnp.swapaxes(cache_k_rope.astype(F32), 1, 2), page_table, ts, bs)
            wuv = w_uv.astype(F32).transpose(1, 0, 2).astype(BF16)
            wo = w_o[j].astype(BF16)
            hp = _mla_out(o_p, wuv, wo, _row(norm_mix_post[i]), hp, name="mla_out_prompt")
            hs = _mla_out(o_s, wuv, wo, _row(norm_mix_post[i]), hs, name="mla_out_sample")

        wq = w_mem_q[i].astype(BF16)
        wo_mem = w_mem_o[i].astype(BF16)
        hp = _mem_attn_prompt(hp, sp, bp, i, _row(norm_mem_pre[i]), wq, mkb, mvb, wo_mem, _row(norm_mem_post[i]))
        q_s = _norm_matmul(hs, _row(norm_mem_pre[i]), wq, MEM_SCALE, F32, name="mem_q_sample")
        o_s = _mem_attn_sample(q_s, ts, bs, i, cmk, cmv)
        hs = _matmul_norm_res(o_s, wo_mem, _row(norm_mem_post[i]), hs, name="mem_out_sample")

        wup = w_mlp_up[i].astype(BF16)
        wdn = w_mlp_down[i].astype(BF16)
        hp = _mlp(hp, _row(norm_mlp_pre[i]), wup, wdn, _row(norm_mlp_post[i]), name="mlp_prompt")
        hs = _mlp(hs, _row(norm_mlp_pre[i]), wup, wdn, _row(norm_mlp_post[i]), name="mlp_sample")

    return (hp.reshape(bp, sp, D_MODEL), hs.reshape(bs, ts, D_MODEL),
            jnp.stack(ssm_p_re), jnp.stack(ssm_p_im), jnp.stack(ssm_s_re), jnp.stack(ssm_s_im),
            ckv_p.reshape(bp, sp, KV_LORA), kr_p[:, :QK_ROPE].reshape(bp, sp, QK_ROPE),
            ckv_s.reshape(bs, ts, KV_LORA), kr_s[:, :QK_ROPE].reshape(bs, ts, QK_ROPE),
            _mem_cache_unview(mk, bp), _mem_cache_unview(mv, bp))
```

```python
import functools
import math

import jax
import jax.numpy as jnp
from jax import lax
from jax.experimental import pallas as pl
from jax.experimental.pallas import tpu as pltpu

F32 = jnp.float32
BF16 = jnp.bfloat16

LANES = 128
SUBLANES = 8
D_MODEL = 1024
PAGE_SIZE = 128
SSM_GROUP = 16
SSM_GROUPS = D_MODEL // SSM_GROUP
SSM_STATE = 64
SSM_WIDTH = SSM_GROUPS * SSM_STATE
SSM_KBLOCKS = 4
SSM_SLABS = SSM_WIDTH // LANES
MLA_HEADS = 8
QK_NOPE = 128
QK_ROPE = 64
ROPE_PAD = LANES
V_HEAD = 128
KV_LORA = 256
Q_LORA = 384
ROPE_BASE = 10000.0
MLA_SCALE = (QK_NOPE + QK_ROPE) ** -0.5
MLA_Q_SCALE = MLA_SCALE * math.log2(math.e)
MEM_TOKENS = 256
MEM_HEADS = 4
MEM_HEAD_DIM = D_MODEL // MEM_HEADS
MEM_SCALE = MEM_HEAD_DIM ** -0.5
D_FF = 4 * D_MODEL
EPS = 1e-6
NEG_BIG = -1e30

VMEM_LIMIT_BYTES = 56 * 1024 * 1024


def _params(*semantics):
    return pltpu.CompilerParams(dimension_semantics=semantics, vmem_limit_bytes=VMEM_LIMIT_BYTES)


def _dot(a, b):
    return jnp.dot(a, b, preferred_element_type=F32)


def _dot_t(a, b):
    return lax.dot_general(a, b, (((1,), (1,)), ((), ())), preferred_element_type=F32)


def _rms(x, g):
    return x * lax.rsqrt(jnp.mean(x * x, axis=-1, keepdims=True) + EPS) * g


def _row(v):
    return v.reshape(1, -1).astype(F32)


def _lane_tile(x, width):
    return jnp.concatenate([x] * (width // LANES), axis=1) if width > LANES else x


def _ssm_disc_kernel(ar_ref, ai_ref, ldt_ref, br_ref, bi_ref,
                     abar_re_ref, abar_im_ref, bbar_re_ref, bbar_im_ref):
    ar = ar_ref[...]
    ai = ai_ref[...]
    dt = jnp.exp(ldt_ref[...])
    mag = jnp.exp(ar * dt)
    abar_re = mag * jnp.cos(ai * dt)
    abar_im = mag * jnp.sin(ai * dt)
    den = ar * ar + ai * ai
    nr = abar_re - 1.0
    ni = abar_im
    coef_re = (nr * ar + ni * ai) / den
    coef_im = (ni * ar - nr * ai) / den
    br = br_ref[...]
    bi = bi_ref[...]
    abar_re_ref[...] = abar_re
    abar_im_ref[...] = abar_im
    bbar_re_ref[...] = coef_re * br - coef_im * bi
    bbar_im_ref[...] = coef_re * bi + coef_im * br


def _ssm_discretise(a_re, a_im, log_dt, b_re, b_im):
    g, n, k = SSM_GROUPS, SSM_STATE, SSM_GROUP
    rep = lambda a: jnp.repeat(a.astype(F32), k, axis=0)
    brt = b_re.astype(F32).transpose(0, 2, 1).reshape(g * k, n)
    bit = b_im.astype(F32).transpose(0, 2, 1).reshape(g * k, n)
    ldt = jnp.repeat(log_dt.astype(F32), k, axis=0).reshape(g * k, 1)
    shp = jax.ShapeDtypeStruct((g * k, n), F32)
    abar_re, abar_im, bbar_re, bbar_im = pl.pallas_call(
        _ssm_disc_kernel, out_shape=(shp, shp, shp, shp), name="ssm_discretise",
    )(rep(a_re), rep(a_im), ldt, brt, bit)
    return abar_re[::k], abar_im[::k], bbar_re, bbar_im


def _block_diag(x):
    kb, ng, r, c = x.shape
    eye = jnp.eye(ng, dtype=x.dtype)
    return (x[:, :, :, None, :] * eye[None, :, None, :, None]).reshape(kb, ng * r, ng * c)


def _s5_kernel(x_ref, gpre_ref, bdre_ref, bdim_ref, are_ref, aim_ref, h0re_ref, h0im_ref,
               cre_ref, ncim_ref, dskip_ref, wglu_ref, gpost_ref,
               out_ref, fre_ref, fim_ref, *scratch, tt, bb, nsub, xpitch, rpitch):
    ti = pl.program_id(1)

    @pl.when(ti == 0)
    def _():
        fre_ref[...] = h0re_ref[...]
        fim_ref[...] = h0im_ref[...]

    for sub in range(nsub):
        _s5_subtile(x_ref, gpre_ref, bdre_ref, bdim_ref, are_ref, aim_ref, cre_ref, ncim_ref, dskip_ref,
                    wglu_ref, gpost_ref, out_ref, fre_ref, fim_ref, *scratch[4 * sub:4 * sub + 4],
                    t0=sub * tt, tt=tt, bb=bb, xpitch=xpitch, rpitch=rpitch)


def _s5_subtile(x_ref, gpre_ref, bdre_ref, bdim_ref, are_ref, aim_ref, cre_ref, ncim_ref, dskip_ref,
                wglu_ref, gpost_ref, out_ref, fre_ref, fim_ref, xs_ref, bure_ref, buim_ref, rs_ref,
                *, t0, tt, bb, xpitch, rpitch):
    rows = tt * bb
    nsl = D_MODEL // LANES
    spk = SSM_SLABS // SSM_KBLOCKS
    steps = slice(t0, t0 + tt)

    for b in range(bb):
        for s in range(nsl):
            xs_ref[s, b * xpitch:b * xpitch + tt, :] = x_ref[b, steps, s * LANES:(s + 1) * LANES]
    x = jnp.concatenate(
        [jnp.concatenate([xs_ref[s, pl.ds(t, bb, stride=xpitch), :] for s in range(nsl)], axis=1)
         for t in range(tt)], axis=0)
    xn = _rms(x, gpre_ref[...])
    xb = xn.astype(BF16)
    kw = D_MODEL // SSM_KBLOCKS
    sw = SSM_WIDTH // SSM_KBLOCKS

    def scan(slab):
        lanes = slice(slab * LANES, (slab + 1) * LANES)
        a_re = jnp.broadcast_to(are_ref[:, lanes], (SUBLANES, LANES))
        a_im = jnp.broadcast_to(aim_ref[:, lanes], (SUBLANES, LANES))
        for vb in range(bb // SUBLANES):
            b0 = vb * SUBLANES
            h_re = fre_ref[b0:b0 + SUBLANES, lanes]
            h_im = fim_ref[b0:b0 + SUBLANES, lanes]
            for t in range(tt):
                rws = slice(t * bb + b0, t * bb + b0 + SUBLANES)
                h_re, h_im = (a_re * h_re - a_im * h_im + bure_ref[rws, lanes],
                              a_re * h_im + a_im * h_re + buim_ref[rws, lanes])
                bure_ref[rws, lanes] = h_re
                buim_ref[rws, lanes] = h_im
            fre_ref[b0:b0 + SUBLANES, lanes] = h_re
            fim_ref[b0:b0 + SUBLANES, lanes] = h_im

    ys = []
    for kb in range(SSM_KBLOCKS):
        cols = slice(kb * sw, (kb + 1) * sw)
        xk = xb[:, kb * kw:(kb + 1) * kw]
        bure_ref[:, cols] = _dot(xk, bdre_ref[kb])
        buim_ref[:, cols] = _dot(xk, bdim_ref[kb])
        for s in range(spk):
            scan(kb * spk + s)
        ys.append(_dot(bure_ref[:, cols].astype(BF16), cre_ref[kb])
                  + _dot(buim_ref[:, cols].astype(BF16), ncim_ref[kb]))
    y = jnp.concatenate(ys, axis=1) + dskip_ref[...] * xn
    g = 0.5 * y * (1.0 + lax.erf(y * (2.0 ** -0.5)))
    z = _dot(g.astype(BF16), wglu_ref[...])
    o = z[:, :D_MODEL] * (1.0 / (1.0 + jnp.exp(-z[:, D_MODEL:])))
    r = _rms(o, gpost_ref[...])
    for s in range(nsl):
        for t in range(tt):
            rs_ref[s, t * rpitch:t * rpitch + bb, :] = r[t * bb:(t + 1) * bb, s * LANES:(s + 1) * LANES]
    for b in range(bb):
        rb = jnp.concatenate([rs_ref[s, pl.ds(b, tt, stride=rpitch), :] for s in range(nsl)], axis=1)
        out_ref[b, steps, :] = x_ref[b, steps, :] + rb


def _odd_pitch(n):
    groups = -(-n // SUBLANES)
    return SUBLANES * (groups if groups % 2 else groups + 1)


def _s5_layer(x3, tt, nsub, bb, gpre, bdre, bdim, a_re, a_im, h0_re, h0_im,
              cre, ncim, dskip, wglu, gpost, name):
    b_all, t_len, _ = x3.shape
    grid = (b_all // bb, t_len // (tt * nsub))
    xpitch, rpitch = _odd_pitch(tt), _odd_pitch(bb)
    nsl = D_MODEL // LANES
    bu = pltpu.VMEM((tt * bb, SSM_WIDTH), F32)
    subtile_scratch = [pltpu.VMEM((nsl, bb * xpitch, LANES), F32), bu, bu,
                       pltpu.VMEM((nsl, tt * rpitch, LANES), F32)]
    return pl.pallas_call(
        functools.partial(_s5_kernel, tt=tt, bb=bb, nsub=nsub, xpitch=xpitch, rpitch=rpitch),
        grid=grid,
        in_specs=[
            pl.BlockSpec((bb, tt * nsub, D_MODEL), lambda bj, ti: (bj, ti, 0)),
            _resident((1, D_MODEL)),
            _resident(bdre.shape),
            _resident(bdim.shape),
            _resident((1, SSM_WIDTH)),
            _resident((1, SSM_WIDTH)),
            pl.BlockSpec((bb, SSM_WIDTH), lambda bj, ti: (bj, 0)),
            pl.BlockSpec((bb, SSM_WIDTH), lambda bj, ti: (bj, 0)),
            _resident(cre.shape),
            _resident(ncim.shape),
            _resident((1, D_MODEL)),
            _resident(wglu.shape),
            _resident((1, D_MODEL)),
        ],
        out_specs=(
            pl.BlockSpec((bb, tt * nsub, D_MODEL), lambda bj, ti: (bj, ti, 0)),
            pl.BlockSpec((bb, SSM_WIDTH), lambda bj, ti: (bj, 0)),
            pl.BlockSpec((bb, SSM_WIDTH), lambda bj, ti: (bj, 0)),
        ),
        out_shape=(
            jax.ShapeDtypeStruct((b_all, t_len, D_MODEL), F32),
            jax.ShapeDtypeStruct((b_all, SSM_WIDTH), F32),
            jax.ShapeDtypeStruct((b_all, SSM_WIDTH), F32),
        ),
        scratch_shapes=subtile_scratch * nsub,
        compiler_params=_params("parallel", "arbitrary"),
        name=name,
    )(x3, gpre, bdre, bdim, a_re, a_im, h0_re, h0_im, cre, ncim, dskip, wglu, gpost)


def _mem_kv_kernel(x_ref, g_ref, wk_ref, wv_ref, k_ref, v_ref, kb_ref, vb_ref):
    xn = _rms(x_ref[...], g_ref[0]).astype(BF16)
    k = _dot(xn, wk_ref[0])
    v = _dot(xn, wv_ref[0])
    tm = k.shape[0]
    chunks = MEM_HEAD_DIM // LANES
    period = chunks * MEM_HEADS
    for hh in range(MEM_HEADS):
        for c in range(chunks):
            lanes = slice(hh * MEM_HEAD_DIM + c * LANES, hh * MEM_HEAD_DIM + (c + 1) * LANES)
            k_ref[0, pl.ds(c * MEM_HEADS + hh, tm, stride=period), :] = k[:, lanes]
            v_ref[0, pl.ds(c * MEM_HEADS + hh, tm, stride=period), :] = v[:, lanes]
    kb_ref[0] = k.astype(BF16)
    vb_ref[0] = v.astype(BF16)


def _mem_kv(mem, g, wk, wv):
    rows = mem.shape[0]
    nl = g.shape[0]
    tm = 512
    period = D_MODEL // LANES
    blk = pl.BlockSpec((1, tm, D_MODEL), lambda l, i: (l, i, 0))
    nat = pl.BlockSpec((1, tm * period, LANES), lambda l, i: (l, i, 0))
    wspec = pl.BlockSpec((1, D_MODEL, D_MODEL), lambda l, i: (l, 0, 0))
    return pl.pallas_call(
        _mem_kv_kernel,
        grid=(nl, rows // tm),
        in_specs=[pl.BlockSpec((tm, D_MODEL), lambda l, i: (i, 0)),
                  pl.BlockSpec((1, 1, D_MODEL), lambda l, i: (l, 0, 0)), wspec, wspec],
        out_specs=(nat, nat, blk, blk),
        out_shape=(jax.ShapeDtypeStruct((nl, rows * period, LANES), F32),) * 2
        + (jax.ShapeDtypeStruct((nl, rows, D_MODEL), BF16),) * 2,
        compiler_params=_params("parallel", "parallel"),
        name="mem_kv",
    )(mem, g, wk, wv)


def _softmax_rows(s):
    m = jnp.max(s, axis=-1, keepdims=True)
    p = jnp.exp(s - m)
    return p / jnp.sum(p, axis=-1, keepdims=True)


def _row_halves(n):
    return (slice(0, n // 2), slice(n // 2, n))


def _mem_attn_prompt_kernel(h_ref, gpre_ref, wq_ref, k_ref, v_ref, wo_ref, gpost_ref, out_ref):
    for rws in _row_halves(h_ref.shape[0]):
        x = h_ref[rws, :]
        xn = _rms(x, gpre_ref[...]).astype(BF16)
        q = (_dot(xn, wq_ref[...]) * MEM_SCALE).astype(BF16)
        os = []
        for hh in range(MEM_HEADS):
            cols = slice(hh * MEM_HEAD_DIM, (hh + 1) * MEM_HEAD_DIM)
            p = _softmax_rows(_dot_t(q[:, cols], k_ref[0, :, cols]))
            os.append(_dot(p.astype(BF16), v_ref[0, :, cols]).astype(BF16))
        y = _dot(jnp.concatenate(os, axis=1), wo_ref[...])
        out_ref[rws, :] = x + _rms(y, gpost_ref[...])


def _mem_attn_prompt(h, t_len, b_all, layer, gpre, wq, kb, vb, wo, gpost):
    tq = 1024
    nq = t_len // tq
    const2 = lambda b, i: (0, 0)
    hspec = pl.BlockSpec((tq, D_MODEL), lambda b, i: (b * nq + i, 0))
    kvspec = pl.BlockSpec((1, MEM_TOKENS, D_MODEL), lambda b, i: (layer, b, 0))
    wspec = pl.BlockSpec((D_MODEL, D_MODEL), const2)
    gspec = pl.BlockSpec((1, D_MODEL), const2)
    return pl.pallas_call(
        _mem_attn_prompt_kernel,
        grid=(b_all, nq),
        in_specs=[hspec, gspec, wspec, kvspec, kvspec, wspec, gspec],
        out_specs=hspec,
        out_shape=jax.ShapeDtypeStruct(h.shape, F32),
        compiler_params=_params("parallel", "parallel"),
        name="mem_attn_prompt",
    )(h, gpre, wq, kb, vb, wo, gpost)


def _mem_cache_view(cache):
    nl, nb, m, nh, dh = cache.shape
    chunks = dh // LANES
    return cache.reshape(nl, nb, m, nh, chunks, LANES).transpose(0, 1, 2, 4, 3, 5).reshape(
        nl, nb, m * chunks * nh, LANES)


def _mem_cache_unview(x, nb):
    nl = x.shape[0]
    chunks = MEM_HEAD_DIM // LANES
    return x.reshape(nl, nb, MEM_TOKENS, chunks, MEM_HEADS, LANES).transpose(0, 1, 2, 4, 3, 5).reshape(
        nl, nb, MEM_TOKENS, MEM_HEADS, MEM_HEAD_DIM)


def _mem_attn_sample_kernel(q_ref, k_ref, v_ref, o_ref, *, nb, ts):
    chunks = MEM_HEAD_DIM // LANES
    period = chunks * MEM_HEADS

    def head(ref, g, hh):
        parts = [ref[0, g, pl.ds(c * MEM_HEADS + hh, MEM_TOKENS, stride=period), :] for c in range(chunks)]
        return jnp.concatenate(parts, axis=1).astype(BF16)

    pairs = [(g, hh) for g in range(nb) for hh in range(MEM_HEADS)]
    ss = [_dot_t(q_ref[g * ts:(g + 1) * ts, hh * MEM_HEAD_DIM:(hh + 1) * MEM_HEAD_DIM].astype(BF16),
                 head(k_ref, g, hh)) for g, hh in pairs]
    p = _softmax_rows(jnp.concatenate(ss, axis=0))
    for i, (g, hh) in enumerate(pairs):
        o_ref[g * ts:(g + 1) * ts, hh * MEM_HEAD_DIM:(hh + 1) * MEM_HEAD_DIM] = _dot(
            p[i * ts:(i + 1) * ts].astype(BF16), head(v_ref, g, hh))


def _mem_attn_sample(q, ts, b_all, layer, cache_k, cache_v):
    nb = 4
    qspec = pl.BlockSpec((nb * ts, D_MODEL), lambda j: (j, 0))
    cspec = pl.BlockSpec((1, nb) + cache_k.shape[2:], lambda j: (layer, j, 0, 0))
    return pl.pallas_call(
        functools.partial(_mem_attn_sample_kernel, nb=nb, ts=ts),
        grid=(b_all // nb,),
        in_specs=[qspec, cspec, cspec],
        out_specs=qspec,
        out_shape=jax.ShapeDtypeStruct(q.shape, F32),
        compiler_params=_params("parallel"),
        name="mem_attn_sample",
    )(q, cache_k, cache_v)


def _norm_matmul_kernel(x_ref, g_ref, w_ref, o_ref, *, scale):
    xn = _rms(x_ref[...], g_ref[...]).astype(BF16)
    o_ref[...] = (_dot(xn, w_ref[...]) * scale).astype(o_ref.dtype)


def _norm_matmul(x, g, w, scale, out_dtype, name):
    rows = x.shape[0]
    tm = 512
    n = w.shape[1]
    return pl.pallas_call(
        functools.partial(_norm_matmul_kernel, scale=scale),
        grid=(rows // tm,),
        in_specs=[pl.BlockSpec((tm, D_MODEL), lambda i: (i, 0)),
                  pl.BlockSpec((1, D_MODEL), lambda i: (0, 0)),
                  pl.BlockSpec(w.shape, lambda i: (0, 0))],
        out_specs=pl.BlockSpec((tm, n), lambda i: (i, 0)),
        out_shape=jax.ShapeDtypeStruct((rows, n), out_dtype),
        compiler_params=_params("parallel"),
        name=name,
    )(x, g, w)


def _matmul_norm_res_kernel(a_ref, w_ref, g_ref, h_ref, o_ref):
    y = _dot(a_ref[...].astype(BF16), w_ref[...])
    o_ref[...] = h_ref[...] + _rms(y, g_ref[...])


def _matmul_norm_res(a, w, g, h, name):
    rows, k = a.shape
    tm = 512
    return pl.pallas_call(
        _matmul_norm_res_kernel,
        grid=(rows // tm,),
        in_specs=[pl.BlockSpec((tm, k), lambda i: (i, 0)),
                  pl.BlockSpec(w.shape, lambda i: (0, 0)),
                  pl.BlockSpec((1, D_MODEL), lambda i: (0, 0)),
                  pl.BlockSpec((tm, D_MODEL), lambda i: (i, 0))],
        out_specs=pl.BlockSpec((tm, D_MODEL), lambda i: (i, 0)),
        out_shape=jax.ShapeDtypeStruct((rows, D_MODEL), F32),
        compiler_params=_params("parallel"),
        name=name,
    )(a, w, g, h)


def _mlp_kernel(x_ref, gpre_ref, wup_ref, wdn_ref, gpost_ref, o_ref, *, tf):
    x = x_ref[...]
    xn = _rms(x, gpre_ref[...]).astype(BF16)
    acc = None
    for c in range(D_FF // tf):
        hid = jnp.maximum(_dot(xn, wup_ref[:, c * tf:(c + 1) * tf]), 0.0)
        part = _dot((hid * hid).astype(BF16), wdn_ref[c * tf:(c + 1) * tf, :])
        acc = part if acc is None else acc + part
    o_ref[...] = x + _rms(acc, gpost_ref[...])


def _resident(shape):
    return pl.BlockSpec(shape, lambda *_: (0,) * len(shape), pipeline_mode=pl.Buffered(1))


def _mlp(h, gpre, wup, wdn, gpost, name):
    rows = h.shape[0]
    tm, tf = 512, 1024
    return pl.pallas_call(
        functools.partial(_mlp_kernel, tf=tf),
        grid=(rows // tm,),
        in_specs=[pl.BlockSpec((tm, D_MODEL), lambda i: (i, 0)),
                  _resident((1, D_MODEL)), _resident(wup.shape), _resident(wdn.shape), _resident((1, D_MODEL))],
        out_specs=pl.BlockSpec((tm, D_MODEL), lambda i: (i, 0)),
        out_shape=jax.ShapeDtypeStruct((rows, D_MODEL), F32),
        compiler_params=_params("parallel"),
        name=name,
    )(h, gpre, wup, wdn, gpost)


def _mla_pre_kernel(h_ref, cos_ref, sin_ref, gmix_ref, gkv_ref, wdq_ref, qn_ref, wuq_ref, ukt_ref,
                    wdkv_ref, glat_ref, wkr_ref,
                    qlat_ref, qrope_ref, kvcat_ref, ckv_ref, kr_ref):
    x = h_ref[...]
    y = x * lax.rsqrt(jnp.mean(x * x, axis=-1, keepdims=True) + EPS)
    cos = cos_ref[...]
    sin = sin_ref[...]
    hn = (y * gkv_ref[...]).astype(BF16)
    ckv = _rms(_dot(hn, wdkv_ref[...]), glat_ref[...])
    kr2 = _dot(hn, wkr_ref[...])
    kr = kr2[:, :ROPE_PAD] * cos + kr2[:, ROPE_PAD:] * sin
    ckv_ref[...] = ckv
    kr_ref[...] = kr
    kvcat_ref[...] = jnp.concatenate([ckv, kr], axis=1).astype(BF16)
    xn = (y * gmix_ref[...]).astype(BF16)
    cq = _rms(_dot(xn, wdq_ref[...]), qn_ref[...]).astype(BF16)
    q = _dot(cq, wuq_ref[...])
    nope_w = MLA_HEADS * QK_NOPE
    rope_w = MLA_HEADS * ROPE_PAD
    qr = (q[:, nope_w:nope_w + rope_w] * _lane_tile(cos, rope_w)
          + q[:, nope_w + rope_w:] * _lane_tile(sin, rope_w))
    qrope_ref[...] = (qr * MLA_Q_SCALE).astype(qrope_ref.dtype)
    for hh in range(MLA_HEADS):
        qn_h = q[:, hh * QK_NOPE:(hh + 1) * QK_NOPE].astype(BF16)
        ql = _dot(qn_h, ukt_ref[hh])
        qlat_ref[:, hh * KV_LORA:(hh + 1) * KV_LORA] = (ql * MLA_Q_SCALE).astype(qlat_ref.dtype)


def _mla_pre(h, cos, sin, gmix, gkv, wdq, qn, wuq, ukt, wdkv, glat, wkr, q_dtype, name):
    rows = h.shape[0]
    tm = 512
    period = cos.shape[0] // tm
    row = lambda w: pl.BlockSpec((tm, w), lambda i: (i, 0))
    tab = pl.BlockSpec((tm, ROPE_PAD), lambda i: (i % period, 0))
    full = lambda a: pl.BlockSpec(a.shape, lambda i: (0,) * a.ndim)
    return pl.pallas_call(
        _mla_pre_kernel,
        grid=(rows // tm,),
        in_specs=[row(D_MODEL), tab, tab, full(gmix), full(gkv), full(wdq), full(qn),
                  full(wuq), full(ukt), full(wdkv), full(glat), full(wkr)],
        out_specs=(row(MLA_HEADS * KV_LORA), row(MLA_HEADS * ROPE_PAD), row(KV_LORA + ROPE_PAD),
                   row(KV_LORA), row(ROPE_PAD)),
        out_shape=(jax.ShapeDtypeStruct((rows, MLA_HEADS * KV_LORA), q_dtype),
                   jax.ShapeDtypeStruct((rows, MLA_HEADS * ROPE_PAD), q_dtype),
                   jax.ShapeDtypeStruct((rows, KV_LORA + ROPE_PAD), BF16),
                   jax.ShapeDtypeStruct((rows, KV_LORA), F32),
                   jax.ShapeDtypeStruct((rows, ROPE_PAD), F32)),
        compiler_params=_params("parallel"),
        name=name,
    )(h, cos, sin, gmix, gkv, wdq, qn, wuq, ukt, wdkv, glat, wkr)


def _mla_attn_prompt_kernel(qlat_ref, qrope_ref, kv_ref, o_ref, m_ref, l_ref, acc_ref, *, tq):
    qi = pl.program_id(1)
    m_ref[...] = jnp.full_like(m_ref, NEG_BIG)
    l_ref[...] = jnp.zeros_like(l_ref)
    acc_ref[...] = jnp.zeros_like(acc_ref)

    def block(j, r0, nr, nk, diagonal):
        k0 = pl.multiple_of(j * tq, tq)
        rws = slice(r0, r0 + nr)
        ck = kv_ref[pl.ds(k0, nk), :KV_LORA]
        kr = kv_ref[pl.ds(k0, nk), KV_LORA:]
        for hh in range(MLA_HEADS):
            s = (_dot_t(qlat_ref[rws, hh * KV_LORA:(hh + 1) * KV_LORA], ck)
                 + _dot_t(qrope_ref[rws, hh * ROPE_PAD:(hh + 1) * ROPE_PAD], kr))
            if diagonal:
                visible = (lax.broadcasted_iota(jnp.int32, (nr, nk), 1)
                           <= lax.broadcasted_iota(jnp.int32, (nr, nk), 0) + r0)
                s = jnp.where(visible, s, NEG_BIG)
            m_old = m_ref[hh, rws]
            m_new = jnp.maximum(m_old, jnp.max(s, axis=-1, keepdims=True))
            alpha = jnp.exp2(m_old - m_new)
            p = jnp.exp2(s - _lane_tile(m_new, nk))
            l_ref[hh, rws] = alpha * l_ref[hh, rws] + jnp.sum(p, axis=-1, keepdims=True)
            acc_ref[hh, rws] = _lane_tile(alpha, KV_LORA) * acc_ref[hh, rws] + _dot(p.astype(BF16), ck)
            m_ref[hh, rws] = m_new

    def body(j, _):
        block(j, 0, tq, tq, False)
        return 0

    lax.fori_loop(0, qi, body, 0)
    block(qi, 0, tq, tq, True)
    for hh in range(MLA_HEADS):
        o = acc_ref[hh] / _lane_tile(l_ref[hh], KV_LORA)
        o_ref[:, hh * KV_LORA:(hh + 1) * KV_LORA] = o.astype(BF16)


def _mla_attn_prompt(qlat, qrope, kvcat, t_len, b_all):
    tq = 512
    nq = t_len // tq
    lw, rw, kw = MLA_HEADS * KV_LORA, MLA_HEADS * ROPE_PAD, KV_LORA + ROPE_PAD
    return pl.pallas_call(
        functools.partial(_mla_attn_prompt_kernel, tq=tq),
        grid=(b_all, nq),
        in_specs=[pl.BlockSpec((tq, lw), lambda b, i: (b * nq + i, 0)),
                  pl.BlockSpec((tq, rw), lambda b, i: (b * nq + i, 0)),
                  pl.BlockSpec((t_len, kw), lambda b, i: (b, 0))],
        out_specs=pl.BlockSpec((tq, lw), lambda b, i: (b * nq + i, 0)),
        out_shape=jax.ShapeDtypeStruct((b_all * t_len, lw), BF16),
        scratch_shapes=[pltpu.VMEM((MLA_HEADS, tq, LANES), F32), pltpu.VMEM((MLA_HEADS, tq, LANES), F32),
                        pltpu.VMEM((MLA_HEADS, tq, KV_LORA), F32)],
        compiler_params=_params("parallel", "parallel"),
        name="mla_attn_prompt",
    )(qlat, qrope, kvcat)


def _stack_heads(ref, width):
    return jnp.concatenate([ref[:, hh * width:(hh + 1) * width] for hh in range(MLA_HEADS)], axis=0)


def _mla_attn_sample_kernel(pt_ref, qlat_ref, qrope_ref, ckvn_ref, krn_ref, kv_hbm, kr_hbm,
                            o_ref, kvbuf, krbuf, sem, s_ref, kvb_ref, *, ts, n_pages, parts):
    slots = kvbuf.shape[0]
    ahead = slots - 1
    b = pl.program_id(0)
    slot = b % slots
    rows = MLA_HEADS * ts
    past = n_pages * PAGE_SIZE

    def page_copies(batch, sl, page_of):
        cps = []
        for p in range(n_pages):
            page = page_of(batch, p)
            cps.append(pltpu.make_async_copy(kv_hbm.at[page], kvbuf.at[sl, p], sem.at[0, sl]))
            cps.append(pltpu.make_async_copy(kr_hbm.at[page], krbuf.at[sl, p], sem.at[1, sl]))
        return cps

    def start_all(cps, first=0):
        for i, cp in enumerate(cps, start=first):
            cp.start(priority=(i // 2) % 2)

    table_page = lambda batch, p: pt_ref[batch * n_pages + p]
    last = pl.num_programs(0) - 1

    any_page = lambda batch, p: 0

    @pl.when(b == 0)
    def _():
        for d in range(ahead):
            start_all(page_copies(jnp.minimum(d, last), d, table_page))

    for cp in page_copies(b, slot, any_page):
        cp.wait()

    prefetch = page_copies(jnp.minimum(b + ahead, last), (b + ahead) % slots, table_page)
    per_pair = len(prefetch) // (n_pages // 2)

    ql = _stack_heads(qlat_ref, KV_LORA).astype(BF16)
    qr = _stack_heads(qrope_ref, ROPE_PAD).astype(BF16)
    qr_cache = qr[:, :QK_ROPE]
    pair = 2 * PAGE_SIZE
    pairs_per_part = n_pages // 2 // parts
    span = pairs_per_part * pair
    ms, ls, os = [], [], []
    for part in range(parts):
        for g in range(part * pairs_per_part, (part + 1) * pairs_per_part):
            ck = kvbuf[slot, 2 * g:2 * g + 2].reshape(pair, KV_LORA).astype(BF16)
            kr = jnp.concatenate([krbuf[slot, 2 * g], krbuf[slot, 2 * g + 1]], axis=1).astype(BF16)
            s_ref[:, g * pair:(g + 1) * pair] = _dot_t(ql, ck) + _dot(qr_cache, kr)
            kvb_ref[g * pair:(g + 1) * pair, :] = ck
            start_all(prefetch[g * per_pair:(g + 1) * per_pair], first=g * per_pair)
        hi = (part + 1) * span
        if part == parts - 1:
            pad = PAGE_SIZE - ts
            ckn = jnp.concatenate([ckvn_ref[...], jnp.zeros((pad, KV_LORA), F32)], axis=0).astype(BF16)
            krn = jnp.concatenate([krn_ref[...], jnp.zeros((pad, ROPE_PAD), F32)], axis=0).astype(BF16)
            s_new = _dot_t(ql, ckn) + _dot_t(qr, krn)
            t_q = lax.broadcasted_iota(jnp.int32, (rows, PAGE_SIZE), 0) & (ts - 1)
            t_k = lax.broadcasted_iota(jnp.int32, (rows, PAGE_SIZE), 1)
            s_ref[:, past:] = jnp.where(t_k <= t_q, s_new, NEG_BIG)
            kvb_ref[past:, :] = ckn
            hi = past + PAGE_SIZE
        s = s_ref[:, part * span:hi]
        m = jnp.max(s, axis=-1, keepdims=True)
        p = jnp.exp2(s - m)
        ms.append(m)
        ls.append(jnp.sum(p, axis=-1, keepdims=True))
        os.append(_dot(p.astype(BF16), kvb_ref[part * span:hi, :]))
    m_all = functools.reduce(jnp.maximum, ms)
    ws = [jnp.exp2(m - m_all) for m in ms]
    o = sum(w * o_part for w, o_part in zip(ws, os)) / sum(w * l for w, l in zip(ws, ls))
    for hh in range(MLA_HEADS):
        o_ref[:, hh * KV_LORA:(hh + 1) * KV_LORA] = o[hh * ts:(hh + 1) * ts]

    @pl.when(b == last)
    def _():
        for d in range(1, slots):
            for cp in page_copies(b, (b + d) % slots, any_page):
                cp.wait()


def _mla_attn_sample(qlat, qrope, ckv_new, kr_new, cache_kv, cache_kr_t, page_table, ts, b_all):
    slots = 3
    n_pages = page_table.shape[1]
    lw, rw = MLA_HEADS * KV_LORA, MLA_HEADS * ROPE_PAD
    rows = MLA_HEADS * ts
    keys = (n_pages + 1) * PAGE_SIZE
    pt = page_table.reshape(-1)
    stream = lambda w: pl.BlockSpec((ts, w), lambda b, pt_ref: (b, 0))
    hbm = pl.BlockSpec(memory_space=pl.ANY)
    grid_spec = pltpu.PrefetchScalarGridSpec(
        num_scalar_prefetch=1,
        grid=(b_all,),
        in_specs=[stream(lw), stream(rw), stream(KV_LORA), stream(ROPE_PAD), hbm, hbm],
        out_specs=stream(lw),
        scratch_shapes=[pltpu.VMEM((slots, n_pages, PAGE_SIZE, KV_LORA), F32),
                        pltpu.VMEM((slots, n_pages, QK_ROPE, PAGE_SIZE), F32),
                        pltpu.SemaphoreType.DMA((2, slots)),
                        pltpu.VMEM((rows, keys), F32),
                        pltpu.VMEM((keys, KV_LORA), BF16)],
    )
    return pl.pallas_call(
        functools.partial(_mla_attn_sample_kernel, ts=ts, n_pages=n_pages, parts=4),
        grid_spec=grid_spec,
        out_shape=jax.ShapeDtypeStruct((b_all * ts, lw), F32),
        compiler_params=_params("arbitrary"),
        name="mla_attn_sample",
    )(pt, qlat, qrope, ckv_new, kr_new, cache_kv, cache_kr_t)


def _mla_out_kernel(o_ref, wuv_ref, wo_ref, g_ref, h_ref, out_ref):
    for rws in _row_halves(h_ref.shape[0]):
        vs = []
        for hh in range(MLA_HEADS):
            vs.append(_dot(o_ref[rws, hh * KV_LORA:(hh + 1) * KV_LORA].astype(BF16), wuv_ref[hh]).astype(BF16))
        y = _dot(jnp.concatenate(vs, axis=1), wo_ref[...])
        out_ref[rws, :] = h_ref[rws, :] + _rms(y, g_ref[...])


def _mla_out(o, wuv, wo, g, h, name):
    rows = h.shape[0]
    tm = 1024
    lw = MLA_HEADS * KV_LORA
    return pl.pallas_call(
        _mla_out_kernel,
        grid=(rows // tm,),
        in_specs=[pl.BlockSpec((tm, lw), lambda i: (i, 0)),
                  pl.BlockSpec(wuv.shape, lambda i: (0, 0, 0)),
                  pl.BlockSpec(wo.shape, lambda i: (0, 0)),
                  pl.BlockSpec((1, D_MODEL), lambda i: (0, 0)),
                  pl.BlockSpec((tm, D_MODEL), lambda i: (i, 0))],
        out_specs=pl.BlockSpec((tm, D_MODEL), lambda i: (i, 0)),
        out_shape=jax.ShapeDtypeStruct((rows, D_MODEL), F32),
        compiler_params=_params("parallel"),
        name=name,
    )(o, wuv, wo, g, h)


def _rope_tables(pos):
    half = QK_ROPE // 2
    inv = ROPE_BASE ** (-jnp.arange(half, dtype=F32) / half)
    ang = pos.astype(F32)[:, None] * inv[None, :]
    cos = jnp.cos(ang)
    sin = jnp.sin(ang)
    zeros = jnp.zeros((pos.shape[0], ROPE_PAD - QK_ROPE), F32)
    return jnp.concatenate([cos, cos, zeros], axis=1), jnp.concatenate([sin, sin, zeros], axis=1)


def _rot_cols(w):
    half = QK_ROPE // 2
    return jnp.concatenate([-w[..., half:], w[..., :half]], axis=-1)


def _pad_rope(w):
    pad = [(0, 0)] * (w.ndim - 1) + [(0, ROPE_PAD - QK_ROPE)]
    return jnp.pad(w, pad)


def kernel(x_prompt, x_sample, cache_ssm_re, cache_ssm_im, cache_kv_latent, cache_k_rope, cache_mem_k, cache_mem_v, page_table, mem_prompt, norm_mix_pre, norm_mix_post, norm_mem_pre, norm_mem_post, norm_mlp_pre, norm_mlp_post, mem_in_norm, w_mem_q, w_mem_k, w_mem_v, w_mem_o, w_mlp_up, w_mlp_down, ssm_a_re, ssm_a_im, ssm_log_dt, ssm_b_re, ssm_b_im, ssm_c_re, ssm_c_im, ssm_d, w_glu, kv_in_norm, w_dkv, kv_latent_norm, w_kr, w_uk, w_uv, w_dq, q_norm, w_uq, w_o):
    depth = norm_mix_pre.shape[0]
    n_a = ssm_a_re.shape[0]
    bp, sp, _ = x_prompt.shape
    bs, ts, _ = x_sample.shape
    past_len = page_table.shape[1] * PAGE_SIZE

    hp = x_prompt.astype(F32).reshape(bp * sp, D_MODEL)
    hs = x_sample.astype(F32).reshape(bs * ts, D_MODEL)

    mem_rows = mem_prompt.astype(F32).reshape(bp * MEM_TOKENS, D_MODEL)
    mk, mv, mkb, mvb = _mem_kv(mem_rows, mem_in_norm.astype(F32).reshape(depth, 1, D_MODEL),
                               w_mem_k.astype(BF16), w_mem_v.astype(BF16))
    cmk = _mem_cache_view(cache_mem_k.astype(F32))
    cmv = _mem_cache_view(cache_mem_v.astype(F32))

    ssm_p_re, ssm_p_im, ssm_s_re, ssm_s_im = [], [], [], []
    ckv_p = kr_p = ckv_s = kr_s = None
    for i in range(depth):
        if i < n_a:
            abar_re, abar_im, bbar_re, bbar_im = _ssm_discretise(
                ssm_a_re[i], ssm_a_im[i], ssm_log_dt[i], ssm_b_re[i], ssm_b_im[i])
            ng = SSM_GROUPS // SSM_KBLOCKS
            bdre = _block_diag(bbar_re.reshape(SSM_KBLOCKS, ng, SSM_GROUP, SSM_STATE)).astype(BF16)
            bdim = _block_diag(bbar_im.reshape(SSM_KBLOCKS, ng, SSM_GROUP, SSM_STATE)).astype(BF16)
            c_t = lambda c: c.astype(F32).reshape(SSM_KBLOCKS, ng, SSM_GROUP, SSM_STATE).transpose(0, 1, 3, 2)
            cre = _block_diag(c_t(ssm_c_re[i])).astype(BF16)
            ncim = _block_diag(-c_t(ssm_c_im[i])).astype(BF16)
            s5_w = (_row(norm_mix_pre[i]), bdre, bdim, abar_re.reshape(1, SSM_WIDTH), abar_im.reshape(1, SSM_WIDTH))
            s5_w2 = (cre, ncim, _row(ssm_d[i]), w_glu[i].astype(BF16), _row(norm_mix_post[i]))
            zeros = jnp.zeros((bp, SSM_WIDTH), F32)
            hp3, fpr, fpi = _s5_layer(hp.reshape(bp, sp, D_MODEL), 32, 2, bp, *s5_w, zeros, zeros, *s5_w2,
                                      name="s5_prompt")
            hs3, fsr, fsi = _s5_layer(hs.reshape(bs, ts, D_MODEL), ts, 1, 32, *s5_w,
                                      cache_ssm_re[i].astype(F32).reshape(bs, SSM_WIDTH),
                                      cache_ssm_im[i].astype(F32).reshape(bs, SSM_WIDTH), *s5_w2, name="s5_sample")
            hp = hp3.reshape(bp * sp, D_MODEL)
            hs = hs3.reshape(bs * ts, D_MODEL)
            ssm_p_re.append(fpr.reshape(bp, SSM_GROUPS, SSM_STATE))
            ssm_p_im.append(fpi.reshape(bp, SSM_GROUPS, SSM_STATE))
            ssm_s_re.append(fsr.reshape(bs, SSM_GROUPS, SSM_STATE))
            ssm_s_im.append(fsi.reshape(bs, SSM_GROUPS, SSM_STATE))
        else:
            j = i - n_a
            wuq = w_uq[j].astype(F32)
            w_nope = wuq[:, :, :QK_NOPE].reshape(Q_LORA, MLA_HEADS * QK_NOPE)
            w_r = wuq[:, :, QK_NOPE:]
            w_rope = _pad_rope(w_r).reshape(Q_LORA, MLA_HEADS * ROPE_PAD)
            w_rope_rot = _pad_rope(_rot_cols(w_r)).reshape(Q_LORA, MLA_HEADS * ROPE_PAD)
            wuq_all = jnp.concatenate([w_nope, w_rope, w_rope_rot], axis=1).astype(BF16)
            ukt = w_uk.astype(F32).transpose(1, 2, 0).astype(BF16)
            wkr = w_kr.astype(F32)
            wkr2 = jnp.concatenate([_pad_rope(wkr), _pad_rope(_rot_cols(wkr))], axis=1).astype(BF16)
            pre_w = (_row(norm_mix_pre[i]), _row(kv_in_norm), w_dq[j].astype(BF16), _row(q_norm[j]), wuq_all, ukt,
                     w_dkv.astype(BF16), _row(kv_latent_norm), wkr2)
            cos_p, sin_p = _rope_tables(jnp.arange(sp, dtype=jnp.int32))
            cos_s, sin_s = _rope_tables(past_len + jnp.tile(jnp.arange(ts, dtype=jnp.int32), 512 // ts))
            qlat_p, qrope_p, kvcat_p, ckv_pj, kr_pj = _mla_pre(hp, cos_p, sin_p, *pre_w, BF16, name="mla_pre_prompt")
            qlat_s, qrope_s, _, ckv_sj, kr_sj = _mla_pre(hs, cos_s, sin_s, *pre_w, F32, name="mla_pre_sample")
            if i == n_a:
                ckv_p, kr_p, ckv_s, kr_s, kvcat_shared = ckv_pj, kr_pj, ckv_sj, kr_sj, kvcat_p
            o_p = _mla_attn_prompt(qlat_p, qrope_p, kvcat_shared, sp, bp)
            o_s = _mla_attn_sample(qlat_s, qrope_s, ckv_s, kr_s, cache_kv_latent.astype(F32),
                                   jnp.swapaxes(cache_k_rope.astype(F32), 1, 2), page_table, ts, bs)
            wuv = w_uv.astype(F32).transpose(1, 0, 2).astype(BF16)
            wo = w_o[j].astype(BF16)
            hp = _mla_out(o_p, wuv, wo, _row(norm_mix_post[i]), hp, name="mla_out_prompt")
            hs = _mla_out(o_s, wuv, wo, _row(norm_mix_post[i]), hs, name="mla_out_sample")

        wq = w_mem_q[i].astype(BF16)
        wo_mem = w_mem_o[i].astype(BF16)
        hp = _mem_attn_prompt(hp, sp, bp, i, _row(norm_mem_pre[i]), wq, mkb, mvb, wo_mem, _row(norm_mem_post[i]))
        q_s = _norm_matmul(hs, _row(norm_mem_pre[i]), wq, MEM_SCALE, F32, name="mem_q_sample")
        o_s = _mem_attn_sample(q_s, ts, bs, i, cmk, cmv)
        hs = _matmul_norm_res(o_s, wo_mem, _row(norm_mem_post[i]), hs, name="mem_out_sample")

        wup = w_mlp_up[i].astype(BF16)
        wdn = w_mlp_down[i].astype(BF16)
        hp = _mlp(hp, _row(norm_mlp_pre[i]), wup, wdn, _row(norm_mlp_post[i]), name="mlp_prompt")
        hs = _mlp(hs, _row(norm_mlp_pre[i]), wup, wdn, _row(norm_mlp_post[i]), name="mlp_sample")

    return (hp.reshape(bp, sp, D_MODEL), hs.reshape(bs, ts, D_MODEL),
            jnp.stack(ssm_p_re), jnp.stack(ssm_p_im), jnp.stack(ssm_s_re), jnp.stack(ssm_s_im),
            ckv_p.reshape(bp, sp, KV_LORA), kr_p[:, :QK_ROPE].reshape(bp, sp, QK_ROPE),
            ckv_s.reshape(bs, ts, KV_LORA), kr_s[:, :QK_ROPE].reshape(bs, ts, QK_ROPE),
            _mem_cache_unview(mk, bp), _mem_cache_unview(mv, bp))
```

```python
import functools
import math

import jax
import jax.numpy as jnp
from jax import lax
from jax.experimental import pallas as pl
from jax.experimental.pallas import tpu as pltpu

F32 = jnp.float32
BF16 = jnp.bfloat16

LANES = 128
SUBLANES = 8
D_MODEL = 1024
PAGE_SIZE = 128
SSM_GROUP = 16
SSM_GROUPS = D_MODEL // SSM_GROUP
SSM_STATE = 64
SSM_WIDTH = SSM_GROUPS * SSM_STATE
SSM_KBLOCKS = 4
SSM_SLABS = SSM_WIDTH // LANES
MLA_HEADS = 8
QK_NOPE = 128
QK_ROPE = 64
ROPE_PAD = LANES
V_HEAD = 128
KV_LORA = 256
Q_LORA = 384
ROPE_BASE = 10000.0
MLA_SCALE = (QK_NOPE + QK_ROPE) ** -0.5
MLA_Q_SCALE = MLA_SCALE * math.log2(math.e)
MEM_TOKENS = 256
MEM_HEADS = 4
MEM_HEAD_DIM = D_MODEL // MEM_HEADS
MEM_SCALE = MEM_HEAD_DIM ** -0.5
D_FF = 4 * D_MODEL
EPS = 1e-6
NEG_BIG = -1e30

VMEM_LIMIT_BYTES = 56 * 1024 * 1024

ROW_TILE = 512
WIDE_ROW_TILE = 1024
MLP_FF_CHUNK = 1024
S5_STEPS = 32
S5_PROMPT_SUBTILES = 2
S5_SAMPLE_BATCHES = 32
MLA_PROMPT_TILE = 512
MEM_SAMPLE_BATCHES = 4
PAGE_SLOTS = 3
SAMPLE_KEY_PARTS = 4


def _params(*semantics):
    return pltpu.CompilerParams(dimension_semantics=semantics, vmem_limit_bytes=VMEM_LIMIT_BYTES)


def _dot(a, b):
    return jnp.dot(a, b, preferred_element_type=F32)


def _dot_t(a, b):
    return lax.dot_general(a, b, (((1,), (1,)), ((), ())), preferred_element_type=F32)


def _rms(x, g):
    return x * lax.rsqrt(jnp.mean(x * x, axis=-1, keepdims=True) + EPS) * g


def _row(v):
    return v.reshape(1, -1).astype(F32)


def _lane_tile(x, width):
    return jnp.concatenate([x] * (width // LANES), axis=1) if width > LANES else x


def _block_diag_rows(x):
    rows, n = x.shape
    groups = rows // SSM_GROUP
    tiled = jnp.concatenate([x] * groups, axis=1)
    own = (lax.broadcasted_iota(jnp.int32, tiled.shape, 0) // SSM_GROUP
           == lax.broadcasted_iota(jnp.int32, tiled.shape, 1) // n)
    return jnp.where(own, tiled, 0.0)


def _ssm_disc_kernel(ar_ref, ai_ref, ldt_ref, br_ref, bi_ref, cr_ref, ci_ref,
                     abar_re_ref, abar_im_ref, bdre_ref, bdim_ref, cre_ref, ncim_ref):
    ar = ar_ref[...]
    ai = ai_ref[...]
    dt = jnp.exp(ldt_ref[...])
    mag = jnp.exp(ar * dt)
    abar_re = mag * jnp.cos(ai * dt)
    abar_im = mag * jnp.sin(ai * dt)
    den = ar * ar + ai * ai
    nr = abar_re - 1.0
    ni = abar_im
    coef_re = (nr * ar + ni * ai) / den
    coef_im = (ni * ar - nr * ai) / den
    br = br_ref[...]
    bi = bi_ref[...]
    abar_re_ref[...] = abar_re
    abar_im_ref[...] = abar_im
    bbar_re = coef_re * br - coef_im * bi
    bbar_im = coef_re * bi + coef_im * br
    kw = D_MODEL // SSM_KBLOCKS
    for kb in range(SSM_KBLOCKS):
        rws = slice(kb * kw, (kb + 1) * kw)
        bdre_ref[kb] = _block_diag_rows(bbar_re[rws]).astype(BF16)
        bdim_ref[kb] = _block_diag_rows(bbar_im[rws]).astype(BF16)
        cre_ref[kb] = _block_diag_rows(cr_ref[rws, :]).T.astype(BF16)
        ncim_ref[kb] = _block_diag_rows(-ci_ref[rws, :]).T.astype(BF16)


def _ssm_discretise(a_re, a_im, log_dt, b_re, b_im, c_re, c_im):
    g, n, k = SSM_GROUPS, SSM_STATE, SSM_GROUP
    rep = lambda a: jnp.repeat(a.astype(F32), k, axis=0)
    brt = b_re.astype(F32).transpose(0, 2, 1).reshape(g * k, n)
    bit = b_im.astype(F32).transpose(0, 2, 1).reshape(g * k, n)
    ldt = jnp.repeat(log_dt.astype(F32), k, axis=0).reshape(g * k, 1)
    kw = D_MODEL // SSM_KBLOCKS
    sw = SSM_WIDTH // SSM_KBLOCKS
    abar = jax.ShapeDtypeStruct((g * k, n), F32)
    bd = jax.ShapeDtypeStruct((SSM_KBLOCKS, kw, sw), BF16)
    bdc = jax.ShapeDtypeStruct((SSM_KBLOCKS, sw, kw), BF16)
    abar_re, abar_im, bdre, bdim, cre, ncim = pl.pallas_call(
        _ssm_disc_kernel, out_shape=(abar, abar, bd, bd, bdc, bdc), name="ssm_discretise",
    )(rep(a_re), rep(a_im), ldt, brt, bit,
      c_re.astype(F32).reshape(g * k, n), c_im.astype(F32).reshape(g * k, n))
    return abar_re[::k], abar_im[::k], bdre, bdim, cre, ncim


def _s5_kernel(x_ref, gpre_ref, bdre_ref, bdim_ref, are_ref, aim_ref, h0re_ref, h0im_ref,
               cre_ref, ncim_ref, dskip_ref, wglu_ref, gpost_ref,
               out_ref, fre_ref, fim_ref, *scratch, tt, bb, nsub, xpitch, rpitch):
    ti = pl.program_id(1)

    @pl.when(ti == 0)
    def _():
        fre_ref[...] = h0re_ref[...]
        fim_ref[...] = h0im_ref[...]

    for sub in range(nsub):
        _s5_subtile(x_ref, gpre_ref, bdre_ref, bdim_ref, are_ref, aim_ref, cre_ref, ncim_ref, dskip_ref,
                    wglu_ref, gpost_ref, out_ref, fre_ref, fim_ref, *scratch[4 * sub:4 * sub + 4],
                    t0=sub * tt, tt=tt, bb=bb, xpitch=xpitch, rpitch=rpitch)


def _s5_subtile(x_ref, gpre_ref, bdre_ref, bdim_ref, are_ref, aim_ref, cre_ref, ncim_ref, dskip_ref,
                wglu_ref, gpost_ref, out_ref, fre_ref, fim_ref, xs_ref, bure_ref, buim_ref, rs_ref,
                *, t0, tt, bb, xpitch, rpitch):
    rows = tt * bb
    nsl = D_MODEL // LANES
    spk = SSM_SLABS // SSM_KBLOCKS
    steps = slice(t0, t0 + tt)

    for b in range(bb):
        for s in range(nsl):
            xs_ref[s, b * xpitch:b * xpitch + tt, :] = x_ref[b, steps, s * LANES:(s + 1) * LANES]
    x = jnp.concatenate(
        [jnp.concatenate([xs_ref[s, pl.ds(t, bb, stride=xpitch), :] for s in range(nsl)], axis=1)
         for t in range(tt)], axis=0)
    xn = _rms(x, gpre_ref[...])
    xb = xn.astype(BF16)
    kw = D_MODEL // SSM_KBLOCKS
    sw = SSM_WIDTH // SSM_KBLOCKS

    def scan(slab):
        lanes = slice(slab * LANES, (slab + 1) * LANES)
        a_re = jnp.broadcast_to(are_ref[:, lanes], (SUBLANES, LANES))
        a_im = jnp.broadcast_to(aim_ref[:, lanes], (SUBLANES, LANES))
        for vb in range(bb // SUBLANES):
            b0 = vb * SUBLANES
            h_re = fre_ref[b0:b0 + SUBLANES, lanes]
            h_im = fim_ref[b0:b0 + SUBLANES, lanes]
            for t in range(tt):
                rws = slice(t * bb + b0, t * bb + b0 + SUBLANES)
                h_re, h_im = (a_re * h_re - a_im * h_im + bure_ref[rws, lanes],
                              a_re * h_im + a_im * h_re + buim_ref[rws, lanes])
                bure_ref[rws, lanes] = h_re
                buim_ref[rws, lanes] = h_im
            fre_ref[b0:b0 + SUBLANES, lanes] = h_re
            fim_ref[b0:b0 + SUBLANES, lanes] = h_im

    ys = []
    for kb in range(SSM_KBLOCKS):
        cols = slice(kb * sw, (kb + 1) * sw)
        xk = xb[:, kb * kw:(kb + 1) * kw]
        bure_ref[:, cols] = _dot(xk, bdre_ref[kb])
        buim_ref[:, cols] = _dot(xk, bdim_ref[kb])
        for s in range(spk):
            scan(kb * spk + s)
        ys.append(_dot(bure_ref[:, cols].astype(BF16), cre_ref[kb])
                  + _dot(buim_ref[:, cols].astype(BF16), ncim_ref[kb]))
    y = jnp.concatenate(ys, axis=1) + dskip_ref[...] * xn
    g = 0.5 * y * (1.0 + lax.erf(y * (2.0 ** -0.5)))
    z = _dot(g.astype(BF16), wglu_ref[...])
    o = z[:, :D_MODEL] * (1.0 / (1.0 + jnp.exp(-z[:, D_MODEL:])))
    r = _rms(o, gpost_ref[...])
    for s in range(nsl):
        for t in range(tt):
            rs_ref[s, t * rpitch:t * rpitch + bb, :] = r[t * bb:(t + 1) * bb, s * LANES:(s + 1) * LANES]
    for b in range(bb):
        rb = jnp.concatenate([rs_ref[s, pl.ds(b, tt, stride=rpitch), :] for s in range(nsl)], axis=1)
        out_ref[b, steps, :] = x_ref[b, steps, :] + rb


def _odd_pitch(n):
    groups = -(-n // SUBLANES)
    return SUBLANES * (groups if groups % 2 else groups + 1)


def _s5_layer(x3, tt, nsub, bb, gpre, bdre, bdim, a_re, a_im, h0_re, h0_im,
              cre, ncim, dskip, wglu, gpost, name):
    b_all, t_len, _ = x3.shape
    grid = (b_all // bb, t_len // (tt * nsub))
    xpitch, rpitch = _odd_pitch(tt), _odd_pitch(bb)
    nsl = D_MODEL // LANES
    bu = pltpu.VMEM((tt * bb, SSM_WIDTH), F32)
    subtile_scratch = [pltpu.VMEM((nsl, bb * xpitch, LANES), F32), bu, bu,
                       pltpu.VMEM((nsl, tt * rpitch, LANES), F32)]
    return pl.pallas_call(
        functools.partial(_s5_kernel, tt=tt, bb=bb, nsub=nsub, xpitch=xpitch, rpitch=rpitch),
        grid=grid,
        in_specs=[
            pl.BlockSpec((bb, tt * nsub, D_MODEL), lambda bj, ti: (bj, ti, 0)),
            _resident((1, D_MODEL)),
            _resident(bdre.shape),
            _resident(bdim.shape),
            _resident((1, SSM_WIDTH)),
            _resident((1, SSM_WIDTH)),
            pl.BlockSpec((bb, SSM_WIDTH), lambda bj, ti: (bj, 0)),
            pl.BlockSpec((bb, SSM_WIDTH), lambda bj, ti: (bj, 0)),
            _resident(cre.shape),
            _resident(ncim.shape),
            _resident((1, D_MODEL)),
            _resident(wglu.shape),
            _resident((1, D_MODEL)),
        ],
        out_specs=(
            pl.BlockSpec((bb, tt * nsub, D_MODEL), lambda bj, ti: (bj, ti, 0)),
            pl.BlockSpec((bb, SSM_WIDTH), lambda bj, ti: (bj, 0)),
            pl.BlockSpec((bb, SSM_WIDTH), lambda bj, ti: (bj, 0)),
        ),
        out_shape=(
            jax.ShapeDtypeStruct((b_all, t_len, D_MODEL), F32),
            jax.ShapeDtypeStruct((b_all, SSM_WIDTH), F32),
            jax.ShapeDtypeStruct((b_all, SSM_WIDTH), F32),
        ),
        scratch_shapes=subtile_scratch * nsub,
        compiler_params=_params("parallel", "arbitrary"),
        name=name,
    )(x3, gpre, bdre, bdim, a_re, a_im, h0_re, h0_im, cre, ncim, dskip, wglu, gpost)


def _mem_kv_kernel(x_ref, g_ref, wk_ref, wv_ref, k_ref, v_ref, kb_ref, vb_ref):
    xn = _rms(x_ref[...], g_ref[0]).astype(BF16)
    k = _dot(xn, wk_ref[0])
    v = _dot(xn, wv_ref[0])
    tm = k.shape[0]
    chunks = MEM_HEAD_DIM // LANES
    period = chunks * MEM_HEADS
    for hh in range(MEM_HEADS):
        for c in range(chunks):
            lanes = slice(hh * MEM_HEAD_DIM + c * LANES, hh * MEM_HEAD_DIM + (c + 1) * LANES)
            k_ref[0, pl.ds(c * MEM_HEADS + hh, tm, stride=period), :] = k[:, lanes]
            v_ref[0, pl.ds(c * MEM_HEADS + hh, tm, stride=period), :] = v[:, lanes]
    kb_ref[0] = k.astype(BF16)
    vb_ref[0] = v.astype(BF16)


def _mem_kv(mem, g, wk, wv):
    rows = mem.shape[0]
    nl = g.shape[0]
    tm = ROW_TILE
    period = D_MODEL // LANES
    blk = pl.BlockSpec((1, tm, D_MODEL), lambda l, i: (l, i, 0))
    nat = pl.BlockSpec((1, tm * period, LANES), lambda l, i: (l, i, 0))
    wspec = pl.BlockSpec((1, D_MODEL, D_MODEL), lambda l, i: (l, 0, 0))
    return pl.pallas_call(
        _mem_kv_kernel,
        grid=(nl, rows // tm),
        in_specs=[pl.BlockSpec((tm, D_MODEL), lambda l, i: (i, 0)),
                  pl.BlockSpec((1, 1, D_MODEL), lambda l, i: (l, 0, 0)), wspec, wspec],
        out_specs=(nat, nat, blk, blk),
        out_shape=(jax.ShapeDtypeStruct((nl, rows * period, LANES), F32),) * 2
        + (jax.ShapeDtypeStruct((nl, rows, D_MODEL), BF16),) * 2,
        compiler_params=_params("parallel", "parallel"),
        name="mem_kv",
    )(mem, g, wk, wv)


def _softmax_rows(s):
    m = jnp.max(s, axis=-1, keepdims=True)
    p = jnp.exp(s - m)
    return p / jnp.sum(p, axis=-1, keepdims=True)


def _row_halves(n):
    return (slice(0, n // 2), slice(n // 2, n))


def _mem_attn_prompt_kernel(h_ref, gpre_ref, wq_ref, k_ref, v_ref, wo_ref, gpost_ref, out_ref):
    for rws in _row_halves(h_ref.shape[0]):
        x = h_ref[rws, :]
        xn = _rms(x, gpre_ref[...]).astype(BF16)
        q = (_dot(xn, wq_ref[...]) * MEM_SCALE).astype(BF16)
        os = []
        for hh in range(MEM_HEADS):
            cols = slice(hh * MEM_HEAD_DIM, (hh + 1) * MEM_HEAD_DIM)
            p = _softmax_rows(_dot_t(q[:, cols], k_ref[0, :, cols]))
            os.append(_dot(p.astype(BF16), v_ref[0, :, cols]).astype(BF16))
        y = _dot(jnp.concatenate(os, axis=1), wo_ref[...])
        out_ref[rws, :] = x + _rms(y, gpost_ref[...])


def _mem_attn_prompt(h, t_len, b_all, layer, gpre, wq, kb, vb, wo, gpost):
    tq = WIDE_ROW_TILE
    nq = t_len // tq
    const2 = lambda b, i: (0, 0)
    hspec = pl.BlockSpec((tq, D_MODEL), lambda b, i: (b * nq + i, 0))
    kvspec = pl.BlockSpec((1, MEM_TOKENS, D_MODEL), lambda b, i: (layer, b, 0))
    wspec = pl.BlockSpec((D_MODEL, D_MODEL), const2)
    gspec = pl.BlockSpec((1, D_MODEL), const2)
    return pl.pallas_call(
        _mem_attn_prompt_kernel,
        grid=(b_all, nq),
        in_specs=[hspec, gspec, wspec, kvspec, kvspec, wspec, gspec],
        out_specs=hspec,
        out_shape=jax.ShapeDtypeStruct(h.shape, F32),
        compiler_params=_params("parallel", "parallel"),
        name="mem_attn_prompt",
    )(h, gpre, wq, kb, vb, wo, gpost)


def _mem_cache_view(cache):
    nl, nb, m, nh, dh = cache.shape
    chunks = dh // LANES
    return cache.reshape(nl, nb, m, nh, chunks, LANES).transpose(0, 1, 2, 4, 3, 5).reshape(
        nl, nb, m * chunks * nh, LANES)


def _mem_cache_unview(x, nb):
    nl = x.shape[0]
    chunks = MEM_HEAD_DIM // LANES
    return x.reshape(nl, nb, MEM_TOKENS, chunks, MEM_HEADS, LANES).transpose(0, 1, 2, 4, 3, 5).reshape(
        nl, nb, MEM_TOKENS, MEM_HEADS, MEM_HEAD_DIM)


def _mem_attn_sample_kernel(q_ref, k_ref, v_ref, o_ref, *, nb, ts):
    chunks = MEM_HEAD_DIM // LANES
    period = chunks * MEM_HEADS

    def head(ref, g, hh):
        parts = [ref[0, g, pl.ds(c * MEM_HEADS + hh, MEM_TOKENS, stride=period), :] for c in range(chunks)]
        return jnp.concatenate(parts, axis=1).astype(BF16)

    pairs = [(g, hh) for g in range(nb) for hh in range(MEM_HEADS)]
    ss = [_dot_t(q_ref[g * ts:(g + 1) * ts, hh * MEM_HEAD_DIM:(hh + 1) * MEM_HEAD_DIM].astype(BF16),
                 head(k_ref, g, hh)) for g, hh in pairs]
    p = _softmax_rows(jnp.concatenate(ss, axis=0))
    for i, (g, hh) in enumerate(pairs):
        o_ref[g * ts:(g + 1) * ts, hh * MEM_HEAD_DIM:(hh + 1) * MEM_HEAD_DIM] = _dot(
            p[i * ts:(i + 1) * ts].astype(BF16), head(v_ref, g, hh))


def _mem_attn_sample(q, ts, b_all, layer, cache_k, cache_v):
    nb = MEM_SAMPLE_BATCHES
    qspec = pl.BlockSpec((nb * ts, D_MODEL), lambda j: (j, 0))
    cspec = pl.BlockSpec((1, nb) + cache_k.shape[2:], lambda j: (layer, j, 0, 0))
    return pl.pallas_call(
        functools.partial(_mem_attn_sample_kernel, nb=nb, ts=ts),
        grid=(b_all // nb,),
        in_specs=[qspec, cspec, cspec],
        out_specs=qspec,
        out_shape=jax.ShapeDtypeStruct(q.shape, F32),
        compiler_params=_params("parallel"),
        name="mem_attn_sample",
    )(q, cache_k, cache_v)


def _norm_matmul_kernel(x_ref, g_ref, w_ref, o_ref, *, scale):
    xn = _rms(x_ref[...], g_ref[...]).astype(BF16)
    o_ref[...] = (_dot(xn, w_ref[...]) * scale).astype(o_ref.dtype)


def _norm_matmul(x, g, w, scale, out_dtype, name):
    rows = x.shape[0]
    tm = ROW_TILE
    n = w.shape[1]
    return pl.pallas_call(
        functools.partial(_norm_matmul_kernel, scale=scale),
        grid=(rows // tm,),
        in_specs=[pl.BlockSpec((tm, D_MODEL), lambda i: (i, 0)),
                  pl.BlockSpec((1, D_MODEL), lambda i: (0, 0)),
                  pl.BlockSpec(w.shape, lambda i: (0, 0))],
        out_specs=pl.BlockSpec((tm, n), lambda i: (i, 0)),
        out_shape=jax.ShapeDtypeStruct((rows, n), out_dtype),
        compiler_params=_params("parallel"),
        name=name,
    )(x, g, w)


def _matmul_norm_res_kernel(a_ref, w_ref, g_ref, h_ref, o_ref):
    y = _dot(a_ref[...].astype(BF16), w_ref[...])
    o_ref[...] = h_ref[...] + _rms(y, g_ref[...])


def _matmul_norm_res(a, w, g, h, name):
    rows, k = a.shape
    tm = ROW_TILE
    return pl.pallas_call(
        _matmul_norm_res_kernel,
        grid=(rows // tm,),
        in_specs=[pl.BlockSpec((tm, k), lambda i: (i, 0)),
                  pl.BlockSpec(w.shape, lambda i: (0, 0)),
                  pl.BlockSpec((1, D_MODEL), lambda i: (0, 0)),
                  pl.BlockSpec((tm, D_MODEL), lambda i: (i, 0))],
        out_specs=pl.BlockSpec((tm, D_MODEL), lambda i: (i, 0)),
        out_shape=jax.ShapeDtypeStruct((rows, D_MODEL), F32),
        compiler_params=_params("parallel"),
        name=name,
    )(a, w, g, h)


def _mlp_kernel(x_ref, gpre_ref, wup_ref, wdn_ref, gpost_ref, o_ref, *, tf):
    x = x_ref[...]
    xn = _rms(x, gpre_ref[...]).astype(BF16)
    acc = None
    for c in range(D_FF // tf):
        hid = jnp.maximum(_dot(xn, wup_ref[:, c * tf:(c + 1) * tf]), 0.0)
        part = _dot((hid * hid).astype(BF16), wdn_ref[c * tf:(c + 1) * tf, :])
        acc = part if acc is None else acc + part
    o_ref[...] = x + _rms(acc, gpost_ref[...])


def _resident(shape):
    return pl.BlockSpec(shape, lambda *_: (0,) * len(shape), pipeline_mode=pl.Buffered(1))


def _mlp(h, gpre, wup, wdn, gpost, name):
    rows = h.shape[0]
    tm, tf = ROW_TILE, MLP_FF_CHUNK
    return pl.pallas_call(
        functools.partial(_mlp_kernel, tf=tf),
        grid=(rows // tm,),
        in_specs=[pl.BlockSpec((tm, D_MODEL), lambda i: (i, 0)),
                  _resident((1, D_MODEL)), _resident(wup.shape), _resident(wdn.shape), _resident((1, D_MODEL))],
        out_specs=pl.BlockSpec((tm, D_MODEL), lambda i: (i, 0)),
        out_shape=jax.ShapeDtypeStruct((rows, D_MODEL), F32),
        compiler_params=_params("parallel"),
        name=name,
    )(h, gpre, wup, wdn, gpost)


def _mla_pre_kernel(h_ref, cos_ref, sin_ref, gmix_ref, gkv_ref, wdq_ref, qn_ref, wuq_ref, ukt_ref,
                    wdkv_ref, glat_ref, wkr_ref,
                    qlat_ref, qrope_ref, kvcat_ref, ckv_ref, kr_ref):
    x = h_ref[...]
    y = x * lax.rsqrt(jnp.mean(x * x, axis=-1, keepdims=True) + EPS)
    cos = cos_ref[...]
    sin = sin_ref[...]
    hn = (y * gkv_ref[...]).astype(BF16)
    ckv = _rms(_dot(hn, wdkv_ref[...]), glat_ref[...])
    kr2 = _dot(hn, wkr_ref[...])
    kr = kr2[:, :ROPE_PAD] * cos + kr2[:, ROPE_PAD:] * sin
    ckv_ref[...] = ckv
    kr_ref[...] = kr
    kvcat_ref[...] = jnp.concatenate([ckv, kr], axis=1).astype(BF16)
    xn = (y * gmix_ref[...]).astype(BF16)
    cq = _rms(_dot(xn, wdq_ref[...]), qn_ref[...]).astype(BF16)
    q = _dot(cq, wuq_ref[...])
    nope_w = MLA_HEADS * QK_NOPE
    rope_w = MLA_HEADS * ROPE_PAD
    qr = (q[:, nope_w:nope_w + rope_w] * _lane_tile(cos, rope_w)
          + q[:, nope_w + rope_w:] * _lane_tile(sin, rope_w))
    qrope_ref[...] = (qr * MLA_Q_SCALE).astype(qrope_ref.dtype)
    for hh in range(MLA_HEADS):
        qn_h = q[:, hh * QK_NOPE:(hh + 1) * QK_NOPE].astype(BF16)
        ql = _dot(qn_h, ukt_ref[hh])
        qlat_ref[:, hh * KV_LORA:(hh + 1) * KV_LORA] = (ql * MLA_Q_SCALE).astype(qlat_ref.dtype)


def _mla_pre(h, cos, sin, gmix, gkv, wdq, qn, wuq, ukt, wdkv, glat, wkr, q_dtype, name):
    rows = h.shape[0]
    tm = ROW_TILE
    period = cos.shape[0] // tm
    row = lambda w: pl.BlockSpec((tm, w), lambda i: (i, 0))
    tab = pl.BlockSpec((tm, ROPE_PAD), lambda i: (i % period, 0))
    full = lambda a: pl.BlockSpec(a.shape, lambda i: (0,) * a.ndim)
    return pl.pallas_call(
        _mla_pre_kernel,
        grid=(rows // tm,),
        in_specs=[row(D_MODEL), tab, tab, full(gmix), full(gkv), full(wdq), full(qn),
                  full(wuq), full(ukt), full(wdkv), full(glat), full(wkr)],
        out_specs=(row(MLA_HEADS * KV_LORA), row(MLA_HEADS * ROPE_PAD), row(KV_LORA + ROPE_PAD),
                   row(KV_LORA), row(ROPE_PAD)),
        out_shape=(jax.ShapeDtypeStruct((rows, MLA_HEADS * KV_LORA), q_dtype),
                   jax.ShapeDtypeStruct((rows, MLA_HEADS * ROPE_PAD), q_dtype),
                   jax.ShapeDtypeStruct((rows, KV_LORA + ROPE_PAD), BF16),
                   jax.ShapeDtypeStruct((rows, KV_LORA), F32),
                   jax.ShapeDtypeStruct((rows, ROPE_PAD), F32)),
        compiler_params=_params("parallel"),
        name=name,
    )(h, cos, sin, gmix, gkv, wdq, qn, wuq, ukt, wdkv, glat, wkr)


def _mla_attn_prompt_kernel(qlat_ref, qrope_ref, kv_ref, o_ref, m_ref, l_ref, acc_ref, *, tq):
    qi = pl.program_id(1)
    m_ref[...] = jnp.full_like(m_ref, NEG_BIG)
    l_ref[...] = jnp.zeros_like(l_ref)
    acc_ref[...] = jnp.zeros_like(acc_ref)

    def block(j, diagonal):
        k0 = pl.multiple_of(j * tq, tq)
        ck = kv_ref[pl.ds(k0, tq), :KV_LORA]
        kr = kv_ref[pl.ds(k0, tq), KV_LORA:]
        for hh in range(MLA_HEADS):
            s = (_dot_t(qlat_ref[:, hh * KV_LORA:(hh + 1) * KV_LORA], ck)
                 + _dot_t(qrope_ref[:, hh * ROPE_PAD:(hh + 1) * ROPE_PAD], kr))
            if diagonal:
                visible = (lax.broadcasted_iota(jnp.int32, (tq, tq), 1)
                           <= lax.broadcasted_iota(jnp.int32, (tq, tq), 0))
                s = jnp.where(visible, s, NEG_BIG)
            m_old = m_ref[hh]
            m_new = jnp.maximum(m_old, jnp.max(s, axis=-1, keepdims=True))
            alpha = jnp.exp2(m_old - m_new)
            p = jnp.exp2(s - _lane_tile(m_new, tq))
            l_ref[hh] = alpha * l_ref[hh] + jnp.sum(p, axis=-1, keepdims=True)
            acc_ref[hh] = _lane_tile(alpha, KV_LORA) * acc_ref[hh] + _dot(p.astype(BF16), ck)
            m_ref[hh] = m_new

    def body(j, _):
        block(j, False)
        return 0

    lax.fori_loop(0, qi, body, 0)
    block(qi, True)
    for hh in range(MLA_HEADS):
        o = acc_ref[hh] / _lane_tile(l_ref[hh], KV_LORA)
        o_ref[:, hh * KV_LORA:(hh + 1) * KV_LORA] = o.astype(BF16)


def _mla_attn_prompt(qlat, qrope, kvcat, t_len, b_all):
    tq = MLA_PROMPT_TILE
    nq = t_len // tq
    lw, rw, kw = MLA_HEADS * KV_LORA, MLA_HEADS * ROPE_PAD, KV_LORA + ROPE_PAD
    return pl.pallas_call(
        functools.partial(_mla_attn_prompt_kernel, tq=tq),
        grid=(b_all, nq),
        in_specs=[pl.BlockSpec((tq, lw), lambda b, i: (b * nq + i, 0)),
                  pl.BlockSpec((tq, rw), lambda b, i: (b * nq + i, 0)),
                  pl.BlockSpec((t_len, kw), lambda b, i: (b, 0))],
        out_specs=pl.BlockSpec((tq, lw), lambda b, i: (b * nq + i, 0)),
        out_shape=jax.ShapeDtypeStruct((b_all * t_len, lw), BF16),
        scratch_shapes=[pltpu.VMEM((MLA_HEADS, tq, LANES), F32), pltpu.VMEM((MLA_HEADS, tq, LANES), F32),
                        pltpu.VMEM((MLA_HEADS, tq, KV_LORA), F32)],
        compiler_params=_params("parallel", "parallel"),
        name="mla_attn_prompt",
    )(qlat, qrope, kvcat)


def _stack_heads(ref, width):
    return jnp.concatenate([ref[:, hh * width:(hh + 1) * width] for hh in range(MLA_HEADS)], axis=0)


def _mla_attn_sample_kernel(pt_ref, qlat_ref, qrope_ref, ckvn_ref, krn_ref, kv_hbm, kr_hbm,
                            o_ref, kvbuf, krbuf, sem, s_ref, kvb_ref, *, ts, n_pages, parts):
    slots = kvbuf.shape[0]
    ahead = slots - 1
    b = pl.program_id(0)
    slot = b % slots
    rows = MLA_HEADS * ts
    past = n_pages * PAGE_SIZE

    def page_copies(batch, sl, page_of):
        cps = []
        for p in range(n_pages):
            page = page_of(batch, p)
            cps.append(pltpu.make_async_copy(kv_hbm.at[page], kvbuf.at[sl, p], sem.at[0, sl]))
            cps.append(pltpu.make_async_copy(kr_hbm.at[page], krbuf.at[sl, p], sem.at[1, sl]))
        return cps

    def start_all(cps, first=0):
        for i, cp in enumerate(cps, start=first):
            cp.start(priority=(i // 2) % 2)

    table_page = lambda batch, p: pt_ref[batch * n_pages + p]
    last = pl.num_programs(0) - 1

    any_page = lambda batch, p: 0

    @pl.when(b == 0)
    def _():
        for d in range(ahead):
            start_all(page_copies(jnp.minimum(d, last), d, table_page))

    for cp in page_copies(b, slot, any_page):
        cp.wait()

    prefetch = page_copies(jnp.minimum(b + ahead, last), (b + ahead) % slots, table_page)
    per_pair = len(prefetch) // (n_pages // 2)

    ql = _stack_heads(qlat_ref, KV_LORA).astype(BF16)
    qr = _stack_heads(qrope_ref, ROPE_PAD).astype(BF16)
    qr_cache = qr[:, :QK_ROPE]
    pair = 2 * PAGE_SIZE
    pairs_per_part = n_pages // 2 // parts
    span = pairs_per_part * pair
    ms, ls, os = [], [], []
    for part in range(parts):
        for g in range(part * pairs_per_part, (part + 1) * pairs_per_part):
            ck = kvbuf[slot, 2 * g:2 * g + 2].reshape(pair, KV_LORA).astype(BF16)
            kr = jnp.concatenate([krbuf[slot, 2 * g], krbuf[slot, 2 * g + 1]], axis=1).astype(BF16)
            s_ref[:, g * pair:(g + 1) * pair] = _dot_t(ql, ck) + _dot(qr_cache, kr)
            kvb_ref[g * pair:(g + 1) * pair, :] = ck
            start_all(prefetch[g * per_pair:(g + 1) * per_pair], first=g * per_pair)
        hi = (part + 1) * span
        if part == parts - 1:
            pad = PAGE_SIZE - ts
            ckn = jnp.concatenate([ckvn_ref[...], jnp.zeros((pad, KV_LORA), F32)], axis=0).astype(BF16)
            krn = jnp.concatenate([krn_ref[...], jnp.zeros((pad, ROPE_PAD), F32)], axis=0).astype(BF16)
            s_new = _dot_t(ql, ckn) + _dot_t(qr, krn)
            t_q = lax.broadcasted_iota(jnp.int32, (rows, PAGE_SIZE), 0) & (ts - 1)
            t_k = lax.broadcasted_iota(jnp.int32, (rows, PAGE_SIZE), 1)
            s_ref[:, past:] = jnp.where(t_k <= t_q, s_new, NEG_BIG)
            kvb_ref[past:, :] = ckn
            hi = past + PAGE_SIZE
        s = s_ref[:, part * span:hi]
        m = jnp.max(s, axis=-1, keepdims=True)
        p = jnp.exp2(s - m)
        ms.append(m)
        ls.append(jnp.sum(p, axis=-1, keepdims=True))
        os.append(_dot(p.astype(BF16), kvb_ref[part * span:hi, :]))
    m_all = functools.reduce(jnp.maximum, ms)
    ws = [jnp.exp2(m - m_all) for m in ms]
    o = sum(w * o_part for w, o_part in zip(ws, os)) / sum(w * l for w, l in zip(ws, ls))
    for hh in range(MLA_HEADS):
        o_ref[:, hh * KV_LORA:(hh + 1) * KV_LORA] = o[hh * ts:(hh + 1) * ts]

    @pl.when(b == last)
    def _():
        for d in range(1, slots):
            for cp in page_copies(b, (b + d) % slots, any_page):
                cp.wait()


def _mla_attn_sample(qlat, qrope, ckv_new, kr_new, cache_kv, cache_kr_t, page_table, ts, b_all):
    slots = PAGE_SLOTS
    n_pages = page_table.shape[1]
    lw, rw = MLA_HEADS * KV_LORA, MLA_HEADS * ROPE_PAD
    rows = MLA_HEADS * ts
    keys = (n_pages + 1) * PAGE_SIZE
    pt = page_table.reshape(-1)
    stream = lambda w: pl.BlockSpec((ts, w), lambda b, pt_ref: (b, 0))
    hbm = pl.BlockSpec(memory_space=pl.ANY)
    grid_spec = pltpu.PrefetchScalarGridSpec(
        num_scalar_prefetch=1,
        grid=(b_all,),
        in_specs=[stream(lw), stream(rw), stream(KV_LORA), stream(ROPE_PAD), hbm, hbm],
        out_specs=stream(lw),
        scratch_shapes=[pltpu.VMEM((slots, n_pages, PAGE_SIZE, KV_LORA), F32),
                        pltpu.VMEM((slots, n_pages, QK_ROPE, PAGE_SIZE), F32),
                        pltpu.SemaphoreType.DMA((2, slots)),
                        pltpu.VMEM((rows, keys), F32),
                        pltpu.VMEM((keys, KV_LORA), BF16)],
    )
    return pl.pallas_call(
        functools.partial(_mla_attn_sample_kernel, ts=ts, n_pages=n_pages, parts=SAMPLE_KEY_PARTS),
        grid_spec=grid_spec,
        out_shape=jax.ShapeDtypeStruct((b_all * ts, lw), F32),
        compiler_params=_params("arbitrary"),
        name="mla_attn_sample",
    )(pt, qlat, qrope, ckv_new, kr_new, cache_kv, cache_kr_t)


def _mla_out_kernel(o_ref, wuv_ref, wo_ref, g_ref, h_ref, out_ref):
    for rws in _row_halves(h_ref.shape[0]):
        vs = []
        for hh in range(MLA_HEADS):
            vs.append(_dot(o_ref[rws, hh * KV_LORA:(hh + 1) * KV_LORA].astype(BF16), wuv_ref[hh]).astype(BF16))
        y = _dot(jnp.concatenate(vs, axis=1), wo_ref[...])
        out_ref[rws, :] = h_ref[rws, :] + _rms(y, g_ref[...])


def _mla_out(o, wuv, wo, g, h, name):
    rows = h.shape[0]
    tm = WIDE_ROW_TILE
    lw = MLA_HEADS * KV_LORA
    return pl.pallas_call(
        _mla_out_kernel,
        grid=(rows // tm,),
        in_specs=[pl.BlockSpec((tm, lw), lambda i: (i, 0)),
                  pl.BlockSpec(wuv.shape, lambda i: (0, 0, 0)),
                  pl.BlockSpec(wo.shape, lambda i: (0, 0)),
                  pl.BlockSpec((1, D_MODEL), lambda i: (0, 0)),
                  pl.BlockSpec((tm, D_MODEL), lambda i: (i, 0))],
        out_specs=pl.BlockSpec((tm, D_MODEL), lambda i: (i, 0)),
        out_shape=jax.ShapeDtypeStruct((rows, D_MODEL), F32),
        compiler_params=_params("parallel"),
        name=name,
    )(o, wuv, wo, g, h)


def _rope_tables(pos):
    half = QK_ROPE // 2
    inv = ROPE_BASE ** (-jnp.arange(half, dtype=F32) / half)
    ang = pos.astype(F32)[:, None] * inv[None, :]
    cos = jnp.cos(ang)
    sin = jnp.sin(ang)
    zeros = jnp.zeros((pos.shape[0], ROPE_PAD - QK_ROPE), F32)
    return jnp.concatenate([cos, cos, zeros], axis=1), jnp.concatenate([sin, sin, zeros], axis=1)


def _rot_cols(w):
    half = QK_ROPE // 2
    return jnp.concatenate([-w[..., half:], w[..., :half]], axis=-1)


def _pad_rope(w):
    pad = [(0, 0)] * (w.ndim - 1) + [(0, ROPE_PAD - QK_ROPE)]
    return jnp.pad(w, pad)


def kernel(x_prompt, x_sample, cache_ssm_re, cache_ssm_im, cache_kv_latent, cache_k_rope, cache_mem_k, cache_mem_v, page_table, mem_prompt, norm_mix_pre, norm_mix_post, norm_mem_pre, norm_mem_post, norm_mlp_pre, norm_mlp_post, mem_in_norm, w_mem_q, w_mem_k, w_mem_v, w_mem_o, w_mlp_up, w_mlp_down, ssm_a_re, ssm_a_im, ssm_log_dt, ssm_b_re, ssm_b_im, ssm_c_re, ssm_c_im, ssm_d, w_glu, kv_in_norm, w_dkv, kv_latent_norm, w_kr, w_uk, w_uv, w_dq, q_norm, w_uq, w_o):
    depth = norm_mix_pre.shape[0]
    n_a = ssm_a_re.shape[0]
    bp, sp, _ = x_prompt.shape
    bs, ts, _ = x_sample.shape
    past_len = page_table.shape[1] * PAGE_SIZE

    hp = x_prompt.astype(F32).reshape(bp * sp, D_MODEL)
    hs = x_sample.astype(F32).reshape(bs * ts, D_MODEL)

    mem_rows = mem_prompt.astype(F32).reshape(bp * MEM_TOKENS, D_MODEL)
    mk, mv, mkb, mvb = _mem_kv(mem_rows, mem_in_norm.astype(F32).reshape(depth, 1, D_MODEL),
                               w_mem_k.astype(BF16), w_mem_v.astype(BF16))
    cmk = _mem_cache_view(cache_mem_k.astype(F32))
    cmv = _mem_cache_view(cache_mem_v.astype(F32))

    ssm_p_re, ssm_p_im, ssm_s_re, ssm_s_im = [], [], [], []
    ckv_p = kr_p = ckv_s = kr_s = None
    for i in range(depth):
        if i < n_a:
            abar_re, abar_im, bdre, bdim, cre, ncim = _ssm_discretise(
                ssm_a_re[i], ssm_a_im[i], ssm_log_dt[i], ssm_b_re[i], ssm_b_im[i], ssm_c_re[i], ssm_c_im[i])
            s5_w = (_row(norm_mix_pre[i]), bdre, bdim, abar_re.reshape(1, SSM_WIDTH), abar_im.reshape(1, SSM_WIDTH))
            s5_w2 = (cre, ncim, _row(ssm_d[i]), w_glu[i].astype(BF16), _row(norm_mix_post[i]))
            zeros = jnp.zeros((bp, SSM_WIDTH), F32)
            hp3, fpr, fpi = _s5_layer(hp.reshape(bp, sp, D_MODEL), S5_STEPS, S5_PROMPT_SUBTILES, bp, *s5_w, zeros, zeros, *s5_w2,
                                      name="s5_prompt")
            hs3, fsr, fsi = _s5_layer(hs.reshape(bs, ts, D_MODEL), ts, 1, S5_SAMPLE_BATCHES, *s5_w,
                                      cache_ssm_re[i].astype(F32).reshape(bs, SSM_WIDTH),
                                      cache_ssm_im[i].astype(F32).reshape(bs, SSM_WIDTH), *s5_w2, name="s5_sample")
            hp = hp3.reshape(bp * sp, D_MODEL)
            hs = hs3.reshape(bs * ts, D_MODEL)
            ssm_p_re.append(fpr.reshape(bp, SSM_GROUPS, SSM_STATE))
            ssm_p_im.append(fpi.reshape(bp, SSM_GROUPS, SSM_STATE))
            ssm_s_re.append(fsr.reshape(bs, SSM_GROUPS, SSM_STATE))
            ssm_s_im.append(fsi.reshape(bs, SSM_GROUPS, SSM_STATE))
        else:
            j = i - n_a
            wuq = w_uq[j].astype(F32)
            w_nope = wuq[:, :, :QK_NOPE].reshape(Q_LORA, MLA_HEADS * QK_NOPE)
            w_r = wuq[:, :, QK_NOPE:]
            w_rope = _pad_rope(w_r).reshape(Q_LORA, MLA_HEADS * ROPE_PAD)
            w_rope_rot = _pad_rope(_rot_cols(w_r)).reshape(Q_LORA, MLA_HEADS * ROPE_PAD)
            wuq_all = jnp.concatenate([w_nope, w_rope, w_rope_rot], axis=1).astype(BF16)
            ukt = w_uk.astype(F32).transpose(1, 2, 0).astype(BF16)
            wkr = w_kr.astype(F32)
            wkr2 = jnp.concatenate([_pad_rope(wkr), _pad_rope(_rot_cols(wkr))], axis=1).astype(BF16)
            pre_w = (_row(norm_mix_pre[i]), _row(kv_in_norm), w_dq[j].astype(BF16), _row(q_norm[j]), wuq_all, ukt,
                     w_dkv.astype(BF16), _row(kv_latent_norm), wkr2)
            cos_p, sin_p = _rope_tables(jnp.arange(sp, dtype=jnp.int32))
            cos_s, sin_s = _rope_tables(past_len + jnp.tile(jnp.arange(ts, dtype=jnp.int32), ROW_TILE // ts))
            qlat_p, qrope_p, kvcat_p, ckv_pj, kr_pj = _mla_pre(hp, cos_p, sin_p, *pre_w, BF16, name="mla_pre_prompt")
            qlat_s, qrope_s, _, ckv_sj, kr_sj = _mla_pre(hs, cos_s, sin_s, *pre_w, F32, name="mla_pre_sample")
            if i == n_a:
                ckv_p, kr_p, ckv_s, kr_s, kvcat_shared = ckv_pj, kr_pj, ckv_sj, kr_sj, kvcat_p
            o_p = _mla_attn_prompt(qlat_p, qrope_p, kvcat_shared, sp, bp)
            o_s = _mla_attn_sample(qlat_s, qrope_s, ckv_s, kr_s, cache_kv_latent.astype(F32),
                                   jnp.swapaxes(cache_k_rope.astype(F32), 1, 2), page_table, ts, bs)
            wuv = w_uv.astype(F32).transpose(1, 0, 2).astype(BF16)
            wo = w_o[j].astype(BF16)
            hp = _mla_out(o_p, wuv, wo, _row(norm_mix_post[i]), hp, name="mla_out_prompt")
            hs = _mla_out(o_s, wuv, wo, _row(norm_mix_post[i]), hs, name="mla_out_sample")

        wq = w_mem_q[i].astype(BF16)
        wo_mem = w_mem_o[i].astype(BF16)
        hp = _mem_attn_prompt(hp, sp, bp, i, _row(norm_mem_pre[i]), wq, mkb, mvb, wo_mem, _row(norm_mem_post[i]))
        q_s = _norm_matmul(hs, _row(norm_mem_pre[i]), wq, MEM_SCALE, F32, name="mem_q_sample")
        o_s = _mem_attn_sample(q_s, ts, bs, i, cmk, cmv)
        hs = _matmul_norm_res(o_s, wo_mem, _row(norm_mem_post[i]), hs, name="mem_out_sample")

        wup = w_mlp_up[i].astype(BF16)
        wdn = w_mlp_down[i].astype(BF16)
        hp = _mlp(hp, _row(norm_mlp_pre[i]), wup, wdn, _row(norm_mlp_post[i]), name="mlp_prompt")
        hs = _mlp(hs, _row(norm_mlp_pre[i]), wup, wdn, _row(norm_mlp_post[i]), name="mlp_sample")

    return (hp.reshape(bp, sp, D_MODEL), hs.reshape(bs, ts, D_MODEL),
            jnp.stack(ssm_p_re), jnp.stack(ssm_p_im), jnp.stack(ssm_s_re), jnp.stack(ssm_s_im),
            ckv_p.reshape(bp, sp, KV_LORA), kr_p[:, :QK_ROPE].reshape(bp, sp, QK_ROPE),
            ckv_s.reshape(bs, ts, KV_LORA), kr_s[:, :QK_ROPE].reshape(bs, ts, QK_ROPE),
            _mem_cache_unview(mk, bp), _mem_cache_unview(mv, bp))
```

```python
import functools
import math

import jax
import jax.numpy as jnp
from jax import lax
from jax.experimental import pallas as pl
from jax.experimental.pallas import tpu as pltpu

F32 = jnp.float32
BF16 = jnp.bfloat16

LANES = 128
SUBLANES = 8
D_MODEL = 1024
PAGE_SIZE = 128
SSM_GROUP = 16
SSM_GROUPS = D_MODEL // SSM_GROUP
SSM_STATE = 64
SSM_WIDTH = SSM_GROUPS * SSM_STATE
SSM_KBLOCKS = 4
SSM_SLABS = SSM_WIDTH // LANES
MLA_HEADS = 8
QK_NOPE = 128
QK_ROPE = 64
ROPE_PAD = LANES
V_HEAD = 128
KV_LORA = 256
Q_LORA = 384
ROPE_BASE = 10000.0
MLA_SCALE = (QK_NOPE + QK_ROPE) ** -0.5
MLA_Q_SCALE = MLA_SCALE * math.log2(math.e)
MEM_TOKENS = 256
MEM_HEADS = 4
MEM_HEAD_DIM = D_MODEL // MEM_HEADS
MEM_SCALE = MEM_HEAD_DIM ** -0.5
D_FF = 4 * D_MODEL
EPS = 1e-6
NEG_BIG = -1e30

VMEM_LIMIT_BYTES = 56 * 1024 * 1024

ROW_TILE = 512
WIDE_ROW_TILE = 1024
MLP_FF_CHUNK = 1024
S5_STEPS = 32
S5_PROMPT_SUBTILES = 2
S5_SAMPLE_BATCHES = 32
MLA_PROMPT_TILE = 512
MEM_SAMPLE_BATCHES = 4
PAGE_SLOTS = 3
SAMPLE_KEY_PARTS = 4


def _params(*semantics):
    return pltpu.CompilerParams(dimension_semantics=semantics, vmem_limit_bytes=VMEM_LIMIT_BYTES)


def _dot(a, b):
    return jnp.dot(a, b, preferred_element_type=F32)


def _dot_t(a, b):
    return lax.dot_general(a, b, (((1,), (1,)), ((), ())), preferred_element_type=F32)


def _rms(x, g):
    return x * lax.rsqrt(jnp.mean(x * x, axis=-1, keepdims=True) + EPS) * g


def _row(v):
    return v.reshape(1, -1).astype(F32)


def _lane_tile(x, width):
    return jnp.concatenate([x] * (width // LANES), axis=1) if width > LANES else x


def _block_diag_rows(x):
    rows, n = x.shape
    groups = rows // SSM_GROUP
    tiled = jnp.concatenate([x] * groups, axis=1)
    own = (lax.broadcasted_iota(jnp.int32, tiled.shape, 0) // SSM_GROUP
           == lax.broadcasted_iota(jnp.int32, tiled.shape, 1) // n)
    return jnp.where(own, tiled, 0.0)


def _ssm_disc_kernel(ar_ref, ai_ref, ldt_ref, br_ref, bi_ref, cr_ref, ci_ref,
                     abar_re_ref, abar_im_ref, bdre_ref, bdim_ref, cre_ref, ncim_ref):
    ar = ar_ref[...]
    ai = ai_ref[...]
    dt = jnp.exp(ldt_ref[...])
    mag = jnp.exp(ar * dt)
    abar_re = mag * jnp.cos(ai * dt)
    abar_im = mag * jnp.sin(ai * dt)
    den = ar * ar + ai * ai
    nr = abar_re - 1.0
    ni = abar_im
    coef_re = (nr * ar + ni * ai) / den
    coef_im = (ni * ar - nr * ai) / den
    br = br_ref[...]
    bi = bi_ref[...]
    abar_re_ref[...] = abar_re
    abar_im_ref[...] = abar_im
    bbar_re = coef_re * br - coef_im * bi
    bbar_im = coef_re * bi + coef_im * br
    kw = D_MODEL // SSM_KBLOCKS
    for kb in range(SSM_KBLOCKS):
        rws = slice(kb * kw, (kb + 1) * kw)
        bdre_ref[kb] = _block_diag_rows(bbar_re[rws]).astype(BF16)
        bdim_ref[kb] = _block_diag_rows(bbar_im[rws]).astype(BF16)
        cre_ref[kb] = _block_diag_rows(cr_ref[rws, :]).T.astype(BF16)
        ncim_ref[kb] = _block_diag_rows(-ci_ref[rws, :]).T.astype(BF16)


def _ssm_discretise(a_re, a_im, log_dt, b_re, b_im, c_re, c_im):
    g, n, k = SSM_GROUPS, SSM_STATE, SSM_GROUP
    rep = lambda a: jnp.repeat(a.astype(F32), k, axis=0)
    brt = b_re.astype(F32).transpose(0, 2, 1).reshape(g * k, n)
    bit = b_im.astype(F32).transpose(0, 2, 1).reshape(g * k, n)
    ldt = jnp.repeat(log_dt.astype(F32), k, axis=0).reshape(g * k, 1)
    kw = D_MODEL // SSM_KBLOCKS
    sw = SSM_WIDTH // SSM_KBLOCKS
    abar = jax.ShapeDtypeStruct((g * k, n), F32)
    bd = jax.ShapeDtypeStruct((SSM_KBLOCKS, kw, sw), BF16)
    bdc = jax.ShapeDtypeStruct((SSM_KBLOCKS, sw, kw), BF16)
    abar_re, abar_im, bdre, bdim, cre, ncim = pl.pallas_call(
        _ssm_disc_kernel, out_shape=(abar, abar, bd, bd, bdc, bdc), name="ssm_discretise",
    )(rep(a_re), rep(a_im), ldt, brt, bit,
      c_re.astype(F32).reshape(g * k, n), c_im.astype(F32).reshape(g * k, n))
    return abar_re[::k], abar_im[::k], bdre, bdim, cre, ncim


def _s5_kernel(x_ref, gpre_ref, bdre_ref, bdim_ref, are_ref, aim_ref, h0re_ref, h0im_ref,
               cre_ref, ncim_ref, dskip_ref, wglu_ref, gpost_ref,
               out_ref, fre_ref, fim_ref, *scratch, tt, bb, nsub, xpitch, rpitch):
    ti = pl.program_id(1)

    @pl.when(ti == 0)
    def _():
        fre_ref[...] = h0re_ref[...]
        fim_ref[...] = h0im_ref[...]

    for sub in range(nsub):
        _s5_subtile(x_ref, gpre_ref, bdre_ref, bdim_ref, are_ref, aim_ref, cre_ref, ncim_ref, dskip_ref,
                    wglu_ref, gpost_ref, out_ref, fre_ref, fim_ref, *scratch[4 * sub:4 * sub + 4],
                    t0=sub * tt, tt=tt, bb=bb, xpitch=xpitch, rpitch=rpitch)


def _s5_subtile(x_ref, gpre_ref, bdre_ref, bdim_ref, are_ref, aim_ref, cre_ref, ncim_ref, dskip_ref,
                wglu_ref, gpost_ref, out_ref, fre_ref, fim_ref, xs_ref, bure_ref, buim_ref, rs_ref,
                *, t0, tt, bb, xpitch, rpitch):
    rows = tt * bb
    nsl = D_MODEL // LANES
    spk = SSM_SLABS // SSM_KBLOCKS
    steps = slice(t0, t0 + tt)

    for b in range(bb):
        for s in range(nsl):
            xs_ref[s, b * xpitch:b * xpitch + tt, :] = x_ref[b, steps, s * LANES:(s + 1) * LANES]
    x = jnp.concatenate(
        [jnp.concatenate([xs_ref[s, pl.ds(t, bb, stride=xpitch), :] for s in range(nsl)], axis=1)
         for t in range(tt)], axis=0)
    xn = _rms(x, gpre_ref[...])
    xb = xn.astype(BF16)
    kw = D_MODEL // SSM_KBLOCKS
    sw = SSM_WIDTH // SSM_KBLOCKS

    def scan(slab):
        lanes = slice(slab * LANES, (slab + 1) * LANES)
        a_re = jnp.broadcast_to(are_ref[:, lanes], (SUBLANES, LANES))
        a_im = jnp.broadcast_to(aim_ref[:, lanes], (SUBLANES, LANES))
        for vb in range(bb // SUBLANES):
            b0 = vb * SUBLANES
            h_re = fre_ref[b0:b0 + SUBLANES, lanes]
            h_im = fim_ref[b0:b0 + SUBLANES, lanes]
            for t in range(tt):
                rws = slice(t * bb + b0, t * bb + b0 + SUBLANES)
                h_re, h_im = (a_re * h_re - a_im * h_im + bure_ref[rws, lanes],
                              a_re * h_im + a_im * h_re + buim_ref[rws, lanes])
                bure_ref[rws, lanes] = h_re
                buim_ref[rws, lanes] = h_im
            fre_ref[b0:b0 + SUBLANES, lanes] = h_re
            fim_ref[b0:b0 + SUBLANES, lanes] = h_im

    ys = []
    for kb in range(SSM_KBLOCKS):
        cols = slice(kb * sw, (kb + 1) * sw)
        xk = xb[:, kb * kw:(kb + 1) * kw]
        bure_ref[:, cols] = _dot(xk, bdre_ref[kb])
        buim_ref[:, cols] = _dot(xk, bdim_ref[kb])
        for s in range(spk):
            scan(kb * spk + s)
        ys.append(_dot(bure_ref[:, cols].astype(BF16), cre_ref[kb])
                  + _dot(buim_ref[:, cols].astype(BF16), ncim_ref[kb]))
    y = jnp.concatenate(ys, axis=1) + dskip_ref[...] * xn
    g = 0.5 * y * (1.0 + lax.erf(y * (2.0 ** -0.5)))
    z = _dot(g.astype(BF16), wglu_ref[...])
    o = z[:, :D_MODEL] * (1.0 / (1.0 + jnp.exp(-z[:, D_MODEL:])))
    r = _rms(o, gpost_ref[...])
    for s in range(nsl):
        for t in range(tt):
            rs_ref[s, t * rpitch:t * rpitch + bb, :] = r[t * bb:(t + 1) * bb, s * LANES:(s + 1) * LANES]
    for b in range(bb):
        rb = jnp.concatenate([rs_ref[s, pl.ds(b, tt, stride=rpitch), :] for s in range(nsl)], axis=1)
        out_ref[b, steps, :] = x_ref[b, steps, :] + rb


def _odd_pitch(n):
    groups = -(-n // SUBLANES)
    return SUBLANES * (groups if groups % 2 else groups + 1)


def _s5_layer(x3, tt, nsub, bb, gpre, bdre, bdim, a_re, a_im, h0_re, h0_im,
              cre, ncim, dskip, wglu, gpost, name):
    b_all, t_len, _ = x3.shape
    grid = (b_all // bb, t_len // (tt * nsub))
    xpitch, rpitch = _odd_pitch(tt), _odd_pitch(bb)
    nsl = D_MODEL // LANES
    bu = pltpu.VMEM((tt * bb, SSM_WIDTH), F32)
    subtile_scratch = [pltpu.VMEM((nsl, bb * xpitch, LANES), F32), bu, bu,
                       pltpu.VMEM((nsl, tt * rpitch, LANES), F32)]
    return pl.pallas_call(
        functools.partial(_s5_kernel, tt=tt, bb=bb, nsub=nsub, xpitch=xpitch, rpitch=rpitch),
        grid=grid,
        in_specs=[
            pl.BlockSpec((bb, tt * nsub, D_MODEL), lambda bj, ti: (bj, ti, 0)),
            _resident((1, D_MODEL)),
            _resident(bdre.shape),
            _resident(bdim.shape),
            _resident((1, SSM_WIDTH)),
            _resident((1, SSM_WIDTH)),
            pl.BlockSpec((bb, SSM_WIDTH), lambda bj, ti: (bj, 0)),
            pl.BlockSpec((bb, SSM_WIDTH), lambda bj, ti: (bj, 0)),
            _resident(cre.shape),
            _resident(ncim.shape),
            _resident((1, D_MODEL)),
            _resident(wglu.shape),
            _resident((1, D_MODEL)),
        ],
        out_specs=(
            pl.BlockSpec((bb, tt * nsub, D_MODEL), lambda bj, ti: (bj, ti, 0)),
            pl.BlockSpec((bb, SSM_WIDTH), lambda bj, ti: (bj, 0)),
            pl.BlockSpec((bb, SSM_WIDTH), lambda bj, ti: (bj, 0)),
        ),
        out_shape=(
            jax.ShapeDtypeStruct((b_all, t_len, D_MODEL), F32),
            jax.ShapeDtypeStruct((b_all, SSM_WIDTH), F32),
            jax.ShapeDtypeStruct((b_all, SSM_WIDTH), F32),
        ),
        scratch_shapes=subtile_scratch * nsub,
        compiler_params=_params("parallel", "arbitrary"),
        name=name,
    )(x3, gpre, bdre, bdim, a_re, a_im, h0_re, h0_im, cre, ncim, dskip, wglu, gpost)


def _mem_kv_kernel(x_ref, g_ref, wk_ref, wv_ref, k_ref, v_ref, kb_ref, vb_ref):
    xn = _rms(x_ref[...], g_ref[0]).astype(BF16)
    k = _dot(xn, wk_ref[0])
    v = _dot(xn, wv_ref[0])
    tm = k.shape[0]
    chunks = MEM_HEAD_DIM // LANES
    period = chunks * MEM_HEADS
    for hh in range(MEM_HEADS):
        for c in range(chunks):
            lanes = slice(hh * MEM_HEAD_DIM + c * LANES, hh * MEM_HEAD_DIM + (c + 1) * LANES)
            k_ref[0, pl.ds(c * MEM_HEADS + hh, tm, stride=period), :] = k[:, lanes]
            v_ref[0, pl.ds(c * MEM_HEADS + hh, tm, stride=period), :] = v[:, lanes]
    kb_ref[0] = k.astype(BF16)
    vb_ref[0] = v.astype(BF16)


def _mem_kv(mem, g, wk, wv):
    rows = mem.shape[0]
    nl = g.shape[0]
    tm = ROW_TILE
    period = D_MODEL // LANES
    blk = pl.BlockSpec((1, tm, D_MODEL), lambda l, i: (l, i, 0))
    nat = pl.BlockSpec((1, tm * period, LANES), lambda l, i: (l, i, 0))
    wspec = pl.BlockSpec((1, D_MODEL, D_MODEL), lambda l, i: (l, 0, 0))
    return pl.pallas_call(
        _mem_kv_kernel,
        grid=(nl, rows // tm),
        in_specs=[pl.BlockSpec((tm, D_MODEL), lambda l, i: (i, 0)),
                  pl.BlockSpec((1, 1, D_MODEL), lambda l, i: (l, 0, 0)), wspec, wspec],
        out_specs=(nat, nat, blk, blk),
        out_shape=(jax.ShapeDtypeStruct((nl, rows * period, LANES), F32),) * 2
        + (jax.ShapeDtypeStruct((nl, rows, D_MODEL), BF16),) * 2,
        compiler_params=_params("parallel", "parallel"),
        name="mem_kv",
    )(mem, g, wk, wv)


def _softmax_rows(s):
    m = jnp.max(s, axis=-1, keepdims=True)
    p = jnp.exp(s - m)
    return p / jnp.sum(p, axis=-1, keepdims=True)


def _row_halves(n):
    return (slice(0, n // 2), slice(n // 2, n))


def _mem_attn_prompt_kernel(h_ref, gpre_ref, wq_ref, k_ref, v_ref, wo_ref, gpost_ref, out_ref):
    for rws in _row_halves(h_ref.shape[0]):
        x = h_ref[rws, :]
        xn = _rms(x, gpre_ref[...]).astype(BF16)
        q = (_dot(xn, wq_ref[...]) * MEM_SCALE).astype(BF16)
        os = []
        for hh in range(MEM_HEADS):
            cols = slice(hh * MEM_HEAD_DIM, (hh + 1) * MEM_HEAD_DIM)
            p = _softmax_rows(_dot_t(q[:, cols], k_ref[0, :, cols]))
            os.append(_dot(p.astype(BF16), v_ref[0, :, cols]).astype(BF16))
        y = _dot(jnp.concatenate(os, axis=1), wo_ref[...])
        out_ref[rws, :] = x + _rms(y, gpost_ref[...])


def _mem_attn_prompt(h, t_len, b_all, layer, gpre, wq, kb, vb, wo, gpost):
    tq = WIDE_ROW_TILE
    nq = t_len // tq
    const2 = lambda b, i: (0, 0)
    hspec = pl.BlockSpec((tq, D_MODEL), lambda b, i: (b * nq + i, 0))
    kvspec = pl.BlockSpec((1, MEM_TOKENS, D_MODEL), lambda b, i: (layer, b, 0))
    wspec = pl.BlockSpec((D_MODEL, D_MODEL), const2)
    gspec = pl.BlockSpec((1, D_MODEL), const2)
    return pl.pallas_call(
        _mem_attn_prompt_kernel,
        grid=(b_all, nq),
        in_specs=[hspec, gspec, wspec, kvspec, kvspec, wspec, gspec],
        out_specs=hspec,
        out_shape=jax.ShapeDtypeStruct(h.shape, F32),
        compiler_params=_params("parallel", "parallel"),
        name="mem_attn_prompt",
    )(h, gpre, wq, kb, vb, wo, gpost)


def _mem_cache_view(cache):
    nl, nb, m, nh, dh = cache.shape
    chunks = dh // LANES
    return cache.reshape(nl, nb, m, nh, chunks, LANES).transpose(0, 1, 2, 4, 3, 5).reshape(
        nl, nb, m * chunks * nh, LANES)


def _mem_cache_unview(x, nb):
    nl = x.shape[0]
    chunks = MEM_HEAD_DIM // LANES
    return x.reshape(nl, nb, MEM_TOKENS, chunks, MEM_HEADS, LANES).transpose(0, 1, 2, 4, 3, 5).reshape(
        nl, nb, MEM_TOKENS, MEM_HEADS, MEM_HEAD_DIM)


def _mem_attn_sample_kernel(q_ref, k_ref, v_ref, o_ref, *, nb, ts):
    chunks = MEM_HEAD_DIM // LANES
    period = chunks * MEM_HEADS

    def head(ref, g, hh):
        parts = [ref[0, g, pl.ds(c * MEM_HEADS + hh, MEM_TOKENS, stride=period), :] for c in range(chunks)]
        return jnp.concatenate(parts, axis=1).astype(BF16)

    pairs = [(g, hh) for g in range(nb) for hh in range(MEM_HEADS)]
    ss = [_dot_t(q_ref[g * ts:(g + 1) * ts, hh * MEM_HEAD_DIM:(hh + 1) * MEM_HEAD_DIM].astype(BF16),
                 head(k_ref, g, hh)) for g, hh in pairs]
    p = _softmax_rows(jnp.concatenate(ss, axis=0))
    for i, (g, hh) in enumerate(pairs):
        o_ref[g * ts:(g + 1) * ts, hh * MEM_HEAD_DIM:(hh + 1) * MEM_HEAD_DIM] = _dot(
            p[i * ts:(i + 1) * ts].astype(BF16), head(v_ref, g, hh))


def _mem_attn_sample(q, ts, b_all, layer, cache_k, cache_v):
    nb = MEM_SAMPLE_BATCHES
    qspec = pl.BlockSpec((nb * ts, D_MODEL), lambda j: (j, 0))
    cspec = pl.BlockSpec((1, nb) + cache_k.shape[2:], lambda j: (layer, j, 0, 0))
    return pl.pallas_call(
        functools.partial(_mem_attn_sample_kernel, nb=nb, ts=ts),
        grid=(b_all // nb,),
        in_specs=[qspec, cspec, cspec],
        out_specs=qspec,
        out_shape=jax.ShapeDtypeStruct(q.shape, F32),
        compiler_params=_params("parallel"),
        name="mem_attn_sample",
    )(q, cache_k, cache_v)


def _norm_matmul_kernel(x_ref, g_ref, w_ref, o_ref, *, scale):
    xn = _rms(x_ref[...], g_ref[...]).astype(BF16)
    o_ref[...] = (_dot(xn, w_ref[...]) * scale).astype(o_ref.dtype)


def _norm_matmul(x, g, w, scale, out_dtype, name):
    rows = x.shape[0]
    tm = ROW_TILE
    n = w.shape[1]
    return pl.pallas_call(
        functools.partial(_norm_matmul_kernel, scale=scale),
        grid=(rows // tm,),
        in_specs=[pl.BlockSpec((tm, D_MODEL), lambda i: (i, 0)),
                  pl.BlockSpec((1, D_MODEL), lambda i: (0, 0)),
                  pl.BlockSpec(w.shape, lambda i: (0, 0))],
        out_specs=pl.BlockSpec((tm, n), lambda i: (i, 0)),
        out_shape=jax.ShapeDtypeStruct((rows, n), out_dtype),
        compiler_params=_params("parallel"),
        name=name,
    )(x, g, w)


def _matmul_norm_res_kernel(a_ref, w_ref, g_ref, h_ref, o_ref):
    y = _dot(a_ref[...].astype(BF16), w_ref[...])
    o_ref[...] = h_ref[...] + _rms(y, g_ref[...])


def _matmul_norm_res(a, w, g, h, name):
    rows, k = a.shape
    tm = ROW_TILE
    return pl.pallas_call(
        _matmul_norm_res_kernel,
        grid=(rows // tm,),
        in_specs=[pl.BlockSpec((tm, k), lambda i: (i, 0)),
                  pl.BlockSpec(w.shape, lambda i: (0, 0)),
                  pl.BlockSpec((1, D_MODEL), lambda i: (0, 0)),
                  pl.BlockSpec((tm, D_MODEL), lambda i: (i, 0))],
        out_specs=pl.BlockSpec((tm, D_MODEL), lambda i: (i, 0)),
        out_shape=jax.ShapeDtypeStruct((rows, D_MODEL), F32),
        compiler_params=_params("parallel"),
        name=name,
    )(a, w, g, h)


def _mlp_kernel(x_ref, gpre_ref, wup_ref, wdn_ref, gpost_ref, o_ref, *, tf):
    x = x_ref[...]
    xn = _rms(x, gpre_ref[...]).astype(BF16)
    acc = None
    for c in range(D_FF // tf):
        hid = jnp.maximum(_dot(xn, wup_ref[:, c * tf:(c + 1) * tf]), 0.0)
        part = _dot((hid * hid).astype(BF16), wdn_ref[c * tf:(c + 1) * tf, :])
        acc = part if acc is None else acc + part
    o_ref[...] = x + _rms(acc, gpost_ref[...])


def _resident(shape):
    return pl.BlockSpec(shape, lambda *_: (0,) * len(shape), pipeline_mode=pl.Buffered(1))


def _mlp(h, gpre, wup, wdn, gpost, name):
    rows = h.shape[0]
    tm, tf = ROW_TILE, MLP_FF_CHUNK
    return pl.pallas_call(
        functools.partial(_mlp_kernel, tf=tf),
        grid=(rows // tm,),
        in_specs=[pl.BlockSpec((tm, D_MODEL), lambda i: (i, 0)),
                  _resident((1, D_MODEL)), _resident(wup.shape), _resident(wdn.shape), _resident((1, D_MODEL))],
        out_specs=pl.BlockSpec((tm, D_MODEL), lambda i: (i, 0)),
        out_shape=jax.ShapeDtypeStruct((rows, D_MODEL), F32),
        compiler_params=_params("parallel"),
        name=name,
    )(h, gpre, wup, wdn, gpost)


def _mla_pre_kernel(h_ref, cos_ref, sin_ref, gmix_ref, gkv_ref, wdq_ref, qn_ref, wuq_ref, ukt_ref,
                    wdkv_ref, glat_ref, wkr_ref,
                    qlat_ref, qrope_ref, kvcat_ref, ckv_ref, kr_ref, *, kr_transposed):
    x = h_ref[...]
    y = x * lax.rsqrt(jnp.mean(x * x, axis=-1, keepdims=True) + EPS)
    cos = cos_ref[...]
    sin = sin_ref[...]
    hn = (y * gkv_ref[...]).astype(BF16)
    ckv = _rms(_dot(hn, wdkv_ref[...]), glat_ref[...])
    kr2 = _dot(hn, wkr_ref[...])
    kr = kr2[:, :ROPE_PAD] * cos + kr2[:, ROPE_PAD:] * sin
    ckv_ref[...] = ckv
    if kr_transposed:
        kr_ref[0] = kr.T[:QK_ROPE]
    else:
        kr_ref[...] = kr
    kvcat_ref[...] = jnp.concatenate([ckv, kr], axis=1).astype(BF16)
    xn = (y * gmix_ref[...]).astype(BF16)
    cq = _rms(_dot(xn, wdq_ref[...]), qn_ref[...]).astype(BF16)
    q = _dot(cq, wuq_ref[...])
    nope_w = MLA_HEADS * QK_NOPE
    rope_w = MLA_HEADS * ROPE_PAD
    qr = (q[:, nope_w:nope_w + rope_w] * _lane_tile(cos, rope_w)
          + q[:, nope_w + rope_w:] * _lane_tile(sin, rope_w))
    qrope_ref[...] = (qr * MLA_Q_SCALE).astype(qrope_ref.dtype)
    for hh in range(MLA_HEADS):
        qn_h = q[:, hh * QK_NOPE:(hh + 1) * QK_NOPE].astype(BF16)
        ql = _dot(qn_h, ukt_ref[hh])
        qlat_ref[:, hh * KV_LORA:(hh + 1) * KV_LORA] = (ql * MLA_Q_SCALE).astype(qlat_ref.dtype)


def _mla_pre(h, cos, sin, gmix, gkv, wdq, qn, wuq, ukt, wdkv, glat, wkr, q_dtype, seq_len, name):
    rows = h.shape[0]
    tm = ROW_TILE
    period = cos.shape[0] // tm
    row = lambda w: pl.BlockSpec((tm, w), lambda i: (i, 0))
    tab = pl.BlockSpec((tm, ROPE_PAD), lambda i: (i % period, 0))
    full = lambda a: pl.BlockSpec(a.shape, lambda i: (0,) * a.ndim)
    if seq_len is None:
        kr_spec, kr_shape = row(ROPE_PAD), jax.ShapeDtypeStruct((rows, ROPE_PAD), F32)
    else:
        per_seq = seq_len // tm
        kr_spec = pl.BlockSpec((1, QK_ROPE, tm), lambda i: (i // per_seq, 0, i % per_seq))
        kr_shape = jax.ShapeDtypeStruct((rows // seq_len, QK_ROPE, seq_len), F32)
    return pl.pallas_call(
        functools.partial(_mla_pre_kernel, kr_transposed=seq_len is not None),
        grid=(rows // tm,),
        in_specs=[row(D_MODEL), tab, tab, full(gmix), full(gkv), full(wdq), full(qn),
                  full(wuq), full(ukt), full(wdkv), full(glat), full(wkr)],
        out_specs=(row(MLA_HEADS * KV_LORA), row(MLA_HEADS * ROPE_PAD), row(KV_LORA + ROPE_PAD),
                   row(KV_LORA), kr_spec),
        out_shape=(jax.ShapeDtypeStruct((rows, MLA_HEADS * KV_LORA), q_dtype),
                   jax.ShapeDtypeStruct((rows, MLA_HEADS * ROPE_PAD), q_dtype),
                   jax.ShapeDtypeStruct((rows, KV_LORA + ROPE_PAD), BF16),
                   jax.ShapeDtypeStruct((rows, KV_LORA), F32),
                   kr_shape),
        compiler_params=_params("parallel"),
        name=name,
    )(h, cos, sin, gmix, gkv, wdq, qn, wuq, ukt, wdkv, glat, wkr)


def _mla_attn_prompt_kernel(qlat_ref, qrope_ref, kv_ref, o_ref, m_ref, l_ref, acc_ref, *, tq):
    qi = pl.program_id(1)
    m_ref[...] = jnp.full_like(m_ref, NEG_BIG)
    l_ref[...] = jnp.zeros_like(l_ref)
    acc_ref[...] = jnp.zeros_like(acc_ref)

    def block(j, diagonal):
        k0 = pl.multiple_of(j * tq, tq)
        ck = kv_ref[pl.ds(k0, tq), :KV_LORA]
        kr = kv_ref[pl.ds(k0, tq), KV_LORA:]
        for hh in range(MLA_HEADS):
            s = (_dot_t(qlat_ref[:, hh * KV_LORA:(hh + 1) * KV_LORA], ck)
                 + _dot_t(qrope_ref[:, hh * ROPE_PAD:(hh + 1) * ROPE_PAD], kr))
            if diagonal:
                visible = (lax.broadcasted_iota(jnp.int32, (tq, tq), 1)
                           <= lax.broadcasted_iota(jnp.int32, (tq, tq), 0))
                s = jnp.where(visible, s, NEG_BIG)
            m_old = m_ref[hh]
            m_new = jnp.maximum(m_old, jnp.max(s, axis=-1, keepdims=True))
            alpha = jnp.exp2(m_old - m_new)
            p = jnp.exp2(s - _lane_tile(m_new, tq))
            l_ref[hh] = alpha * l_ref[hh] + jnp.sum(p, axis=-1, keepdims=True)
            acc_ref[hh] = _lane_tile(alpha, KV_LORA) * acc_ref[hh] + _dot(p.astype(BF16), ck)
            m_ref[hh] = m_new

    def body(j, _):
        block(j, False)
        return 0

    lax.fori_loop(0, qi, body, 0)
    block(qi, True)
    for hh in range(MLA_HEADS):
        o = acc_ref[hh] / _lane_tile(l_ref[hh], KV_LORA)
        o_ref[:, hh * KV_LORA:(hh + 1) * KV_LORA] = o.astype(BF16)


def _mla_attn_prompt(qlat, qrope, kvcat, t_len, b_all):
    tq = MLA_PROMPT_TILE
    nq = t_len // tq
    lw, rw, kw = MLA_HEADS * KV_LORA, MLA_HEADS * ROPE_PAD, KV_LORA + ROPE_PAD
    return pl.pallas_call(
        functools.partial(_mla_attn_prompt_kernel, tq=tq),
        grid=(b_all, nq),
        in_specs=[pl.BlockSpec((tq, lw), lambda b, i: (b * nq + i, 0)),
                  pl.BlockSpec((tq, rw), lambda b, i: (b * nq + i, 0)),
                  pl.BlockSpec((t_len, kw), lambda b, i: (b, 0))],
        out_specs=pl.BlockSpec((tq, lw), lambda b, i: (b * nq + i, 0)),
        out_shape=jax.ShapeDtypeStruct((b_all * t_len, lw), BF16),
        scratch_shapes=[pltpu.VMEM((MLA_HEADS, tq, LANES), F32), pltpu.VMEM((MLA_HEADS, tq, LANES), F32),
                        pltpu.VMEM((MLA_HEADS, tq, KV_LORA), F32)],
        compiler_params=_params("parallel", "parallel"),
        name="mla_attn_prompt",
    )(qlat, qrope, kvcat)


def _stack_heads(ref, width):
    return jnp.concatenate([ref[:, hh * width:(hh + 1) * width] for hh in range(MLA_HEADS)], axis=0)


def _mla_attn_sample_kernel(pt_ref, qlat_ref, qrope_ref, ckvn_ref, krn_ref, kv_hbm, kr_hbm,
                            o_ref, kvbuf, krbuf, sem, s_ref, kvb_ref, *, ts, n_pages, parts):
    slots = kvbuf.shape[0]
    ahead = slots - 1
    b = pl.program_id(0)
    slot = b % slots
    rows = MLA_HEADS * ts
    past = n_pages * PAGE_SIZE

    def page_copies(batch, sl, page_of):
        cps = []
        for p in range(n_pages):
            page = page_of(batch, p)
            cps.append(pltpu.make_async_copy(kv_hbm.at[page], kvbuf.at[sl, p], sem.at[0, sl]))
            cps.append(pltpu.make_async_copy(kr_hbm.at[page], krbuf.at[sl, p], sem.at[1, sl]))
        return cps

    def start_all(cps, first=0):
        for i, cp in enumerate(cps, start=first):
            cp.start(priority=(i // 2) % 2)

    table_page = lambda batch, p: pt_ref[batch * n_pages + p]
    last = pl.num_programs(0) - 1

    any_page = lambda batch, p: 0

    @pl.when(b == 0)
    def _():
        for d in range(ahead):
            start_all(page_copies(jnp.minimum(d, last), d, table_page))

    for cp in page_copies(b, slot, any_page):
        cp.wait()

    prefetch = page_copies(jnp.minimum(b + ahead, last), (b + ahead) % slots, table_page)
    per_pair = len(prefetch) // (n_pages // 2)

    ql = _stack_heads(qlat_ref, KV_LORA).astype(BF16)
    qr = _stack_heads(qrope_ref, ROPE_PAD).astype(BF16)
    qr_cache = qr[:, :QK_ROPE]
    pair = 2 * PAGE_SIZE
    pairs_per_part = n_pages // 2 // parts
    span = pairs_per_part * pair
    ms, ls, os = [], [], []
    for part in range(parts):
        for g in range(part * pairs_per_part, (part + 1) * pairs_per_part):
            ck = kvbuf[slot, 2 * g:2 * g + 2].reshape(pair, KV_LORA).astype(BF16)
            kr = jnp.concatenate([krbuf[slot, 2 * g], krbuf[slot, 2 * g + 1]], axis=1).astype(BF16)
            s_ref[:, g * pair:(g + 1) * pair] = _dot_t(ql, ck) + _dot(qr_cache, kr)
            kvb_ref[g * pair:(g + 1) * pair, :] = ck
            start_all(prefetch[g * per_pair:(g + 1) * per_pair], first=g * per_pair)
        hi = (part + 1) * span
        if part == parts - 1:
            pad = PAGE_SIZE - ts
            ckn = jnp.concatenate([ckvn_ref[...], jnp.zeros((pad, KV_LORA), F32)], axis=0).astype(BF16)
            krn = jnp.concatenate([krn_ref[...], jnp.zeros((pad, ROPE_PAD), F32)], axis=0).astype(BF16)
            s_new = _dot_t(ql, ckn) + _dot_t(qr, krn)
            t_q = lax.broadcasted_iota(jnp.int32, (rows, PAGE_SIZE), 0) & (ts - 1)
            t_k = lax.broadcasted_iota(jnp.int32, (rows, PAGE_SIZE), 1)
            s_ref[:, past:] = jnp.where(t_k <= t_q, s_new, NEG_BIG)
            kvb_ref[past:, :] = ckn
            hi = past + PAGE_SIZE
        s = s_ref[:, part * span:hi]
        m = jnp.max(s, axis=-1, keepdims=True)
        p = jnp.exp2(s - m)
        ms.append(m)
        ls.append(jnp.sum(p, axis=-1, keepdims=True))
        os.append(_dot(p.astype(BF16), kvb_ref[part * span:hi, :]))
    m_all = functools.reduce(jnp.maximum, ms)
    ws = [jnp.exp2(m - m_all) for m in ms]
    o = sum(w * o_part for w, o_part in zip(ws, os)) / sum(w * l for w, l in zip(ws, ls))
    for hh in range(MLA_HEADS):
        o_ref[:, hh * KV_LORA:(hh + 1) * KV_LORA] = o[hh * ts:(hh + 1) * ts]

    @pl.when(b == last)
    def _():
        for d in range(1, slots):
            for cp in page_copies(b, (b + d) % slots, any_page):
                cp.wait()


def _mla_attn_sample(qlat, qrope, ckv_new, kr_new, cache_kv, cache_kr_t, page_table, ts, b_all):
    slots = PAGE_SLOTS
    n_pages = page_table.shape[1]
    lw, rw = MLA_HEADS * KV_LORA, MLA_HEADS * ROPE_PAD
    rows = MLA_HEADS * ts
    keys = (n_pages + 1) * PAGE_SIZE
    pt = page_table.reshape(-1)
    stream = lambda w: pl.BlockSpec((ts, w), lambda b, pt_ref: (b, 0))
    hbm = pl.BlockSpec(memory_space=pl.ANY)
    grid_spec = pltpu.PrefetchScalarGridSpec(
        num_scalar_prefetch=1,
        grid=(b_all,),
        in_specs=[stream(lw), stream(rw), stream(KV_LORA), stream(ROPE_PAD), hbm, hbm],
        out_specs=stream(lw),
        scratch_shapes=[pltpu.VMEM((slots, n_pages, PAGE_SIZE, KV_LORA), F32),
                        pltpu.VMEM((slots, n_pages, QK_ROPE, PAGE_SIZE), F32),
                        pltpu.SemaphoreType.DMA((2, slots)),
                        pltpu.VMEM((rows, keys), F32),
                        pltpu.VMEM((keys, KV_LORA), BF16)],
    )
    return pl.pallas_call(
        functools.partial(_mla_attn_sample_kernel, ts=ts, n_pages=n_pages, parts=SAMPLE_KEY_PARTS),
        grid_spec=grid_spec,
        out_shape=jax.ShapeDtypeStruct((b_all * ts, lw), F32),
        compiler_params=_params("arbitrary"),
        name="mla_attn_sample",
    )(pt, qlat, qrope, ckv_new, kr_new, cache_kv, cache_kr_t)


def _mla_out_kernel(o_ref, wuv_ref, wo_ref, g_ref, h_ref, out_ref):
    for rws in _row_halves(h_ref.shape[0]):
        vs = []
        for hh in range(MLA_HEADS):
            vs.append(_dot(o_ref[rws, hh * KV_LORA:(hh + 1) * KV_LORA].astype(BF16), wuv_ref[hh]).astype(BF16))
        y = _dot(jnp.concatenate(vs, axis=1), wo_ref[...])
        out_ref[rws, :] = h_ref[rws, :] + _rms(y, g_ref[...])


def _mla_out(o, wuv, wo, g, h, name):
    rows = h.shape[0]
    tm = WIDE_ROW_TILE
    lw = MLA_HEADS * KV_LORA
    return pl.pallas_call(
        _mla_out_kernel,
        grid=(rows // tm,),
        in_specs=[pl.BlockSpec((tm, lw), lambda i: (i, 0)),
                  pl.BlockSpec(wuv.shape, lambda i: (0, 0, 0)),
                  pl.BlockSpec(wo.shape, lambda i: (0, 0)),
                  pl.BlockSpec((1, D_MODEL), lambda i: (0, 0)),
                  pl.BlockSpec((tm, D_MODEL), lambda i: (i, 0))],
        out_specs=pl.BlockSpec((tm, D_MODEL), lambda i: (i, 0)),
        out_shape=jax.ShapeDtypeStruct((rows, D_MODEL), F32),
        compiler_params=_params("parallel"),
        name=name,
    )(o, wuv, wo, g, h)


def _rope_tables(pos):
    half = QK_ROPE // 2
    inv = ROPE_BASE ** (-jnp.arange(half, dtype=F32) / half)
    ang = pos.astype(F32)[:, None] * inv[None, :]
    cos = jnp.cos(ang)
    sin = jnp.sin(ang)
    zeros = jnp.zeros((pos.shape[0], ROPE_PAD - QK_ROPE), F32)
    return jnp.concatenate([cos, cos, zeros], axis=1), jnp.concatenate([sin, sin, zeros], axis=1)


def _rot_cols(w):
    half = QK_ROPE // 2
    return jnp.concatenate([-w[..., half:], w[..., :half]], axis=-1)


def _pad_rope(w):
    pad = [(0, 0)] * (w.ndim - 1) + [(0, ROPE_PAD - QK_ROPE)]
    return jnp.pad(w, pad)


def kernel(x_prompt, x_sample, cache_ssm_re, cache_ssm_im, cache_kv_latent, cache_k_rope, cache_mem_k, cache_mem_v, page_table, mem_prompt, norm_mix_pre, norm_mix_post, norm_mem_pre, norm_mem_post, norm_mlp_pre, norm_mlp_post, mem_in_norm, w_mem_q, w_mem_k, w_mem_v, w_mem_o, w_mlp_up, w_mlp_down, ssm_a_re, ssm_a_im, ssm_log_dt, ssm_b_re, ssm_b_im, ssm_c_re, ssm_c_im, ssm_d, w_glu, kv_in_norm, w_dkv, kv_latent_norm, w_kr, w_uk, w_uv, w_dq, q_norm, w_uq, w_o):
    depth = norm_mix_pre.shape[0]
    n_a = ssm_a_re.shape[0]
    bp, sp, _ = x_prompt.shape
    bs, ts, _ = x_sample.shape
    past_len = page_table.shape[1] * PAGE_SIZE

    hp = x_prompt.astype(F32).reshape(bp * sp, D_MODEL)
    hs = x_sample.astype(F32).reshape(bs * ts, D_MODEL)

    mem_rows = mem_prompt.astype(F32).reshape(bp * MEM_TOKENS, D_MODEL)
    mk, mv, mkb, mvb = _mem_kv(mem_rows, mem_in_norm.astype(F32).reshape(depth, 1, D_MODEL),
                               w_mem_k.astype(BF16), w_mem_v.astype(BF16))
    cmk = _mem_cache_view(cache_mem_k.astype(F32))
    cmv = _mem_cache_view(cache_mem_v.astype(F32))

    ssm_p_re, ssm_p_im, ssm_s_re, ssm_s_im = [], [], [], []
    ckv_p = kr_p = ckv_s = kr_s = None
    for i in range(depth):
        if i < n_a:
            abar_re, abar_im, bdre, bdim, cre, ncim = _ssm_discretise(
                ssm_a_re[i], ssm_a_im[i], ssm_log_dt[i], ssm_b_re[i], ssm_b_im[i], ssm_c_re[i], ssm_c_im[i])
            s5_w = (_row(norm_mix_pre[i]), bdre, bdim, abar_re.reshape(1, SSM_WIDTH), abar_im.reshape(1, SSM_WIDTH))
            s5_w2 = (cre, ncim, _row(ssm_d[i]), w_glu[i].astype(BF16), _row(norm_mix_post[i]))
            zeros = jnp.zeros((bp, SSM_WIDTH), F32)
            hp3, fpr, fpi = _s5_layer(hp.reshape(bp, sp, D_MODEL), S5_STEPS, S5_PROMPT_SUBTILES, bp, *s5_w, zeros, zeros, *s5_w2,
                                      name="s5_prompt")
            hs3, fsr, fsi = _s5_layer(hs.reshape(bs, ts, D_MODEL), ts, 1, S5_SAMPLE_BATCHES, *s5_w,
                                      cache_ssm_re[i].astype(F32).reshape(bs, SSM_WIDTH),
                                      cache_ssm_im[i].astype(F32).reshape(bs, SSM_WIDTH), *s5_w2, name="s5_sample")
            hp = hp3.reshape(bp * sp, D_MODEL)
            hs = hs3.reshape(bs * ts, D_MODEL)
            ssm_p_re.append(fpr.reshape(bp, SSM_GROUPS, SSM_STATE))
            ssm_p_im.append(fpi.reshape(bp, SSM_GROUPS, SSM_STATE))
            ssm_s_re.append(fsr.reshape(bs, SSM_GROUPS, SSM_STATE))
            ssm_s_im.append(fsi.reshape(bs, SSM_GROUPS, SSM_STATE))
        else:
            j = i - n_a
            wuq = w_uq[j].astype(F32)
            w_nope = wuq[:, :, :QK_NOPE].reshape(Q_LORA, MLA_HEADS * QK_NOPE)
            w_r = wuq[:, :, QK_NOPE:]
            w_rope = _pad_rope(w_r).reshape(Q_LORA, MLA_HEADS * ROPE_PAD)
            w_rope_rot = _pad_rope(_rot_cols(w_r)).reshape(Q_LORA, MLA_HEADS * ROPE_PAD)
            wuq_all = jnp.concatenate([w_nope, w_rope, w_rope_rot], axis=1).astype(BF16)
            ukt = w_uk.astype(F32).transpose(1, 2, 0).astype(BF16)
            wkr = w_kr.astype(F32)
            wkr2 = jnp.concatenate([_pad_rope(wkr), _pad_rope(_rot_cols(wkr))], axis=1).astype(BF16)
            pre_w = (_row(norm_mix_pre[i]), _row(kv_in_norm), w_dq[j].astype(BF16), _row(q_norm[j]), wuq_all, ukt,
                     w_dkv.astype(BF16), _row(kv_latent_norm), wkr2)
            cos_p, sin_p = _rope_tables(jnp.arange(sp, dtype=jnp.int32))
            cos_s, sin_s = _rope_tables(past_len + jnp.tile(jnp.arange(ts, dtype=jnp.int32), ROW_TILE // ts))
            qlat_p, qrope_p, kvcat_p, ckv_pj, kr_pj = _mla_pre(hp, cos_p, sin_p, *pre_w, BF16, sp, name="mla_pre_prompt")
            qlat_s, qrope_s, _, ckv_sj, kr_sj = _mla_pre(hs, cos_s, sin_s, *pre_w, F32, None, name="mla_pre_sample")
            if i == n_a:
                ckv_p, kr_p, ckv_s, kr_s, kvcat_shared = ckv_pj, kr_pj, ckv_sj, kr_sj, kvcat_p
            o_p = _mla_attn_prompt(qlat_p, qrope_p, kvcat_shared, sp, bp)
            o_s = _mla_attn_sample(qlat_s, qrope_s, ckv_s, kr_s, cache_kv_latent.astype(F32),
                                   jnp.swapaxes(cache_k_rope.astype(F32), 1, 2), page_table, ts, bs)
            wuv = w_uv.astype(F32).transpose(1, 0, 2).astype(BF16)
            wo = w_o[j].astype(BF16)
            hp = _mla_out(o_p, wuv, wo, _row(norm_mix_post[i]), hp, name="mla_out_prompt")
            hs = _mla_out(o_s, wuv, wo, _row(norm_mix_post[i]), hs, name="mla_out_sample")

        wq = w_mem_q[i].astype(BF16)
        wo_mem = w_mem_o[i].astype(BF16)
        hp = _mem_attn_prompt(hp, sp, bp, i, _row(norm_mem_pre[i]), wq, mkb, mvb, wo_mem, _row(norm_mem_post[i]))
        q_s = _norm_matmul(hs, _row(norm_mem_pre[i]), wq, MEM_SCALE, F32, name="mem_q_sample")
        o_s = _mem_attn_sample(q_s, ts, bs, i, cmk, cmv)
        hs = _matmul_norm_res(o_s, wo_mem, _row(norm_mem_post[i]), hs, name="mem_out_sample")

        wup = w_mlp_up[i].astype(BF16)
        wdn = w_mlp_down[i].astype(BF16)
        hp = _mlp(hp, _row(norm_mlp_pre[i]), wup, wdn, _row(norm_mlp_post[i]), name="mlp_prompt")
        hs = _mlp(hs, _row(norm_mlp_pre[i]), wup, wdn, _row(norm_mlp_post[i]), name="mlp_sample")

    return (hp.reshape(bp, sp, D_MODEL), hs.reshape(bs, ts, D_MODEL),
            jnp.stack(ssm_p_re), jnp.stack(ssm_p_im), jnp.stack(ssm_s_re), jnp.stack(ssm_s_im),
            ckv_p.reshape(bp, sp, KV_LORA), jnp.swapaxes(kr_p, 1, 2),
            ckv_s.reshape(bs, ts, KV_LORA), kr_s[:, :QK_ROPE].reshape(bs, ts, QK_ROPE),
            _mem_cache_unview(mk, bp), _mem_cache_unview(mv, bp))
```
